```python
import math
import jax, jax.numpy as jnp
from jax import lax
import numpy as np

D_MODEL = 2048
BATCH = 1
SEQ = 8192
DEPTH = 4

MIXER_ORDER = ("mla", "hyena", "diff")
N_MIXERS = len(MIXER_ORDER)

RMS_EPS = 1e-6
ROPE_THETA = 10000.0
Q_BLOCK = 128

MLA_HEADS = 16
MLA_Q_LORA = 768
MLA_KV_LORA = 512
MLA_NOPE = 128
MLA_ROPE = 64
MLA_V = 128
MLA_SCALE = (MLA_NOPE + MLA_ROPE) ** -0.5

HY_ORDER = 2
HY_EMB_DIM = 33
HY_FILTER_WIDTH = 64
HY_SHORT = 3
HY_TARGET = 1e-2
HY_FAST_PCT = 0.3
HY_SLOW_PCT = 1.5
HY_MAX_DECAY = math.log(HY_TARGET) / HY_FAST_PCT
HY_MIN_DECAY = math.log(HY_TARGET) / HY_SLOW_PCT
HY_SHIFT = 0.0

DIFF_HEAD_DIM = 128
DIFF_HEADS = D_MODEL // (2 * DIFF_HEAD_DIM)
DIFF_QK = DIFF_HEADS * 2 * DIFF_HEAD_DIM
DIFF_SCALE = DIFF_HEAD_DIM ** -0.5
DIFF_EPS = 1e-5

D_FF = -(-8 * D_MODEL // (3 * 256)) * 256

kernel_name = "hybrid_mla_hyena_diffattn_encoder"


def _rms_norm(x, g, eps=RMS_EPS):
    xf = x.astype(jnp.float32)
    y = xf * lax.rsqrt(jnp.mean(xf * xf, axis=-1, keepdims=True) + eps)
    return (y * g.astype(jnp.float32)).astype(x.dtype)


def _rope_tables(seq, dim):
    inv = 1.0 / (ROPE_THETA ** (jnp.arange(0, dim, 2, dtype=jnp.float32) / dim))
    ang = jnp.arange(seq, dtype=jnp.float32)[:, None] * inv[None, :]
    return jnp.cos(ang), jnp.sin(ang)


def _apply_rope(x, cos, sin):
    shape = (x.shape[1],) + (1,) * (x.ndim - 3) + (cos.shape[-1],)
    c = cos.reshape(shape).astype(x.dtype)
    s = sin.reshape(shape).astype(x.dtype)
    x1, x2 = jnp.split(x, 2, axis=-1)
    return jnp.concatenate([x1 * c - x2 * s, x1 * s + x2 * c], axis=-1)


def _blocked_attention(q, k, v, scale):
    b, h, s, dk = q.shape
    nb = s // Q_BLOCK
    qb = q.reshape(b, h, nb, Q_BLOCK, dk).transpose(2, 0, 1, 3, 4)

    def block(qblk):
        sc = jnp.einsum("bhqd,bhkd->bhqk", qblk, k).astype(jnp.float32) * scale
        p = jax.nn.softmax(sc, axis=-1).astype(v.dtype)
        return jnp.einsum("bhqk,bhkd->bhqd", p, v)

    o = lax.map(block, qb)
    return o.transpose(1, 2, 0, 3, 4).reshape(b, h, s, v.shape[-1])


def _mla(xn, w_in, q_norm, w_uq, kv_norm, w_ukv, w_o, cos, sin):
    b, s, _ = xn.shape
    hcat = xn @ w_in
    c_q = hcat[..., :MLA_Q_LORA]
    c_kv = hcat[..., MLA_Q_LORA:MLA_Q_LORA + MLA_KV_LORA]
    k_r = hcat[..., MLA_Q_LORA + MLA_KV_LORA:]
    q = (_rms_norm(c_q, q_norm) @ w_uq).reshape(b, s, MLA_HEADS, MLA_NOPE + MLA_ROPE)
    q = jnp.concatenate([q[..., :MLA_NOPE], _apply_rope(q[..., MLA_NOPE:], cos, sin)], axis=-1)
    kv = (_rms_norm(c_kv, kv_norm) @ w_ukv).reshape(b, s, MLA_HEADS, MLA_NOPE + MLA_V)
    k_r = _apply_rope(k_r[:, :, None, :], cos, sin)
    k = jnp.concatenate([kv[..., :MLA_NOPE], jnp.broadcast_to(k_r, (b, s, MLA_HEADS, MLA_ROPE))], axis=-1)
    v = kv[..., MLA_NOPE:]
    o = _blocked_attention(q.transpose(0, 2, 1, 3), k.transpose(0, 2, 1, 3), v.transpose(0, 2, 1, 3), MLA_SCALE)
    return o.transpose(0, 2, 1, 3).reshape(b, s, MLA_HEADS * MLA_V) @ w_o


def _hyena_filter_spectrum(seq, f_w1, f_b1, f_freq, f_w2, f_b2, f_w3, f_b3, f_w4):
    f32 = jnp.float32
    t = jnp.linspace(0.0, 1.0, seq, dtype=f32)[:, None]
    bands = (HY_EMB_DIM - 1) // 2
    w = 2.0 * math.pi * jnp.arange(seq, dtype=f32)[:, None] / seq
    f = jnp.linspace(1e-4, bands - 1, bands, dtype=f32)[None, :]
    z = jnp.concatenate([t, jnp.cos(f * w), -jnp.sin(f * w)], axis=-1)
    freq = f_freq.astype(f32)
    h = jnp.sin(freq * (z @ f_w1.astype(f32) + f_b1.astype(f32)))
    h = jnp.sin(freq * (h @ f_w2.astype(f32) + f_b2.astype(f32)))
    h = jnp.sin(freq * (h @ f_w3.astype(f32) + f_b3.astype(f32)))
    h = (h @ f_w4.astype(f32)).reshape(seq, 2, HY_ORDER, D_MODEL)
    deltas = jnp.abs(jnp.linspace(HY_MIN_DECAY, HY_MAX_DECAY, D_MODEL, dtype=f32))
    decay = jnp.exp(-t * deltas[None, :])
    h = h * (decay[:, None, None, :] + HY_SHIFT)
    kern = jnp.concatenate([h[:, 0], jnp.zeros((1, HY_ORDER, D_MODEL), f32), h[:0:-1, 1]], axis=0)
    kern = kern / jnp.sum(jnp.abs(kern), axis=0, keepdims=True)
    return jnp.fft.rfft(kern, axis=0)


def _hyena(xn, w_in, conv_w, conv_b, f_w1, f_b1, f_freq, f_w2, f_b2, f_w3, f_b3, f_w4, bias, w_out):
    b, s, _ = xn.shape
    u = xn @ w_in
    up = jnp.pad(u, ((0, 0), (1, 1), (0, 0)))
    u = up[:, :-2] * conv_w[0] + up[:, 1:-1] * conv_w[1] + up[:, 2:] * conv_w[2] + conv_b
    x1, x2, z = jnp.split(u, 3, axis=-1)
    spec = _hyena_filter_spectrum(s, f_w1, f_b1, f_freq, f_w2, f_b2, f_w3, f_b3, f_w4)
    for o, gate in enumerate((x1, x2)):
        zf = jnp.fft.rfft(z.astype(jnp.float32), n=2 * s, axis=1)
        conv = jnp.fft.irfft(zf * spec[None, :, o, :], n=2 * s, axis=1)[:, :s]
        z = gate * (conv.astype(z.dtype) + z * bias[o])
    return z @ w_out


def _diff_attention(xn, w_qkv, lq1, lk1, lq2, lk2, subln, w_o, cos, sin, layer_idx):
    f32 = jnp.float32
    b, s, _ = xn.shape
    qkv = xn @ w_qkv
    q = qkv[..., :DIFF_QK].reshape(b, s, DIFF_HEADS, 2, DIFF_HEAD_DIM)
    k = qkv[..., DIFF_QK:2 * DIFF_QK].reshape(b, s, DIFF_HEADS, 2, DIFF_HEAD_DIM)
    v = qkv[..., 2 * DIFF_QK:].reshape(b, s, DIFF_HEADS, 2 * DIFF_HEAD_DIM)
    q = _apply_rope(q, cos, sin)
    k = _apply_rope(k, cos, sin)
    lam_init = 0.8 - 0.6 * math.exp(-0.3 * layer_idx)
    lam = (jnp.exp(jnp.sum(lq1.astype(f32) * lk1.astype(f32)))
           - jnp.exp(jnp.sum(lq2.astype(f32) * lk2.astype(f32))) + lam_init)
    qh = q.transpose(0, 2, 3, 1, 4)
    kh = k.transpose(0, 2, 3, 1, 4)
    vh = v.transpose(0, 2, 1, 3)
    nb = s // Q_BLOCK
    qb = qh.reshape(b, DIFF_HEADS, 2, nb, Q_BLOCK, DIFF_HEAD_DIM).transpose(3, 0, 1, 2, 4, 5)

    def block(qblk):
        sc = jnp.einsum("bhcqd,bhckd->bhcqk", qblk, kh).astype(f32) * DIFF_SCALE
        p = jax.nn.softmax(sc, axis=-1)
        wgt = p[:, :, 0] - lam * p[:, :, 1]
        return jnp.einsum("bhqk,bhkd->bhqd", wgt.astype(vh.dtype), vh)

    o = lax.map(block, qb).transpose(1, 2, 0, 3, 4).reshape(b, DIFF_HEADS, s, 2 * DIFF_HEAD_DIM)
    o = _rms_norm(o, subln, DIFF_EPS) * (1.0 - lam_init)
    return o.transpose(0, 2, 1, 3).reshape(b, s, DIFF_HEADS * 2 * DIFF_HEAD_DIM) @ w_o


def _swiglu(xn, w_gate, w_up, w_down):
    return (jax.nn.silu(xn @ w_gate) * (xn @ w_up)) @ w_down


def setup_inputs(seed: int = 0) -> dict:
    key = jax.random.key(seed)
    keys = iter(jax.random.split(key, 96))
    f32 = jnp.float32

    def dense(fi, fo):
        return jax.random.normal(next(keys), (fi, fo), f32) * fi ** -0.5

    def gain(n):
        return 1.0 + 0.01 * jax.random.normal(next(keys), (n,), f32)

    def small(shape, scale):
        return scale * jax.random.normal(next(keys), shape, f32)

    p = {"x": jax.random.normal(next(keys), (BATCH, SEQ, D_MODEL), f32)}
    for i in range(DEPTH):
        kind = MIXER_ORDER[i % N_MIXERS]
        pre = "l%d_" % i
        p[pre + "mix_norm"] = gain(D_MODEL)
        if kind == "mla":
            p[pre + "mla_w_in"] = dense(D_MODEL, MLA_Q_LORA + MLA_KV_LORA + MLA_ROPE)
            p[pre + "mla_q_norm"] = gain(MLA_Q_LORA)
            p[pre + "mla_w_uq"] = dense(MLA_Q_LORA, MLA_HEADS * (MLA_NOPE + MLA_ROPE))
            p[pre + "mla_kv_norm"] = gain(MLA_KV_LORA)
            p[pre + "mla_w_ukv"] = dense(MLA_KV_LORA, MLA_HEADS * (MLA_NOPE + MLA_V))
            p[pre + "mla_w_o"] = dense(MLA_HEADS * MLA_V, D_MODEL)
        elif kind == "hyena":
            p[pre + "hy_w_in"] = dense(D_MODEL, 3 * D_MODEL)
            p[pre + "hy_conv_w"] = small((HY_SHORT, 3 * D_MODEL), HY_SHORT ** -0.5)
            p[pre + "hy_conv_b"] = small((3 * D_MODEL,), 0.01)
            p[pre + "hy_f_w1"] = dense(HY_EMB_DIM, HY_FILTER_WIDTH)
            p[pre + "hy_f_b1"] = small((HY_FILTER_WIDTH,), 0.1)
            p[pre + "hy_f_freq"] = gain(HY_FILTER_WIDTH)
            p[pre + "hy_f_w2"] = dense(HY_FILTER_WIDTH, HY_FILTER_WIDTH)
            p[pre + "hy_f_b2"] = small((HY_FILTER_WIDTH,), 0.1)
            p[pre + "hy_f_w3"] = dense(HY_FILTER_WIDTH, HY_FILTER_WIDTH)
            p[pre + "hy_f_b3"] = small((HY_FILTER_WIDTH,), 0.1)
            p[pre + "hy_f_w4"] = dense(HY_FILTER_WIDTH, 2 * HY_ORDER * D_MODEL)
            p[pre + "hy_bias"] = small((HY_ORDER, D_MODEL), 0.5)
            p[pre + "hy_w_out"] = dense(D_MODEL, D_MODEL)
        else:
            p[pre + "da_w_qkv"] = dense(D_MODEL, 3 * DIFF_QK)
            p[pre + "da_lq1"] = small((DIFF_HEAD_DIM,), 0.1)
            p[pre + "da_lk1"] = small((DIFF_HEAD_DIM,), 0.1)
            p[pre + "da_lq2"] = small((DIFF_HEAD_DIM,), 0.1)
            p[pre + "da_lk2"] = small((DIFF_HEAD_DIM,), 0.1)
            p[pre + "da_subln"] = gain(2 * DIFF_HEAD_DIM)
            p[pre + "da_w_o"] = dense(DIFF_QK, D_MODEL)
        p[pre + "ffn_norm"] = gain(D_MODEL)
        p[pre + "ffn_w_gate"] = dense(D_MODEL, D_FF)
        p[pre + "ffn_w_up"] = dense(D_MODEL, D_FF)
        p[pre + "ffn_w_down"] = dense(D_FF, D_MODEL)
    p["final_norm"] = gain(D_MODEL)
    return p


def reference(x,
              l0_mix_norm, l0_mla_w_in, l0_mla_q_norm, l0_mla_w_uq, l0_mla_kv_norm, l0_mla_w_ukv, l0_mla_w_o,
              l0_ffn_norm, l0_ffn_w_gate, l0_ffn_w_up, l0_ffn_w_down,
              l1_mix_norm, l1_hy_w_in, l1_hy_conv_w, l1_hy_conv_b, l1_hy_f_w1, l1_hy_f_b1, l1_hy_f_freq,
              l1_hy_f_w2, l1_hy_f_b2, l1_hy_f_w3, l1_hy_f_b3, l1_hy_f_w4, l1_hy_bias, l1_hy_w_out,
              l1_ffn_norm, l1_ffn_w_gate, l1_ffn_w_up, l1_ffn_w_down,
              l2_mix_norm, l2_da_w_qkv, l2_da_lq1, l2_da_lk1, l2_da_lq2, l2_da_lk2, l2_da_subln, l2_da_w_o,
              l2_ffn_norm, l2_ffn_w_gate, l2_ffn_w_up, l2_ffn_w_down,
              l3_mix_norm, l3_mla_w_in, l3_mla_q_norm, l3_mla_w_uq, l3_mla_kv_norm, l3_mla_w_ukv, l3_mla_w_o,
              l3_ffn_norm, l3_ffn_w_gate, l3_ffn_w_up, l3_ffn_w_down,
              final_norm):
    seq = x.shape[1]
    mla_cos, mla_sin = _rope_tables(seq, MLA_ROPE)
    diff_cos, diff_sin = _rope_tables(seq, DIFF_HEAD_DIM)
    mix_norms = (l0_mix_norm, l1_mix_norm, l2_mix_norm, l3_mix_norm)
    mixer_params = (
        (l0_mla_w_in, l0_mla_q_norm, l0_mla_w_uq, l0_mla_kv_norm, l0_mla_w_ukv, l0_mla_w_o),
        (l1_hy_w_in, l1_hy_conv_w, l1_hy_conv_b, l1_hy_f_w1, l1_hy_f_b1, l1_hy_f_freq,
         l1_hy_f_w2, l1_hy_f_b2, l1_hy_f_w3, l1_hy_f_b3, l1_hy_f_w4, l1_hy_bias, l1_hy_w_out),
        (l2_da_w_qkv, l2_da_lq1, l2_da_lk1, l2_da_lq2, l2_da_lk2, l2_da_subln, l2_da_w_o),
        (l3_mla_w_in, l3_mla_q_norm, l3_mla_w_uq, l3_mla_kv_norm, l3_mla_w_ukv, l3_mla_w_o),
    )
    ffn_params = (
        (l0_ffn_norm, l0_ffn_w_gate, l0_ffn_w_up, l0_ffn_w_down),
        (l1_ffn_norm, l1_ffn_w_gate, l1_ffn_w_up, l1_ffn_w_down),
        (l2_ffn_norm, l2_ffn_w_gate, l2_ffn_w_up, l2_ffn_w_down),
        (l3_ffn_norm, l3_ffn_w_gate, l3_ffn_w_up, l3_ffn_w_down),
    )
    for i in range(DEPTH):
        kind = MIXER_ORDER[i % N_MIXERS]
        xn = _rms_norm(x, mix_norms[i])
        if kind == "mla":
            x = x + _mla(xn, *mixer_params[i], mla_cos, mla_sin)
        elif kind == "hyena":
            x = x + _hyena(xn, *mixer_params[i])
        else:
            x = x + _diff_attention(xn, *mixer_params[i], diff_cos, diff_sin, i)
        f_norm, w_gate, w_up, w_down = ffn_params[i]
        x = x + _swiglu(_rms_norm(x, f_norm), w_gate, w_up, w_down)
    return _rms_norm(x, final_norm)
```

```python
import functools
import math

import jax
import jax.numpy as jnp
from jax import lax
from jax.experimental import pallas as pl
from jax.experimental.pallas import tpu as pltpu

F32 = jnp.float32
BF16 = jnp.bfloat16

D_MODEL = 2048
SEQ = 8192
RMS_EPS = 1e-6
ROPE_THETA = 10000.0
LOG2E = math.log2(math.e)

MLA_HEADS = 16
MLA_Q_LORA = 768
MLA_KV_LORA = 512
MLA_NOPE = 128
MLA_ROPE = 64
MLA_V = 128
MLA_SCALE = (MLA_NOPE + MLA_ROPE) ** -0.5
MLA_QK_PAD = 256

HY_ORDER = 2
HY_EMB_DIM = 33
HY_TARGET = 1e-2
HY_MAX_DECAY = math.log(HY_TARGET) / 0.3
HY_MIN_DECAY = math.log(HY_TARGET) / 1.5

DIFF_HEAD_DIM = 128
DIFF_HEADS = D_MODEL // (2 * DIFF_HEAD_DIM)
DIFF_QK = DIFF_HEADS * 2 * DIFF_HEAD_DIM
DIFF_SCALE = DIFF_HEAD_DIM ** -0.5
DIFF_EPS = 1e-5

D_FF = -(-8 * D_MODEL // (3 * 256)) * 256

VMEM_LIMIT = 56 * 1024 * 1024
NEG_BIG = -1e30


def _params(*sem):
    return pltpu.CompilerParams(dimension_semantics=sem, vmem_limit_bytes=VMEM_LIMIT)


def _rms(x, g, eps):
    return x * lax.rsqrt(jnp.mean(x * x, axis=-1, keepdims=True) + eps) * g


def _norm_matmul_kernel(x_ref, g_ref, w_ref, o_ref, xn_ref):
    @pl.when(pl.program_id(1) == 0)
    def _():
        xn_ref[...] = _rms(x_ref[...], g_ref[...], RMS_EPS).astype(BF16)

    o_ref[...] = jnp.dot(xn_ref[...], w_ref[...], preferred_element_type=F32).astype(o_ref.dtype)


def norm_matmul(x, g, w, out_dtype, tm=512, tn=512):
    m, k = x.shape
    n = w.shape[1]
    return pl.pallas_call(
        _norm_matmul_kernel,
        grid=(m // tm, n // tn),
        in_specs=[
            pl.BlockSpec((tm, k), lambda i, j: (i, 0)),
            pl.BlockSpec((1, k), lambda i, j: (0, 0)),
            pl.BlockSpec((k, tn), lambda i, j: (0, j)),
        ],
        out_specs=pl.BlockSpec((tm, tn), lambda i, j: (i, j)),
        out_shape=jax.ShapeDtypeStruct((m, n), out_dtype),
        scratch_shapes=[pltpu.VMEM((tm, k), BF16)],
        compiler_params=_params("parallel", "arbitrary"),
        name="norm_matmul",
    )(x, g.reshape(1, k), w)


def _matmul_res_kernel(a_ref, w_ref, r_ref, o_ref):
    o_ref[...] = r_ref[...] + jnp.dot(a_ref[...], w_ref[...], preferred_element_type=F32)


def matmul_residual(a, w, res, tm=512, tn=512):
    m, k = a.shape
    n = w.shape[1]
    return pl.pallas_call(
        _matmul_res_kernel,
        grid=(m // tm, n // tn),
        in_specs=[
            pl.BlockSpec((tm, k), lambda i, j: (i, 0)),
            pl.BlockSpec((k, tn), lambda i, j: (0, j)),
            pl.BlockSpec((tm, tn), lambda i, j: (i, j)),
        ],
        out_specs=pl.BlockSpec((tm, tn), lambda i, j: (i, j)),
        out_shape=jax.ShapeDtypeStruct((m, n), F32),
        compiler_params=_params("parallel", "parallel"),
        name="matmul_residual",
    )(a, w, res)


def _ffn_kernel(x_ref, g_ref, wg_ref, wu_ref, wd_ref, o_ref, xn_ref):
    @pl.when(pl.program_id(1) == 0)
    def _():
        x = x_ref[...]
        xn_ref[...] = _rms(x, g_ref[...], RMS_EPS).astype(BF16)
        o_ref[...] = x

    xn = xn_ref[...]
    a = jnp.dot(xn, wg_ref[...], preferred_element_type=F32)
    b = jnp.dot(xn, wu_ref[...], preferred_element_type=F32)
    h = (a * jax.nn.sigmoid(a) * b).astype(BF16)
    o_ref[...] += jnp.dot(h, wd_ref[...], preferred_element_type=F32)


def ffn_block(x, g, wg, wu, wd, tm=512, tf=512):
    m, d = x.shape
    f = wg.shape[1]
    return pl.pallas_call(
        _ffn_kernel,
        grid=(m // tm, f // tf),
        in_specs=[
            pl.BlockSpec((tm, d), lambda i, j: (i, 0)),
            pl.BlockSpec((1, d), lambda i, j: (0, 0)),
            pl.BlockSpec((d, tf), lambda i, j: (0, j)),
            pl.BlockSpec((d, tf), lambda i, j: (0, j)),
            pl.BlockSpec((tf, d), lambda i, j: (j, 0)),
        ],
        out_specs=pl.BlockSpec((tm, d), lambda i, j: (i, 0)),
        out_shape=jax.ShapeDtypeStruct((m, d), F32),
        scratch_shapes=[pltpu.VMEM((tm, d), BF16)],
        compiler_params=_params("parallel", "arbitrary"),
        name="ffn_block",
    )(x, g.reshape(1, d), wg, wu, wd)


def _rmsnorm_kernel(x_ref, g_ref, o_ref):
    o_ref[...] = _rms(x_ref[...], g_ref[...], RMS_EPS)


def rmsnorm(x, g, tm=512):
    m, d = x.shape
    return pl.pallas_call(
        _rmsnorm_kernel,
        grid=(m // tm,),
        in_specs=[pl.BlockSpec((tm, d), lambda i: (i, 0)), pl.BlockSpec((1, d), lambda i: (0, 0))],
        out_specs=pl.BlockSpec((tm, d), lambda i: (i, 0)),
        out_shape=jax.ShapeDtypeStruct((m, d), F32),
        compiler_params=_params("parallel"),
        name="final_rmsnorm",
    )(x, g.reshape(1, d))


def _rope_pair(t, tab):
    t = t * tab
    return t + pltpu.roll(t, 64, axis=1)


def _mla_in_kernel(x_ref, g_ref, w_ref, qn_ref, kvn_ref, tab_ref, cq_ref, ckv_ref, kr_ref):
    xn = _rms(x_ref[...], g_ref[...], RMS_EPS).astype(BF16)
    h = jnp.dot(xn, w_ref[...], preferred_element_type=F32)
    cq_ref[...] = _rms(h[:, :MLA_Q_LORA], qn_ref[...], RMS_EPS).astype(BF16)
    ckv_ref[...] = _rms(h[:, MLA_Q_LORA:MLA_Q_LORA + MLA_KV_LORA], kvn_ref[...], RMS_EPS).astype(BF16)
    kr = _rope_pair(h[:, MLA_Q_LORA + MLA_KV_LORA:], tab_ref[...])
    lane = lax.broadcasted_iota(jnp.int32, kr.shape, 1)
    kr_ref[...] = jnp.where(lane < MLA_ROPE, kr, 0.0).astype(BF16)


def mla_in(x, g, w_in_ext, q_norm, kv_norm, tab, tm=512):
    m, d = x.shape
    n = w_in_ext.shape[1]
    row = lambda i: (i, 0)
    fixed = lambda i: (0, 0)
    return pl.pallas_call(
        _mla_in_kernel,
        grid=(m // tm,),
        in_specs=[
            pl.BlockSpec((tm, d), row),
            pl.BlockSpec((1, d), fixed),
            pl.BlockSpec((d, n), fixed),
            pl.BlockSpec((1, MLA_Q_LORA), fixed),
            pl.BlockSpec((1, MLA_KV_LORA), fixed),
            pl.BlockSpec((tm, 128), row),
        ],
        out_specs=[
            pl.BlockSpec((tm, MLA_Q_LORA), row),
            pl.BlockSpec((tm, MLA_KV_LORA), row),
            pl.BlockSpec((tm, 128), row),
        ],
        out_shape=[
            jax.ShapeDtypeStruct((m, MLA_Q_LORA), BF16),
            jax.ShapeDtypeStruct((m, MLA_KV_LORA), BF16),
            jax.ShapeDtypeStruct((m, 128), BF16),
        ],
        compiler_params=_params("parallel"),
        name="mla_in",
    )(x, g.reshape(1, d), w_in_ext, q_norm.reshape(1, -1), kv_norm.reshape(1, -1), tab)


def _mla_q_kernel(c_ref, w_ref, tab_ref, q_ref, *, heads):
    y = jnp.dot(c_ref[...], w_ref[...], preferred_element_type=F32) * (MLA_SCALE * LOG2E)
    tab = tab_ref[...]
    for h in range(heads):
        base = h * MLA_QK_PAD
        q_ref[:, base:base + 128] = y[:, base:base + 128].astype(BF16)
        q_ref[:, base + 128:base + 256] = _rope_pair(y[:, base + 128:base + 256], tab).astype(BF16)


def mla_q(cq, w_uq_ext, tab, tm=512, heads_per_step=4):
    m, k = cq.shape
    n = w_uq_ext.shape[1]
    tn = heads_per_step * MLA_QK_PAD
    return pl.pallas_call(
        functools.partial(_mla_q_kernel, heads=heads_per_step),
        grid=(m // tm, n // tn),
        in_specs=[
            pl.BlockSpec((tm, k), lambda i, j: (i, 0)),
            pl.BlockSpec((k, tn), lambda i, j: (0, j)),
            pl.BlockSpec((tm, 128), lambda i, j: (i, 0)),
        ],
        out_specs=pl.BlockSpec((tm, tn), lambda i, j: (i, j)),
        out_shape=jax.ShapeDtypeStruct((m, n), BF16),
        compiler_params=_params("parallel", "parallel"),
        name="mla_q",
    )(cq, w_uq_ext, tab)


def _mla_kv_kernel(c_ref, w_ref, kr_ref, k_ref, v_ref, *, heads):
    y = jnp.dot(c_ref[...], w_ref[...], preferred_element_type=F32)
    kr = kr_ref[...]
    for h in range(heads):
        k_ref[:, h * 256:h * 256 + 128] = y[:, h * 256:h * 256 + 128].astype(BF16)
        k_ref[:, h * 256 + 128:h * 256 + 256] = kr
        v_ref[:, h * 128:h * 128 + 128] = y[:, h * 256 + 128:h * 256 + 256].astype(BF16)


def mla_kv(ckv, w_ukv, kr, tm=512, heads_per_step=4):
    m, k = ckv.shape
    n = w_ukv.shape[1]
    tn = heads_per_step * (MLA_NOPE + MLA_V)
    return pl.pallas_call(
        functools.partial(_mla_kv_kernel, heads=heads_per_step),
        grid=(m // tm, n // tn),
        in_specs=[
            pl.BlockSpec((tm, k), lambda i, j: (i, 0)),
            pl.BlockSpec((k, tn), lambda i, j: (0, j)),
            pl.BlockSpec((tm, 128), lambda i, j: (i, 0)),
        ],
        out_specs=[
            pl.BlockSpec((tm, heads_per_step * MLA_QK_PAD), lambda i, j: (i, j)),
            pl.BlockSpec((tm, heads_per_step * MLA_V), lambda i, j: (i, j)),
        ],
        out_shape=[
            jax.ShapeDtypeStruct((m, MLA_HEADS * MLA_QK_PAD), BF16),
            jax.ShapeDtypeStruct((m, MLA_HEADS * MLA_V), BF16),
        ],
        compiler_params=_params("parallel", "parallel"),
        name="mla_kv",
    )(ckv, w_ukv, kr)


def _softmax_step(q, k, v, m, l, acc):
    s = lax.dot_general(q, k, (((1,), (1,)), ((), ())), preferred_element_type=F32)
    m_new = jnp.maximum(m, jnp.max(s, axis=-1, keepdims=True))
    alpha = jnp.exp2(m - m_new)
    p = jnp.exp2(s - m_new)
    l = alpha * l + jnp.sum(p, axis=-1, keepdims=True)
    acc = alpha * acc + jnp.dot(p.astype(BF16), v, preferred_element_type=F32)
    return m_new, l, acc


def _mla_attn_kernel(q_ref, k_ref, v_ref, o_ref, *, tk):
    q = q_ref[...]
    tq = q.shape[0]
    nk = k_ref.shape[0] // tk

    def body(c, carry):
        off = pl.multiple_of(c * tk, tk)
        return _softmax_step(q, k_ref[pl.ds(off, tk), :], v_ref[pl.ds(off, tk), :], *carry)

    init = (jnp.full((tq, 1), NEG_BIG, F32), jnp.zeros((tq, 1), F32), jnp.zeros((tq, v_ref.shape[1]), F32))
    _, l, acc = lax.fori_loop(0, nk, body, init)
    o_ref[...] = (acc / l).astype(o_ref.dtype)


def mla_attention(q, k, v, tq=256, tk=512):
    s = q.shape[0]
    return pl.pallas_call(
        functools.partial(_mla_attn_kernel, tk=tk),
        grid=(MLA_HEADS, s // tq),
        in_specs=[
            pl.BlockSpec((tq, MLA_QK_PAD), lambda h, i: (i, h)),
            pl.BlockSpec((s, MLA_QK_PAD), lambda h, i: (0, h)),
            pl.BlockSpec((s, MLA_V), lambda h, i: (0, h)),
        ],
        out_specs=pl.BlockSpec((tq, MLA_V), lambda h, i: (i, h)),
        out_shape=jax.ShapeDtypeStruct((s, MLA_HEADS * MLA_V), BF16),
        compiler_params=_params("parallel", "parallel"),
        name="mla_attention",
    )(q, k, v)


def _rot_cols(w):
    half = w.shape[-1] // 2
    return jnp.concatenate([-w[..., half:], w[..., :half]], axis=-1)


def mla_layer(x, mix_norm, w_in, q_norm, w_uq, kv_norm, w_ukv, w_o, tab):
    w_kr = w_in[:, MLA_Q_LORA + MLA_KV_LORA:]
    w_in_ext = jnp.concatenate([w_in, _rot_cols(w_kr)], axis=1).astype(BF16)
    wq = w_uq.reshape(MLA_Q_LORA, MLA_HEADS, MLA_NOPE + MLA_ROPE)
    wq_rope = wq[..., MLA_NOPE:]
    w_uq_ext = jnp.concatenate([wq, _rot_cols(wq_rope)], axis=-1).reshape(MLA_Q_LORA, -1).astype(BF16)
    cq, ckv, kr = mla_in(x, mix_norm, w_in_ext, q_norm, kv_norm, tab)
    q = mla_q(cq, w_uq_ext, tab)
    k, v = mla_kv(ckv, w_ukv.astype(BF16), kr)
    o = mla_attention(q, k, v)
    return matmul_residual(o, w_o.astype(BF16), x)


def _diff_qkv_kernel(x_ref, g_ref, w_ref, cos_ref, sin_ref, o_ref, xn_ref, *, tn, n_q, n_rope):
    j = pl.program_id(1)

    @pl.when(j == 0)
    def _():
        xn_ref[...] = _rms(x_ref[...], g_ref[...], RMS_EPS).astype(BF16)

    y = jnp.dot(xn_ref[...], w_ref[...], preferred_element_type=F32)

    @pl.when(j < n_rope)
    def _():
        scale = jnp.where(j < n_q, DIFF_SCALE * LOG2E, 1.0)
        c = cos_ref[...] * scale
        s = sin_ref[...] * scale
        for g in range(tn // 128):
            t = y[:, g * 128:(g + 1) * 128]
            o_ref[:, g * 128:(g + 1) * 128] = (t * c + pltpu.roll(t, 64, axis=1) * s).astype(BF16)

    @pl.when(j >= n_rope)
    def _():
        o_ref[...] = y.astype(BF16)


def diff_qkv(x, g, w, cos, sin_signed, tm=512, tn=512):
    m, k = x.shape
    n = w.shape[1]
    kern = functools.partial(_diff_qkv_kernel, tn=tn, n_q=DIFF_QK // tn, n_rope=2 * DIFF_QK // tn)
    return pl.pallas_call(
        kern,
        grid=(m // tm, n // tn),
        in_specs=[
            pl.BlockSpec((tm, k), lambda i, j: (i, 0)),
            pl.BlockSpec((1, k), lambda i, j: (0, 0)),
            pl.BlockSpec((k, tn), lambda i, j: (0, j)),
            pl.BlockSpec((tm, 128), lambda i, j: (i, 0)),
            pl.BlockSpec((tm, 128), lambda i, j: (i, 0)),
        ],
        out_specs=pl.BlockSpec((tm, tn), lambda i, j: (i, j)),
        out_shape=jax.ShapeDtypeStruct((m, n), BF16),
        scratch_shapes=[pltpu.VMEM((tm, k), BF16)],
        compiler_params=_params("parallel", "arbitrary"),
        name="diff_qkv",
    )(x, g.reshape(1, k), w, cos, sin_signed)


def _diff_attn_kernel(lam_ref, q_ref, k_ref, v_ref, g_ref, o_ref, *, tk, lam_init):
    tq = q_ref.shape[0]
    nk = k_ref.shape[0] // tk
    dv = v_ref.shape[1]
    q0 = q_ref[:, :128]
    q1 = q_ref[:, 128:]

    def body(c, carry):
        off = pl.multiple_of(c * tk, tk)
        k = k_ref[pl.ds(off, tk), :]
        v = v_ref[pl.ds(off, tk), :]
        st0 = _softmax_step(q0, k[:, :128], v, *carry[:3])
        st1 = _softmax_step(q1, k[:, 128:], v, *carry[3:])
        return st0 + st1

    one = (jnp.full((tq, 1), NEG_BIG, F32), jnp.zeros((tq, 1), F32), jnp.zeros((tq, dv), F32))
    _, l0, a0, _, l1, a1 = lax.fori_loop(0, nk, body, one + one)
    lam = lam_ref[0, 0]
    o = a0 / l0 - lam * (a1 / l1)
    o = _rms(o, g_ref[...], DIFF_EPS) * (1.0 - lam_init)
    o_ref[...] = o.astype(o_ref.dtype)


def diff_attention(qkv, lam, subln, lam_init, tq=256, tk=512):
    s = qkv.shape[0]
    hd = 2 * DIFF_HEAD_DIM
    return pl.pallas_call(
        functools.partial(_diff_attn_kernel, tk=tk, lam_init=lam_init),
        grid=(DIFF_HEADS, s // tq),
        in_specs=[
            pl.BlockSpec(memory_space=pltpu.SMEM),
            pl.BlockSpec((tq, hd), lambda h, i: (i, h)),
            pl.BlockSpec((s, hd), lambda h, i: (0, DIFF_HEADS + h)),
            pl.BlockSpec((s, hd), lambda h, i: (0, 2 * DIFF_HEADS + h)),
            pl.BlockSpec((1, hd), lambda h, i: (0, 0)),
        ],
        out_specs=pl.BlockSpec((tq, hd), lambda h, i: (i, h)),
        out_shape=jax.ShapeDtypeStruct((s, DIFF_HEADS * hd), BF16),
        compiler_params=_params("parallel", "parallel"),
        name="diff_attention",
    )(lam, qkv, qkv, qkv, subln.reshape(1, hd))


def _lambda_kernel(lq1_ref, lk1_ref, lq2_ref, lk2_ref, o_ref, *, lam_init):
    a = jnp.sum(lq1_ref[...] * lk1_ref[...], axis=-1, keepdims=True)
    b = jnp.sum(lq2_ref[...] * lk2_ref[...], axis=-1, keepdims=True)
    o_ref[...] = jnp.exp(a) - jnp.exp(b) + lam_init


def diff_lambda(lq1, lk1, lq2, lk2, lam_init):
    args = [v.reshape(1, -1) for v in (lq1, lk1, lq2, lk2)]
    return pl.pallas_call(
        functools.partial(_lambda_kernel, lam_init=lam_init),
        out_shape=jax.ShapeDtypeStruct((1, 1), F32),
        name="diff_lambda",
    )(*args)


def diff_layer(x, mix_norm, w_qkv, lq1, lk1, lq2, lk2, subln, w_o, cos, sin_signed, layer_idx):
    lam_init = 0.8 - 0.6 * math.exp(-0.3 * layer_idx)
    qkv = diff_qkv(x, mix_norm, w_qkv.astype(BF16), cos, sin_signed)
    lam = diff_lambda(lq1, lk1, lq2, lk2, lam_init)
    o = diff_attention(qkv, lam, subln, lam_init)
    return matmul_residual(o, w_o.astype(BF16), x)


def _hyena_filter_spectrum(seq, f_w1, f_b1, f_freq, f_w2, f_b2, f_w3, f_b3, f_w4):
    t = jnp.linspace(0.0, 1.0, seq, dtype=F32)[:, None]
    bands = (HY_EMB_DIM - 1) // 2
    w = 2.0 * math.pi * jnp.arange(seq, dtype=F32)[:, None] / seq
    f = jnp.linspace(1e-4, bands - 1, bands, dtype=F32)[None, :]
    z = jnp.concatenate([t, jnp.cos(f * w), -jnp.sin(f * w)], axis=-1)
    h = jnp.sin(f_freq * (z @ f_w1 + f_b1))
    h = jnp.sin(f_freq * (h @ f_w2 + f_b2))
    h = jnp.sin(f_freq * (h @ f_w3 + f_b3))
    h = (h @ f_w4).reshape(seq, 2, HY_ORDER, D_MODEL)
    deltas = jnp.abs(jnp.linspace(HY_MIN_DECAY, HY_MAX_DECAY, D_MODEL, dtype=F32))
    decay = jnp.exp(-t * deltas[None, :])
    h = h * decay[:, None, None, :]
    kern = jnp.concatenate([h[:, 0], jnp.zeros((1, HY_ORDER, D_MODEL), F32), h[:0:-1, 1]], axis=0)
    kern = kern / jnp.sum(jnp.abs(kern), axis=0, keepdims=True)
    return jnp.fft.rfft(kern, axis=0)


def hyena_layer(x, mix_norm, w_in, conv_w, conv_b, f_w1, f_b1, f_freq, f_w2, f_b2, f_w3, f_b3, f_w4, bias, w_out):
    s = x.shape[0]
    u = norm_matmul(x, mix_norm, w_in.astype(BF16), F32)
    up = jnp.pad(u, ((1, 1), (0, 0)))
    u = up[:-2] * conv_w[0] + up[1:-1] * conv_w[1] + up[2:] * conv_w[2] + conv_b
    x1, x2, z = jnp.split(u, 3, axis=-1)
    spec = _hyena_filter_spectrum(s, f_w1, f_b1, f_freq, f_w2, f_b2, f_w3, f_b3, f_w4)
    for o, gate in enumerate((x1, x2)):
        zf = jnp.fft.rfft(z, n=2 * s, axis=0)
        conv = jnp.fft.irfft(zf * spec[:, o, :], n=2 * s, axis=0)[:s]
        z = gate * (conv + z * bias[o])
    return matmul_residual(z.astype(BF16), w_out.astype(BF16), x)


def _rope_tables(seq, dim):
    inv = 1.0 / (ROPE_THETA ** (jnp.arange(0, dim, 2, dtype=F32) / dim))
    ang = jnp.arange(seq, dtype=F32)[:, None] * inv[None, :]
    return jnp.cos(ang), jnp.sin(ang)


def kernel(x, l0_mix_norm, l0_mla_w_in, l0_mla_q_norm, l0_mla_w_uq, l0_mla_kv_norm, l0_mla_w_ukv, l0_mla_w_o, l0_ffn_norm, l0_ffn_w_gate, l0_ffn_w_up, l0_ffn_w_down, l1_mix_norm, l1_hy_w_in, l1_hy_conv_w, l1_hy_conv_b, l1_hy_f_w1, l1_hy_f_b1, l1_hy_f_freq, l1_hy_f_w2, l1_hy_f_b2, l1_hy_f_w3, l1_hy_f_b3, l1_hy_f_w4, l1_hy_bias, l1_hy_w_out, l1_ffn_norm, l1_ffn_w_gate, l1_ffn_w_up, l1_ffn_w_down, l2_mix_norm, l2_da_w_qkv, l2_da_lq1, l2_da_lk1, l2_da_lq2, l2_da_lk2, l2_da_subln, l2_da_w_o, l2_ffn_norm, l2_ffn_w_gate, l2_ffn_w_up, l2_ffn_w_down, l3_mix_norm, l3_mla_w_in, l3_mla_q_norm, l3_mla_w_uq, l3_mla_kv_norm, l3_mla_w_ukv, l3_mla_w_o, l3_ffn_norm, l3_ffn_w_gate, l3_ffn_w_up, l3_ffn_w_down, final_norm):
    b, s, d = x.shape
    assert b == 1
    h = x.reshape(s, d)

    mc, ms = _rope_tables(s, MLA_ROPE)
    mla_tab = jnp.concatenate([mc, mc, ms, ms], axis=-1)
    dc, ds = _rope_tables(s, DIFF_HEAD_DIM)
    diff_cos = jnp.concatenate([dc, dc], axis=-1)
    diff_sin = jnp.concatenate([-ds, ds], axis=-1)

    def ffn(h, g, wg, wu, wd):
        return ffn_block(h, g, wg.astype(BF16), wu.astype(BF16), wd.astype(BF16))

    h = mla_layer(h, l0_mix_norm, l0_mla_w_in, l0_mla_q_norm, l0_mla_w_uq, l0_mla_kv_norm, l0_mla_w_ukv, l0_mla_w_o, mla_tab)
    h = ffn(h, l0_ffn_norm, l0_ffn_w_gate, l0_ffn_w_up, l0_ffn_w_down)
    h = hyena_layer(h, l1_mix_norm, l1_hy_w_in, l1_hy_conv_w, l1_hy_conv_b, l1_hy_f_w1, l1_hy_f_b1, l1_hy_f_freq, l1_hy_f_w2, l1_hy_f_b2, l1_hy_f_w3, l1_hy_f_b3, l1_hy_f_w4, l1_hy_bias, l1_hy_w_out)
    h = ffn(h, l1_ffn_norm, l1_ffn_w_gate, l1_ffn_w_up, l1_ffn_w_down)
    h = diff_layer(h, l2_mix_norm, l2_da_w_qkv, l2_da_lq1, l2_da_lk1, l2_da_lq2, l2_da_lk2, l2_da_subln, l2_da_w_o, diff_cos, diff_sin, 2)
    h = ffn(h, l2_ffn_norm, l2_ffn_w_gate, l2_ffn_w_up, l2_ffn_w_down)
    h = mla_layer(h, l3_mix_norm, l3_mla_w_in, l3_mla_q_norm, l3_mla_w_uq, l3_mla_kv_norm, l3_mla_w_ukv, l3_mla_w_o, mla_tab)
    h = ffn(h, l3_ffn_norm, l3_ffn_w_gate, l3_ffn_w_up, l3_ffn_w_down)
    return rmsnorm(h, final_norm).reshape(b, s, d)
```

```python
import functools
import math

import jax
import jax.numpy as jnp
from jax import lax
from jax.experimental import pallas as pl
from jax.experimental.pallas import tpu as pltpu

F32 = jnp.float32
BF16 = jnp.bfloat16

D_MODEL = 2048
SEQ = 8192
RMS_EPS = 1e-6
ROPE_THETA = 10000.0
LOG2E = math.log2(math.e)

MLA_HEADS = 16
MLA_Q_LORA = 768
MLA_KV_LORA = 512
MLA_NOPE = 128
MLA_ROPE = 64
MLA_V = 128
MLA_QK = MLA_NOPE + MLA_ROPE
MLA_SCALE = MLA_QK ** -0.5
MLA_QK_PAD = 256

HY_ORDER = 2
HY_EMB_DIM = 33
HY_TARGET = 1e-2
HY_MAX_DECAY = math.log(HY_TARGET) / 0.3
HY_MIN_DECAY = math.log(HY_TARGET) / 1.5

DIFF_HEAD_DIM = 128
DIFF_HEADS = D_MODEL // (2 * DIFF_HEAD_DIM)
DIFF_QK = DIFF_HEADS * 2 * DIFF_HEAD_DIM
DIFF_SCALE = DIFF_HEAD_DIM ** -0.5
DIFF_EPS = 1e-5

D_FF = -(-8 * D_MODEL // (3 * 256)) * 256

VMEM_LIMIT = 56 * 1024 * 1024
NEG_BIG = -1e30


def _params(*sem):
    return pltpu.CompilerParams(dimension_semantics=sem, vmem_limit_bytes=VMEM_LIMIT)


def _rms(x, g, eps):
    return x * lax.rsqrt(jnp.mean(x * x, axis=-1, keepdims=True) + eps) * g


def _dot_nt(a, b):
    return lax.dot_general(a, b, (((1,), (1,)), ((), ())), preferred_element_type=F32)


def _rope_rows(y, c, s):
    h = c.shape[0]
    y1, y2 = y[:h], y[h:]
    return y1 * c - y2 * s, y2 * c + y1 * s


def _norm_matmul_kernel(x_ref, g_ref, w_ref, *rest, rope, scale):
    o_ref, xn_ref = rest[-2:]

    @pl.when(pl.program_id(1) == 0)
    def _():
        xn_ref[...] = _rms(x_ref[...], g_ref[...], RMS_EPS).astype(BF16)

    y = jnp.dot(xn_ref[...], w_ref[...], preferred_element_type=F32)
    if rope:
        c = rest[0][...] * scale
        s = rest[1][...] * scale
        for g in range(y.shape[1] // 128):
            t = y[:, g * 128:(g + 1) * 128]
            o_ref[:, g * 128:(g + 1) * 128] = (t * c + pltpu.roll(t, 64, axis=1) * s).astype(o_ref.dtype)
    else:
        o_ref[...] = (y * scale).astype(o_ref.dtype)


def norm_matmul(x, g, w, out_dtype, rope=None, scale=1.0, tm=512, tn=512):
    m, k = x.shape
    n = w.shape[1]
    in_specs = [
        pl.BlockSpec((tm, k), lambda i, j: (i, 0)),
        pl.BlockSpec((1, k), lambda i, j: (0, 0)),
        pl.BlockSpec((k, tn), lambda i, j: (0, j)),
    ]
    args = [x, g.reshape(1, k), w]
    if rope is not None:
        in_specs += [pl.BlockSpec((tm, 128), lambda i, j: (i, 0))] * 2
        args += list(rope)
    return pl.pallas_call(
        functools.partial(_norm_matmul_kernel, rope=rope is not None, scale=scale),
        grid=(m // tm, n // tn),
        in_specs=in_specs,
        out_specs=pl.BlockSpec((tm, tn), lambda i, j: (i, j)),
        out_shape=jax.ShapeDtypeStruct((m, n), out_dtype),
        scratch_shapes=[pltpu.VMEM((tm, k), BF16)],
        compiler_params=_params("parallel", "arbitrary"),
        name="norm_matmul",
    )(*args)


def _norm_matmul_t_kernel(x_ref, g_ref, wt_ref, *rest, rope, scale):
    o_ref, xn_ref = rest[-2:]

    @pl.when(pl.program_id(1) == 0)
    def _():
        xn_ref[...] = _rms(x_ref[...], g_ref[...], RMS_EPS).astype(BF16)

    y = _dot_nt(wt_ref[...], xn_ref[...]) * scale
    if rope:
        c = rest[0][...]
        s = rest[1][...]
        for g in range(y.shape[0] // 128):
            r1, r2 = _rope_rows(y[g * 128:(g + 1) * 128], c, s)
            o_ref[g * 128:g * 128 + 64, :] = r1.astype(o_ref.dtype)
            o_ref[g * 128 + 64:(g + 1) * 128, :] = r2.astype(o_ref.dtype)
    else:
        o_ref[...] = y.astype(o_ref.dtype)


def norm_matmul_t(x, g, wt, out_dtype, rope_t=None, scale=1.0, tm=512, tn=512):
    m, k = x.shape
    n = wt.shape[0]
    in_specs = [
        pl.BlockSpec((tm, k), lambda i, j: (i, 0)),
        pl.BlockSpec((1, k), lambda i, j: (0, 0)),
        pl.BlockSpec((tn, k), lambda i, j: (j, 0)),
    ]
    args = [x, g.reshape(1, k), wt]
    if rope_t is not None:
        in_specs += [pl.BlockSpec((64, tm), lambda i, j: (0, i))] * 2
        args += list(rope_t)
    return pl.pallas_call(
        functools.partial(_norm_matmul_t_kernel, rope=rope_t is not None, scale=scale),
        grid=(m // tm, n // tn),
        in_specs=in_specs,
        out_specs=pl.BlockSpec((tn, tm), lambda i, j: (j, i)),
        out_shape=jax.ShapeDtypeStruct((n, m), out_dtype),
        scratch_shapes=[pltpu.VMEM((tm, k), BF16)],
        compiler_params=_params("parallel", "arbitrary"),
        name="norm_matmul_t",
    )(*args)


def _matmul_res_kernel(a_ref, w_ref, r_ref, o_ref):
    o_ref[...] = r_ref[...] + jnp.dot(a_ref[...], w_ref[...], preferred_element_type=F32)


def matmul_residual(a, w, res, tm=512, tn=512):
    m, k = a.shape
    n = w.shape[1]
    return pl.pallas_call(
        _matmul_res_kernel,
        grid=(m // tm, n // tn),
        in_specs=[
            pl.BlockSpec((tm, k), lambda i, j: (i, 0)),
            pl.BlockSpec((k, tn), lambda i, j: (0, j)),
            pl.BlockSpec((tm, tn), lambda i, j: (i, j)),
        ],
        out_specs=pl.BlockSpec((tm, tn), lambda i, j: (i, j)),
        out_shape=jax.ShapeDtypeStruct((m, n), F32),
        compiler_params=_params("parallel", "parallel"),
        name="matmul_residual",
    )(a, w, res)


def _ffn_kernel(x_ref, g_ref, wg_ref, wu_ref, wd_ref, o_ref, xn_ref):
    @pl.when(pl.program_id(1) == 0)
    def _():
        x = x_ref[...]
        xn_ref[...] = _rms(x, g_ref[...], RMS_EPS).astype(BF16)
        o_ref[...] = x

    xn = xn_ref[...]
    a = jnp.dot(xn, wg_ref[...], preferred_element_type=F32)
    b = jnp.dot(xn, wu_ref[...], preferred_element_type=F32)
    h = (a * jax.nn.sigmoid(a) * b).astype(BF16)
    o_ref[...] += jnp.dot(h, wd_ref[...], preferred_element_type=F32)


def ffn_block(x, g, wg, wu, wd, tm=512, tf=512):
    m, d = x.shape
    f = wg.shape[1]
    return pl.pallas_call(
        _ffn_kernel,
        grid=(m // tm, f // tf),
        in_specs=[
            pl.BlockSpec((tm, d), lambda i, j: (i, 0)),
            pl.BlockSpec((1, d), lambda i, j: (0, 0)),
            pl.BlockSpec((d, tf), lambda i, j: (0, j)),
            pl.BlockSpec((d, tf), lambda i, j: (0, j)),
            pl.BlockSpec((tf, d), lambda i, j: (j, 0)),
        ],
        out_specs=pl.BlockSpec((tm, d), lambda i, j: (i, 0)),
        out_shape=jax.ShapeDtypeStruct((m, d), F32),
        scratch_shapes=[pltpu.VMEM((tm, d), BF16)],
        compiler_params=_params("parallel", "arbitrary"),
        name="ffn_block",
    )(x, g.reshape(1, d), wg, wu, wd)


def _rmsnorm_kernel(x_ref, g_ref, o_ref):
    o_ref[...] = _rms(x_ref[...], g_ref[...], RMS_EPS)


def rmsnorm(x, g, tm=512):
    m, d = x.shape
    return pl.pallas_call(
        _rmsnorm_kernel,
        grid=(m // tm,),
        in_specs=[pl.BlockSpec((tm, d), lambda i: (i, 0)), pl.BlockSpec((1, d), lambda i: (0, 0))],
        out_specs=pl.BlockSpec((tm, d), lambda i: (i, 0)),
        out_shape=jax.ShapeDtypeStruct((m, d), F32),
        compiler_params=_params("parallel"),
        name="final_rmsnorm",
    )(x, g.reshape(1, d))


def _flash_t(qs, k_of_part, k_ref, vt_ref, sa_ref, sb_ref, acc_ref, *, tk):
    parts = len(qs)
    tp = qs[0].shape[1]
    nk = k_ref.shape[0] // tk
    assert nk % 2 == 0 and nk >= 4

    def chunk(c):
        return pl.ds(c * tk if isinstance(c, int) else pl.multiple_of(c * tk, tk), tk)

    def scores(c, s_ref):
        k = k_ref[chunk(c), :]
        mc = ()
        for i in range(parts):
            st = jnp.dot(k_of_part(k, i), qs[i], preferred_element_type=F32)
            s_ref[i] = st
            mc += (jnp.max(st, axis=0, keepdims=True),)
        return mc

    def softmax_pv(c, s_ref, mc, state):
        vt = vt_ref[:, chunk(c)]
        new = ()
        for i in range(parts):
            m, l = state[2 * i:2 * i + 2]
            m_new = jnp.maximum(m, mc[i])
            alpha = jnp.exp2(m - m_new)
            p = jnp.exp2(s_ref[i] - m_new)
            l = alpha * l + jnp.sum(p, axis=0, keepdims=True)
            acc_ref[i] = alpha * acc_ref[i] + jnp.dot(vt, p.astype(BF16), preferred_element_type=F32)
            new += (m_new, l)
        return new

    def body(c2, carry):
        mc_a, state = carry[:parts], carry[parts:]
        c = 2 * c2
        mc_b = scores(c + 1, sb_ref)
        state = softmax_pv(c, sa_ref, mc_a, state)
        mc_a = scores(c + 2, sa_ref)
        state = softmax_pv(c + 1, sb_ref, mc_b, state)
        return mc_a + state

    acc_ref[...] = jnp.zeros(acc_ref.shape, F32)
    state0 = (jnp.full((1, tp), NEG_BIG, F32), jnp.zeros((1, tp), F32)) * parts
    carry = lax.fori_loop(0, nk // 2 - 1, body, scores(0, sa_ref) + state0)
    mc_a, state = carry[:parts], carry[parts:]
    mc_b = scores(nk - 1, sb_ref)
    state = softmax_pv(nk - 2, sa_ref, mc_a, state)
    state = softmax_pv(nk - 1, sb_ref, mc_b, state)
    return state[1::2]


def _rope_pair(t, tab):
    t = t * tab
    return t + pltpu.roll(t, 64, axis=1)


def _mla_in_kernel(x_ref, g_ref, w_ref, qn_ref, kvn_ref, tab_ref, cq_ref, ckv_ref, kr_ref):
    xn = _rms(x_ref[...], g_ref[...], RMS_EPS).astype(BF16)
    h = jnp.dot(xn, w_ref[...], preferred_element_type=F32)
    cq_ref[...] = _rms(h[:, :MLA_Q_LORA], qn_ref[...], RMS_EPS).astype(BF16)
    ckv_ref[...] = _rms(h[:, MLA_Q_LORA:MLA_Q_LORA + MLA_KV_LORA], kvn_ref[...], RMS_EPS).astype(BF16)
    kr = _rope_pair(h[:, MLA_Q_LORA + MLA_KV_LORA:], tab_ref[...])
    lane = lax.broadcasted_iota(jnp.int32, kr.shape, 1)
    kr_ref[...] = jnp.where(lane < MLA_ROPE, kr, 0.0).astype(BF16)


def mla_in(x, g, w_in_ext, q_norm, kv_norm, tab, tm=512):
    m, d = x.shape
    n = w_in_ext.shape[1]
    row = lambda i: (i, 0)
    fixed = lambda i: (0, 0)
    return pl.pallas_call(
        _mla_in_kernel,
        grid=(m // tm,),
        in_specs=[
            pl.BlockSpec((tm, d), row),
            pl.BlockSpec((1, d), fixed),
            pl.BlockSpec((d, n), fixed),
            pl.BlockSpec((1, MLA_Q_LORA), fixed),
            pl.BlockSpec((1, MLA_KV_LORA), fixed),
            pl.BlockSpec((tm, 128), row),
        ],
        out_specs=[
            pl.BlockSpec((tm, MLA_Q_LORA), row),
            pl.BlockSpec((tm, MLA_KV_LORA), row),
            pl.BlockSpec((tm, 128), row),
        ],
        out_shape=[
            jax.ShapeDtypeStruct((m, MLA_Q_LORA), BF16),
            jax.ShapeDtypeStruct((m, MLA_KV_LORA), BF16),
            jax.ShapeDtypeStruct((m, 128), BF16),
        ],
        compiler_params=_params("parallel"),
        name="mla_in",
    )(x, g.reshape(1, d), w_in_ext, q_norm.reshape(1, -1), kv_norm.reshape(1, -1), tab)


def _mla_qt_kernel(c_ref, wt_ref, cos_ref, sin_ref, q_ref, *, heads):
    y = _dot_nt(wt_ref[...], c_ref[...]) * (MLA_SCALE * LOG2E)
    c = cos_ref[...]
    s = sin_ref[...]
    zeros = jnp.zeros((MLA_QK_PAD - MLA_QK, y.shape[1]), BF16)
    for h in range(heads):
        src = h * MLA_QK
        dst = h * MLA_QK_PAD
        q_ref[dst:dst + MLA_NOPE, :] = y[src:src + MLA_NOPE].astype(BF16)
        r1, r2 = _rope_rows(y[src + MLA_NOPE:src + MLA_QK], c, s)
        q_ref[dst + MLA_NOPE:dst + MLA_NOPE + 32, :] = r1.astype(BF16)
        q_ref[dst + MLA_NOPE + 32:dst + MLA_QK, :] = r2.astype(BF16)
        q_ref[dst + MLA_QK:dst + MLA_QK_PAD, :] = zeros


def mla_qt(cq, w_uq_t, cos_t, sin_t, tm=512, heads_per_step=4):
    m, k = cq.shape
    tn_in = heads_per_step * MLA_QK
    tn_out = heads_per_step * MLA_QK_PAD
    return pl.pallas_call(
        functools.partial(_mla_qt_kernel, heads=heads_per_step),
        grid=(m // tm, MLA_HEADS // heads_per_step),
        in_specs=[
            pl.BlockSpec((tm, k), lambda i, j: (i, 0)),
            pl.BlockSpec((tn_in, k), lambda i, j: (j, 0)),
            pl.BlockSpec((32, tm), lambda i, j: (0, i)),
            pl.BlockSpec((32, tm), lambda i, j: (0, i)),
        ],
        out_specs=pl.BlockSpec((tn_out, tm), lambda i, j: (j, i)),
        out_shape=jax.ShapeDtypeStruct((MLA_HEADS * MLA_QK_PAD, m), BF16),
        compiler_params=_params("parallel", "parallel"),
        name="mla_qt",
    )(cq, w_uq_t, cos_t, sin_t)


def _mla_kv_kernel(c_ref, wk_ref, wvt_ref, kr_ref, k_ref, vt_ref, *, heads):
    c = c_ref[...]
    yk = jnp.dot(c, wk_ref[...], preferred_element_type=F32)
    kr = kr_ref[...]
    for h in range(heads):
        k_ref[:, h * 256:h * 256 + 128] = yk[:, h * 128:(h + 1) * 128].astype(BF16)
        k_ref[:, h * 256 + 128:h * 256 + 256] = kr
    vt_ref[...] = _dot_nt(wvt_ref[...], c).astype(BF16)


def mla_kv(ckv, w_k, w_v_t, kr, tm=512, heads_per_step=4):
    m, k = ckv.shape
    hp = heads_per_step
    return pl.pallas_call(
        functools.partial(_mla_kv_kernel, heads=hp),
        grid=(m // tm, MLA_HEADS // hp),
        in_specs=[
            pl.BlockSpec((tm, k), lambda i, j: (i, 0)),
            pl.BlockSpec((k, hp * MLA_NOPE), lambda i, j: (0, j)),
            pl.BlockSpec((hp * MLA_V, k), lambda i, j: (j, 0)),
            pl.BlockSpec((tm, 128), lambda i, j: (i, 0)),
        ],
        out_specs=[
            pl.BlockSpec((tm, hp * MLA_QK_PAD), lambda i, j: (i, j)),
            pl.BlockSpec((hp * MLA_V, tm), lambda i, j: (j, i)),
        ],
        out_shape=[
            jax.ShapeDtypeStruct((m, MLA_HEADS * MLA_QK_PAD), BF16),
            jax.ShapeDtypeStruct((MLA_HEADS * MLA_V, m), BF16),
        ],
        compiler_params=_params("parallel", "parallel"),
        name="mla_kv",
    )(ckv, w_k, w_v_t, kr)


def _mla_attn_kernel(qt_ref, k_ref, vt_ref, o_ref, sa_ref, sb_ref, acc_ref, *, tk):
    parts, _, tp = sa_ref.shape
    qs = [qt_ref[:, i * tp:(i + 1) * tp] for i in range(parts)]
    ls = _flash_t(qs, lambda k, i: k, k_ref, vt_ref, sa_ref, sb_ref, acc_ref, tk=tk)
    for i in range(parts):
        o_ref[i * tp:(i + 1) * tp, :] = (acc_ref[i] / ls[i]).T.astype(o_ref.dtype)


def mla_attention(qt, k, vt, tq=512, tk=512, parts=2):
    s = k.shape[0]
    return pl.pallas_call(
        functools.partial(_mla_attn_kernel, tk=tk),
        grid=(MLA_HEADS, s // tq),
        in_specs=[
            pl.BlockSpec((MLA_QK_PAD, tq), lambda h, i: (h, i)),
            pl.BlockSpec((s, MLA_QK_PAD), lambda h, i: (0, h)),
            pl.BlockSpec((MLA_V, s), lambda h, i: (h, 0)),
        ],
        out_specs=pl.BlockSpec((tq, MLA_V), lambda h, i: (i, h)),
        out_shape=jax.ShapeDtypeStruct((s, MLA_HEADS * MLA_V), BF16),
        scratch_shapes=[
            pltpu.VMEM((parts, tk, tq // parts), F32),
            pltpu.VMEM((parts, tk, tq // parts), F32),
            pltpu.VMEM((parts, MLA_V, tq // parts), F32),
        ],
        compiler_params=_params("parallel", "parallel"),
        name="mla_attention",
    )(qt, k, vt)


def _rot_cols(w):
    half = w.shape[-1] // 2
    return jnp.concatenate([-w[..., half:], w[..., :half]], axis=-1)


def mla_layer(x, mix_norm, w_in, q_norm, w_uq, kv_norm, w_ukv, w_o, tab, cos_t, sin_t):
    w_kr = w_in[:, MLA_Q_LORA + MLA_KV_LORA:]
    w_in_ext = jnp.concatenate([w_in, _rot_cols(w_kr)], axis=1).astype(BF16)
    w_uq_t = w_uq.astype(BF16).T
    wkv = w_ukv.astype(BF16).reshape(MLA_KV_LORA, MLA_HEADS, MLA_NOPE + MLA_V)
    w_k = wkv[..., :MLA_NOPE].reshape(MLA_KV_LORA, -1)
    w_v_t = wkv[..., MLA_NOPE:].reshape(MLA_KV_LORA, -1).T
    cq, ckv, kr = mla_in(x, mix_norm, w_in_ext, q_norm, kv_norm, tab)
    qt = mla_qt(cq, w_uq_t, cos_t, sin_t)
    k, vt = mla_kv(ckv, w_k, w_v_t, kr)
    o = mla_attention(qt, k, vt)
    return matmul_residual(o, w_o.astype(BF16), x)


def _diff_attn_kernel(lam_ref, qt_ref, k_ref, vt_ref, g_ref, o_ref, sa_ref, sb_ref, acc_ref, *, tk, lam_init):
    d = DIFF_HEAD_DIM
    qs = [qt_ref[:d, :], qt_ref[d:, :]]
    l0, l1 = _flash_t(qs, lambda k, i: k[:, i * d:(i + 1) * d], k_ref, vt_ref, sa_ref, sb_ref, acc_ref, tk=tk)
    ot = acc_ref[0] / l0 - lam_ref[0, 0] * (acc_ref[1] / l1)
    o = _rms(ot.T, g_ref[...], DIFF_EPS) * (1.0 - lam_init)
    o_ref[...] = o.astype(o_ref.dtype)


def diff_attention(qt, k, vt, lam, subln, lam_init, tq=256, tk=512):
    s = k.shape[0]
    hd = 2 * DIFF_HEAD_DIM
    return pl.pallas_call(
        functools.partial(_diff_attn_kernel, tk=tk, lam_init=lam_init),
        grid=(DIFF_HEADS, s // tq),
        in_specs=[
            pl.BlockSpec(memory_space=pltpu.SMEM),
            pl.BlockSpec((hd, tq), lambda h, i: (h, i)),
            pl.BlockSpec((s, hd), lambda h, i: (0, h)),
            pl.BlockSpec((hd, s), lambda h, i: (h, 0)),
            pl.BlockSpec((1, hd), lambda h, i: (0, 0)),
        ],
        out_specs=pl.BlockSpec((tq, hd), lambda h, i: (i, h)),
        out_shape=jax.ShapeDtypeStruct((s, DIFF_HEADS * hd), BF16),
        scratch_shapes=[
            pltpu.VMEM((2, tk, tq), F32),
            pltpu.VMEM((2, tk, tq), F32),
            pltpu.VMEM((2, hd, tq), F32),
        ],
        compiler_params=_params("parallel", "parallel"),
        name="diff_attention",
    )(lam, qt, k, vt, subln.reshape(1, hd))


def _lambda_kernel(lq1_ref, lk1_ref, lq2_ref, lk2_ref, o_ref, *, lam_init):
    a = jnp.sum(lq1_ref[...] * lk1_ref[...], axis=-1, keepdims=True)
    b = jnp.sum(lq2_ref[...] * lk2_ref[...], axis=-1, keepdims=True)
    o_ref[...] = jnp.exp(a) - jnp.exp(b) + lam_init


def diff_lambda(lq1, lk1, lq2, lk2, lam_init):
    args = [v.reshape(1, -1) for v in (lq1, lk1, lq2, lk2)]
    return pl.pallas_call(
        functools.partial(_lambda_kernel, lam_init=lam_init),
        out_shape=jax.ShapeDtypeStruct((1, 1), F32),
        name="diff_lambda",
    )(*args)


def diff_layer(x, mix_norm, w_qkv, lq1, lk1, lq2, lk2, subln, w_o, rope, rope_t, layer_idx):
    lam_init = 0.8 - 0.6 * math.exp(-0.3 * layer_idx)
    w = w_qkv.astype(BF16)
    qt = norm_matmul_t(x, mix_norm, w[:, :DIFF_QK].T, BF16, rope_t=rope_t, scale=DIFF_SCALE * LOG2E)
    k = norm_matmul(x, mix_norm, w[:, DIFF_QK:2 * DIFF_QK], BF16, rope=rope)
    vt = norm_matmul_t(x, mix_norm, w[:, 2 * DIFF_QK:].T, BF16)
    lam = diff_lambda(lq1, lk1, lq2, lk2, lam_init)
    o = diff_attention(qt, k, vt, lam, subln, lam_init)
    return matmul_residual(o, w_o.astype(BF16), x)


def _hyena_filter_spectrum(seq, f_w1, f_b1, f_freq, f_w2, f_b2, f_w3, f_b3, f_w4):
    t = jnp.linspace(0.0, 1.0, seq, dtype=F32)[:, None]
    bands = (HY_EMB_DIM - 1) // 2
    w = 2.0 * math.pi * jnp.arange(seq, dtype=F32)[:, None] / seq
    f = jnp.linspace(1e-4, bands - 1, bands, dtype=F32)[None, :]
    z = jnp.concatenate([t, jnp.cos(f * w), -jnp.sin(f * w)], axis=-1)
    h = jnp.sin(f_freq * (z @ f_w1 + f_b1))
    h = jnp.sin(f_freq * (h @ f_w2 + f_b2))
    h = jnp.sin(f_freq * (h @ f_w3 + f_b3))
    h = (h @ f_w4).reshape(seq, 2, HY_ORDER, D_MODEL)
    deltas = jnp.abs(jnp.linspace(HY_MIN_DECAY, HY_MAX_DECAY, D_MODEL, dtype=F32))
    decay = jnp.exp(-t * deltas[None, :])
    h = h * decay[:, None, None, :]
    kern = jnp.concatenate([h[:, 0], jnp.zeros((1, HY_ORDER, D_MODEL), F32), h[:0:-1, 1]], axis=0)
    kern = kern / jnp.sum(jnp.abs(kern), axis=0, keepdims=True)
    return jnp.fft.rfft(kern, axis=0)


def hyena_layer(x, mix_norm, w_in, conv_w, conv_b, f_w1, f_b1, f_freq, f_w2, f_b2, f_w3, f_b3, f_w4, bias, w_out):
    s = x.shape[0]
    u = norm_matmul(x, mix_norm, w_in.astype(BF16), F32)
    up = jnp.pad(u, ((1, 1), (0, 0)))
    u = up[:-2] * conv_w[0] + up[1:-1] * conv_w[1] + up[2:] * conv_w[2] + conv_b
    x1, x2, z = jnp.split(u, 3, axis=-1)
    spec = _hyena_filter_spectrum(s, f_w1, f_b1, f_freq, f_w2, f_b2, f_w3, f_b3, f_w4)
    for o, gate in enumerate((x1, x2)):
        zf = jnp.fft.rfft(z, n=2 * s, axis=0)
        conv = jnp.fft.irfft(zf * spec[:, o, :], n=2 * s, axis=0)[:s]
        z = gate * (conv + z * bias[o])
    return matmul_residual(z.astype(BF16), w_out.astype(BF16), x)


def _rope_tables(seq, dim):
    inv = 1.0 / (ROPE_THETA ** (jnp.arange(0, dim, 2, dtype=F32) / dim))
    ang = jnp.arange(seq, dtype=F32)[:, None] * inv[None, :]
    return jnp.cos(ang), jnp.sin(ang)


def kernel(x, l0_mix_norm, l0_mla_w_in, l0_mla_q_norm, l0_mla_w_uq, l0_mla_kv_norm, l0_mla_w_ukv, l0_mla_w_o, l0_ffn_norm, l0_ffn_w_gate, l0_ffn_w_up, l0_ffn_w_down, l1_mix_norm, l1_hy_w_in, l1_hy_conv_w, l1_hy_conv_b, l1_hy_f_w1, l1_hy_f_b1, l1_hy_f_freq, l1_hy_f_w2, l1_hy_f_b2, l1_hy_f_w3, l1_hy_f_b3, l1_hy_f_w4, l1_hy_bias, l1_hy_w_out, l1_ffn_norm, l1_ffn_w_gate, l1_ffn_w_up, l1_ffn_w_down, l2_mix_norm, l2_da_w_qkv, l2_da_lq1, l2_da_lk1, l2_da_lq2, l2_da_lk2, l2_da_subln, l2_da_w_o, l2_ffn_norm, l2_ffn_w_gate, l2_ffn_w_up, l2_ffn_w_down, l3_mix_norm, l3_mla_w_in, l3_mla_q_norm, l3_mla_w_uq, l3_mla_kv_norm, l3_mla_w_ukv, l3_mla_w_o, l3_ffn_norm, l3_ffn_w_gate, l3_ffn_w_up, l3_ffn_w_down, final_norm):
    b, s, d = x.shape
    assert b == 1
    h = x.reshape(s, d)

    mc, ms = _rope_tables(s, MLA_ROPE)
    mla_tab = jnp.concatenate([mc, mc, ms, ms], axis=-1)
    mla = (mla_tab, mc.T, ms.T)
    dc, ds = _rope_tables(s, DIFF_HEAD_DIM)
    diff_rope = (jnp.concatenate([dc, dc], axis=-1), jnp.concatenate([-ds, ds], axis=-1))
    diff_rope_t = (dc.T, ds.T)

    def ffn(h, g, wg, wu, wd):
        return ffn_block(h, g, wg.astype(BF16), wu.astype(BF16), wd.astype(BF16))

    h = mla_layer(h, l0_mix_norm, l0_mla_w_in, l0_mla_q_norm, l0_mla_w_uq, l0_mla_kv_norm, l0_mla_w_ukv, l0_mla_w_o, *mla)
    h = ffn(h, l0_ffn_norm, l0_ffn_w_gate, l0_ffn_w_up, l0_ffn_w_down)
    h = hyena_layer(h, l1_mix_norm, l1_hy_w_in, l1_hy_conv_w, l1_hy_conv_b, l1_hy_f_w1, l1_hy_f_b1, l1_hy_f_freq, l1_hy_f_w2, l1_hy_f_b2, l1_hy_f_w3, l1_hy_f_b3, l1_hy_f_w4, l1_hy_bias, l1_hy_w_out)
    h = ffn(h, l1_ffn_norm, l1_ffn_w_gate, l1_ffn_w_up, l1_ffn_w_down)
    h = diff_layer(h, l2_mix_norm, l2_da_w_qkv, l2_da_lq1, l2_da_lk1, l2_da_lq2, l2_da_lk2, l2_da_subln, l2_da_w_o, diff_rope, diff_rope_t, 2)
    h = ffn(h, l2_ffn_norm, l2_ffn_w_gate, l2_ffn_w_up, l2_ffn_w_down)
    h = mla_layer(h, l3_mix_norm, l3_mla_w_in, l3_mla_q_norm, l3_mla_w_uq, l3_mla_kv_norm, l3_mla_w_ukv, l3_mla_w_o, *mla)
    h = ffn(h, l3_ffn_norm, l3_ffn_w_gate, l3_ffn_w_up, l3_ffn_w_down)
    return rmsnorm(h, final_norm).reshape(b, s, d)
```

```python
import functools
import math

import jax
import jax.numpy as jnp
import numpy as np
from jax import lax
from jax.experimental import pallas as pl
from jax.experimental.pallas import tpu as pltpu

F32 = jnp.float32
BF16 = jnp.bfloat16

D_MODEL = 2048
SEQ = 8192
RMS_EPS = 1e-6
ROPE_THETA = 10000.0
LOG2E = math.log2(math.e)

MLA_HEADS = 16
MLA_Q_LORA = 768
MLA_KV_LORA = 512
MLA_NOPE = 128
MLA_ROPE = 64
MLA_V = 128
MLA_QK = MLA_NOPE + MLA_ROPE
MLA_SCALE = MLA_QK ** -0.5
MLA_QK_PAD = 256

HY_ORDER = 2
HY_EMB_DIM = 33
HY_TARGET = 1e-2
HY_MAX_DECAY = math.log(HY_TARGET) / 0.3
HY_MIN_DECAY = math.log(HY_TARGET) / 1.5

DIFF_HEAD_DIM = 128
DIFF_HEADS = D_MODEL // (2 * DIFF_HEAD_DIM)
DIFF_QK = DIFF_HEADS * 2 * DIFF_HEAD_DIM
DIFF_SCALE = DIFF_HEAD_DIM ** -0.5
DIFF_EPS = 1e-5

D_FF = -(-8 * D_MODEL // (3 * 256)) * 256

VMEM_LIMIT = 56 * 1024 * 1024
NEG_BIG = -1e30


def _params(*sem):
    return pltpu.CompilerParams(dimension_semantics=sem, vmem_limit_bytes=VMEM_LIMIT)


def _rms(x, g, eps):
    return x * lax.rsqrt(jnp.mean(x * x, axis=-1, keepdims=True) + eps) * g


def _dot_nt(a, b):
    return lax.dot_general(a, b, (((1,), (1,)), ((), ())), preferred_element_type=F32)


def _rope_rows(y, c, s):
    h = c.shape[0]
    y1, y2 = y[:h], y[h:]
    return y1 * c - y2 * s, y2 * c + y1 * s


def _norm_matmul_kernel(x_ref, g_ref, w_ref, *rest, rope, scale):
    o_ref, xn_ref = rest[-2:]

    @pl.when(pl.program_id(1) == 0)
    def _():
        xn_ref[...] = _rms(x_ref[...], g_ref[...], RMS_EPS).astype(BF16)

    y = jnp.dot(xn_ref[...], w_ref[...], preferred_element_type=F32)
    if rope:
        c = rest[0][...] * scale
        s = rest[1][...] * scale
        for g in range(y.shape[1] // 128):
            t = y[:, g * 128:(g + 1) * 128]
            o_ref[:, g * 128:(g + 1) * 128] = (t * c + pltpu.roll(t, 64, axis=1) * s).astype(o_ref.dtype)
    else:
        o_ref[...] = (y * scale).astype(o_ref.dtype)


def norm_matmul(x, g, w, out_dtype, rope=None, scale=1.0, tm=512, tn=512):
    m, k = x.shape
    n = w.shape[1]
    in_specs = [
        pl.BlockSpec((tm, k), lambda i, j: (i, 0)),
        pl.BlockSpec((1, k), lambda i, j: (0, 0)),
        pl.BlockSpec((k, tn), lambda i, j: (0, j)),
    ]
    args = [x, g.reshape(1, k), w]
    if rope is not None:
        in_specs += [pl.BlockSpec((tm, 128), lambda i, j: (i, 0))] * 2
        args += list(rope)
    return pl.pallas_call(
        functools.partial(_norm_matmul_kernel, rope=rope is not None, scale=scale),
        grid=(m // tm, n // tn),
        in_specs=in_specs,
        out_specs=pl.BlockSpec((tm, tn), lambda i, j: (i, j)),
        out_shape=jax.ShapeDtypeStruct((m, n), out_dtype),
        scratch_shapes=[pltpu.VMEM((tm, k), BF16)],
        compiler_params=_params("parallel", "arbitrary"),
        name="norm_matmul",
    )(*args)


def _norm_matmul_t_kernel(x_ref, g_ref, wt_ref, *rest, rope, scale):
    o_ref, xn_ref = rest[-2:]

    @pl.when(pl.program_id(1) == 0)
    def _():
        xn_ref[...] = _rms(x_ref[...], g_ref[...], RMS_EPS).astype(BF16)

    y = _dot_nt(wt_ref[...], xn_ref[...]) * scale
    if rope:
        c = rest[0][...]
        s = rest[1][...]
        for g in range(y.shape[0] // 128):
            r1, r2 = _rope_rows(y[g * 128:(g + 1) * 128], c, s)
            o_ref[g * 128:g * 128 + 64, :] = r1.astype(o_ref.dtype)
            o_ref[g * 128 + 64:(g + 1) * 128, :] = r2.astype(o_ref.dtype)
    else:
        o_ref[...] = y.astype(o_ref.dtype)


def norm_matmul_t(x, g, wt, out_dtype, rope_t=None, scale=1.0, tm=512, tn=512):
    m, k = x.shape
    n = wt.shape[0]
    in_specs = [
        pl.BlockSpec((tm, k), lambda i, j: (i, 0)),
        pl.BlockSpec((1, k), lambda i, j: (0, 0)),
        pl.BlockSpec((tn, k), lambda i, j: (j, 0)),
    ]
    args = [x, g.reshape(1, k), wt]
    if rope_t is not None:
        in_specs += [pl.BlockSpec((64, tm), lambda i, j: (0, i))] * 2
        args += list(rope_t)
    return pl.pallas_call(
        functools.partial(_norm_matmul_t_kernel, rope=rope_t is not None, scale=scale),
        grid=(m // tm, n // tn),
        in_specs=in_specs,
        out_specs=pl.BlockSpec((tn, tm), lambda i, j: (j, i)),
        out_shape=jax.ShapeDtypeStruct((n, m), out_dtype),
        scratch_shapes=[pltpu.VMEM((tm, k), BF16)],
        compiler_params=_params("parallel", "arbitrary"),
        name="norm_matmul_t",
    )(*args)


def _matmul_res_kernel(a_ref, w_ref, r_ref, o_ref):
    o_ref[...] = r_ref[...] + jnp.dot(a_ref[...].astype(BF16), w_ref[...], preferred_element_type=F32)


def matmul_residual(a, w, res, tm=512, tn=512):
    m, k = a.shape
    n = w.shape[1]
    return pl.pallas_call(
        _matmul_res_kernel,
        grid=(m // tm, n // tn),
        in_specs=[
            pl.BlockSpec((tm, k), lambda i, j: (i, 0)),
            pl.BlockSpec((k, tn), lambda i, j: (0, j)),
            pl.BlockSpec((tm, tn), lambda i, j: (i, j)),
        ],
        out_specs=pl.BlockSpec((tm, tn), lambda i, j: (i, j)),
        out_shape=jax.ShapeDtypeStruct((m, n), F32),
        compiler_params=_params("parallel", "parallel"),
        name="matmul_residual",
    )(a, w, res)


def _ffn_kernel(x_ref, g_ref, wg_ref, wu_ref, wd_ref, o_ref, xn_ref):
    @pl.when(pl.program_id(1) == 0)
    def _():
        x = x_ref[...]
        xn_ref[...] = _rms(x, g_ref[...], RMS_EPS).astype(BF16)
        o_ref[...] = x

    xn = xn_ref[...]
    a = jnp.dot(xn, wg_ref[...], preferred_element_type=F32)
    b = jnp.dot(xn, wu_ref[...], preferred_element_type=F32)
    h = (a * jax.nn.sigmoid(a) * b).astype(BF16)
    o_ref[...] += jnp.dot(h, wd_ref[...], preferred_element_type=F32)


def ffn_block(x, g, wg, wu, wd, tm=512, tf=512):
    m, d = x.shape
    f = wg.shape[1]
    return pl.pallas_call(
        _ffn_kernel,
        grid=(m // tm, f // tf),
        in_specs=[
            pl.BlockSpec((tm, d), lambda i, j: (i, 0)),
            pl.BlockSpec((1, d), lambda i, j: (0, 0)),
            pl.BlockSpec((d, tf), lambda i, j: (0, j)),
            pl.BlockSpec((d, tf), lambda i, j: (0, j)),
            pl.BlockSpec((tf, d), lambda i, j: (j, 0)),
        ],
        out_specs=pl.BlockSpec((tm, d), lambda i, j: (i, 0)),
        out_shape=jax.ShapeDtypeStruct((m, d), F32),
        scratch_shapes=[pltpu.VMEM((tm, d), BF16)],
        compiler_params=_params("parallel", "arbitrary"),
        name="ffn_block",
    )(x, g.reshape(1, d), wg, wu, wd)


def _rmsnorm_kernel(x_ref, g_ref, o_ref):
    o_ref[...] = _rms(x_ref[...], g_ref[...], RMS_EPS)


def rmsnorm(x, g, tm=512):
    m, d = x.shape
    return pl.pallas_call(
        _rmsnorm_kernel,
        grid=(m // tm,),
        in_specs=[pl.BlockSpec((tm, d), lambda i: (i, 0)), pl.BlockSpec((1, d), lambda i: (0, 0))],
        out_specs=pl.BlockSpec((tm, d), lambda i: (i, 0)),
        out_shape=jax.ShapeDtypeStruct((m, d), F32),
        compiler_params=_params("parallel"),
        name="final_rmsnorm",
    )(x, g.reshape(1, d))


def _flash_t(qs, k_of_part, k_ref, vt_ref, sa_ref, sb_ref, acc_ref, *, tk):
    parts = len(qs)
    tp = qs[0].shape[1]
    nk = k_ref.shape[0] // tk
    assert nk % 2 == 0 and nk >= 4

    def chunk(c):
        return pl.ds(c * tk if isinstance(c, int) else pl.multiple_of(c * tk, tk), tk)

    def scores(c, s_ref):
        k = k_ref[chunk(c), :]
        mc = ()
        for i in range(parts):
            st = jnp.dot(k_of_part(k, i), qs[i], preferred_element_type=F32)
            s_ref[i] = st
            mc += (jnp.max(st, axis=0, keepdims=True),)
        return mc

    def softmax_pv(c, s_ref, mc, state):
        vt = vt_ref[:, chunk(c)]
        new = ()
        for i in range(parts):
            m, l = state[2 * i:2 * i + 2]
            m_new = jnp.maximum(m, mc[i])
            alpha = jnp.exp2(m - m_new)
            p = jnp.exp2(s_ref[i] - m_new)
            l = alpha * l + jnp.sum(p, axis=0, keepdims=True)
            acc_ref[i] = alpha * acc_ref[i] + jnp.dot(vt, p.astype(BF16), preferred_element_type=F32)
            new += (m_new, l)
        return new

    def body(c2, carry):
        mc_a, state = carry[:parts], carry[parts:]
        c = 2 * c2
        mc_b = scores(c + 1, sb_ref)
        state = softmax_pv(c, sa_ref, mc_a, state)
        mc_a = scores(c + 2, sa_ref)
        state = softmax_pv(c + 1, sb_ref, mc_b, state)
        return mc_a + state

    acc_ref[...] = jnp.zeros(acc_ref.shape, F32)
    state0 = (jnp.full((1, tp), NEG_BIG, F32), jnp.zeros((1, tp), F32)) * parts
    carry = lax.fori_loop(0, nk // 2 - 1, body, scores(0, sa_ref) + state0)
    mc_a, state = carry[:parts], carry[parts:]
    mc_b = scores(nk - 1, sb_ref)
    state = softmax_pv(nk - 2, sa_ref, mc_a, state)
    state = softmax_pv(nk - 1, sb_ref, mc_b, state)
    return state[1::2]


def _rope_pair(t, tab):
    t = t * tab
    return t + pltpu.roll(t, 64, axis=1)


def _mla_in_kernel(x_ref, g_ref, w_ref, qn_ref, kvn_ref, tab_ref, cq_ref, ckv_ref, kr_ref):
    xn = _rms(x_ref[...], g_ref[...], RMS_EPS).astype(BF16)
    h = jnp.dot(xn, w_ref[...], preferred_element_type=F32)
    cq_ref[...] = _rms(h[:, :MLA_Q_LORA], qn_ref[...], RMS_EPS).astype(BF16)
    ckv_ref[...] = _rms(h[:, MLA_Q_LORA:MLA_Q_LORA + MLA_KV_LORA], kvn_ref[...], RMS_EPS).astype(BF16)
    kr = _rope_pair(h[:, MLA_Q_LORA + MLA_KV_LORA:], tab_ref[...])
    lane = lax.broadcasted_iota(jnp.int32, kr.shape, 1)
    kr_ref[...] = jnp.where(lane < MLA_ROPE, kr, 0.0).astype(BF16)


def mla_in(x, g, w_in_ext, q_norm, kv_norm, tab, tm=512):
    m, d = x.shape
    n = w_in_ext.shape[1]
    row = lambda i: (i, 0)
    fixed = lambda i: (0, 0)
    return pl.pallas_call(
        _mla_in_kernel,
        grid=(m // tm,),
        in_specs=[
            pl.BlockSpec((tm, d), row),
            pl.BlockSpec((1, d), fixed),
            pl.BlockSpec((d, n), fixed),
            pl.BlockSpec((1, MLA_Q_LORA), fixed),
            pl.BlockSpec((1, MLA_KV_LORA), fixed),
            pl.BlockSpec((tm, 128), row),
        ],
        out_specs=[
            pl.BlockSpec((tm, MLA_Q_LORA), row),
            pl.BlockSpec((tm, MLA_KV_LORA), row),
            pl.BlockSpec((tm, 128), row),
        ],
        out_shape=[
            jax.ShapeDtypeStruct((m, MLA_Q_LORA), BF16),
            jax.ShapeDtypeStruct((m, MLA_KV_LORA), BF16),
            jax.ShapeDtypeStruct((m, 128), BF16),
        ],
        compiler_params=_params("parallel"),
        name="mla_in",
    )(x, g.reshape(1, d), w_in_ext, q_norm.reshape(1, -1), kv_norm.reshape(1, -1), tab)


def _mla_qt_kernel(c_ref, wt_ref, cos_ref, sin_ref, q_ref, *, heads):
    y = _dot_nt(wt_ref[...], c_ref[...]) * (MLA_SCALE * LOG2E)
    c = cos_ref[...]
    s = sin_ref[...]
    zeros = jnp.zeros((MLA_QK_PAD - MLA_QK, y.shape[1]), BF16)
    for h in range(heads):
        src = h * MLA_QK
        dst = h * MLA_QK_PAD
        q_ref[dst:dst + MLA_NOPE, :] = y[src:src + MLA_NOPE].astype(BF16)
        r1, r2 = _rope_rows(y[src + MLA_NOPE:src + MLA_QK], c, s)
        q_ref[dst + MLA_NOPE:dst + MLA_NOPE + 32, :] = r1.astype(BF16)
        q_ref[dst + MLA_NOPE + 32:dst + MLA_QK, :] = r2.astype(BF16)
        q_ref[dst + MLA_QK:dst + MLA_QK_PAD, :] = zeros


def mla_qt(cq, w_uq_t, cos_t, sin_t, tm=512, heads_per_step=4):
    m, k = cq.shape
    tn_in = heads_per_step * MLA_QK
    tn_out = heads_per_step * MLA_QK_PAD
    return pl.pallas_call(
        functools.partial(_mla_qt_kernel, heads=heads_per_step),
        grid=(m // tm, MLA_HEADS // heads_per_step),
        in_specs=[
            pl.BlockSpec((tm, k), lambda i, j: (i, 0)),
            pl.BlockSpec((tn_in, k), lambda i, j: (j, 0)),
            pl.BlockSpec((32, tm), lambda i, j: (0, i)),
            pl.BlockSpec((32, tm), lambda i, j: (0, i)),
        ],
        out_specs=pl.BlockSpec((tn_out, tm), lambda i, j: (j, i)),
        out_shape=jax.ShapeDtypeStruct((MLA_HEADS * MLA_QK_PAD, m), BF16),
        compiler_params=_params("parallel", "parallel"),
        name="mla_qt",
    )(cq, w_uq_t, cos_t, sin_t)


def _mla_kv_kernel(c_ref, wk_ref, wvt_ref, kr_ref, k_ref, vt_ref, *, heads):
    c = c_ref[...]
    yk = jnp.dot(c, wk_ref[...], preferred_element_type=F32)
    kr = kr_ref[...]
    for h in range(heads):
        k_ref[:, h * 256:h * 256 + 128] = yk[:, h * 128:(h + 1) * 128].astype(BF16)
        k_ref[:, h * 256 + 128:h * 256 + 256] = kr
    vt_ref[...] = _dot_nt(wvt_ref[...], c).astype(BF16)


def mla_kv(ckv, w_k, w_v_t, kr, tm=512, heads_per_step=4):
    m, k = ckv.shape
    hp = heads_per_step
    return pl.pallas_call(
        functools.partial(_mla_kv_kernel, heads=hp),
        grid=(m // tm, MLA_HEADS // hp),
        in_specs=[
            pl.BlockSpec((tm, k), lambda i, j: (i, 0)),
            pl.BlockSpec((k, hp * MLA_NOPE), lambda i, j: (0, j)),
            pl.BlockSpec((hp * MLA_V, k), lambda i, j: (j, 0)),
            pl.BlockSpec((tm, 128), lambda i, j: (i, 0)),
        ],
        out_specs=[
            pl.BlockSpec((tm, hp * MLA_QK_PAD), lambda i, j: (i, j)),
            pl.BlockSpec((hp * MLA_V, tm), lambda i, j: (j, i)),
        ],
        out_shape=[
            jax.ShapeDtypeStruct((m, MLA_HEADS * MLA_QK_PAD), BF16),
            jax.ShapeDtypeStruct((MLA_HEADS * MLA_V, m), BF16),
        ],
        compiler_params=_params("parallel", "parallel"),
        name="mla_kv",
    )(ckv, w_k, w_v_t, kr)


def _mla_attn_kernel(qt_ref, k_ref, vt_ref, o_ref, sa_ref, sb_ref, acc_ref, *, tk):
    parts, _, tp = sa_ref.shape
    qs = [qt_ref[:, i * tp:(i + 1) * tp] for i in range(parts)]
    ls = _flash_t(qs, lambda k, i: k, k_ref, vt_ref, sa_ref, sb_ref, acc_ref, tk=tk)
    for i in range(parts):
        o_ref[i * tp:(i + 1) * tp, :] = (acc_ref[i] / ls[i]).T.astype(o_ref.dtype)


def mla_attention(qt, k, vt, tq=512, tk=512, parts=2):
    s = k.shape[0]
    return pl.pallas_call(
        functools.partial(_mla_attn_kernel, tk=tk),
        grid=(MLA_HEADS, s // tq),
        in_specs=[
            pl.BlockSpec((MLA_QK_PAD, tq), lambda h, i: (h, i)),
            pl.BlockSpec((s, MLA_QK_PAD), lambda h, i: (0, h)),
            pl.BlockSpec((MLA_V, s), lambda h, i: (h, 0)),
        ],
        out_specs=pl.BlockSpec((tq, MLA_V), lambda h, i: (i, h)),
        out_shape=jax.ShapeDtypeStruct((s, MLA_HEADS * MLA_V), BF16),
        scratch_shapes=[
            pltpu.VMEM((parts, tk, tq // parts), F32),
            pltpu.VMEM((parts, tk, tq // parts), F32),
            pltpu.VMEM((parts, MLA_V, tq // parts), F32),
        ],
        compiler_params=_params("parallel", "parallel"),
        name="mla_attention",
    )(qt, k, vt)


def _rot_cols(w):
    half = w.shape[-1] // 2
    return jnp.concatenate([-w[..., half:], w[..., :half]], axis=-1)


def mla_layer(x, mix_norm, w_in, q_norm, w_uq, kv_norm, w_ukv, w_o, tab, cos_t, sin_t):
    w_kr = w_in[:, MLA_Q_LORA + MLA_KV_LORA:]
    w_in_ext = jnp.concatenate([w_in, _rot_cols(w_kr)], axis=1).astype(BF16)
    w_uq_t = w_uq.astype(BF16).T
    wkv = w_ukv.astype(BF16).reshape(MLA_KV_LORA, MLA_HEADS, MLA_NOPE + MLA_V)
    w_k = wkv[..., :MLA_NOPE].reshape(MLA_KV_LORA, -1)
    w_v_t = wkv[..., MLA_NOPE:].reshape(MLA_KV_LORA, -1).T
    cq, ckv, kr = mla_in(x, mix_norm, w_in_ext, q_norm, kv_norm, tab)
    qt = mla_qt(cq, w_uq_t, cos_t, sin_t)
    k, vt = mla_kv(ckv, w_k, w_v_t, kr)
    o = mla_attention(qt, k, vt)
    return matmul_residual(o, w_o.astype(BF16), x)


def _diff_attn_kernel(lam_ref, qt_ref, k_ref, vt_ref, g_ref, o_ref, sa_ref, sb_ref, acc_ref, *, tk, lam_init):
    d = DIFF_HEAD_DIM
    qs = [qt_ref[:d, :], qt_ref[d:, :]]
    l0, l1 = _flash_t(qs, lambda k, i: k[:, i * d:(i + 1) * d], k_ref, vt_ref, sa_ref, sb_ref, acc_ref, tk=tk)
    ot = acc_ref[0] / l0 - lam_ref[0, 0] * (acc_ref[1] / l1)
    o = _rms(ot.T, g_ref[...], DIFF_EPS) * (1.0 - lam_init)
    o_ref[...] = o.astype(o_ref.dtype)


def diff_attention(qt, k, vt, lam, subln, lam_init, tq=256, tk=512):
    s = k.shape[0]
    hd = 2 * DIFF_HEAD_DIM
    return pl.pallas_call(
        functools.partial(_diff_attn_kernel, tk=tk, lam_init=lam_init),
        grid=(DIFF_HEADS, s // tq),
        in_specs=[
            pl.BlockSpec(memory_space=pltpu.SMEM),
            pl.BlockSpec((hd, tq), lambda h, i: (h, i)),
            pl.BlockSpec((s, hd), lambda h, i: (0, h)),
            pl.BlockSpec((hd, s), lambda h, i: (h, 0)),
            pl.BlockSpec((1, hd), lambda h, i: (0, 0)),
        ],
        out_specs=pl.BlockSpec((tq, hd), lambda h, i: (i, h)),
        out_shape=jax.ShapeDtypeStruct((s, DIFF_HEADS * hd), BF16),
        scratch_shapes=[
            pltpu.VMEM((2, tk, tq), F32),
            pltpu.VMEM((2, tk, tq), F32),
            pltpu.VMEM((2, hd, tq), F32),
        ],
        compiler_params=_params("parallel", "parallel"),
        name="diff_attention",
    )(lam, qt, k, vt, subln.reshape(1, hd))


def _lambda_kernel(lq1_ref, lk1_ref, lq2_ref, lk2_ref, o_ref, *, lam_init):
    a = jnp.sum(lq1_ref[...] * lk1_ref[...], axis=-1, keepdims=True)
    b = jnp.sum(lq2_ref[...] * lk2_ref[...], axis=-1, keepdims=True)
    o_ref[...] = jnp.exp(a) - jnp.exp(b) + lam_init


def diff_lambda(lq1, lk1, lq2, lk2, lam_init):
    args = [v.reshape(1, -1) for v in (lq1, lk1, lq2, lk2)]
    return pl.pallas_call(
        functools.partial(_lambda_kernel, lam_init=lam_init),
        out_shape=jax.ShapeDtypeStruct((1, 1), F32),
        name="diff_lambda",
    )(*args)


def diff_layer(x, mix_norm, w_qkv, lq1, lk1, lq2, lk2, subln, w_o, rope, rope_t, layer_idx):
    lam_init = 0.8 - 0.6 * math.exp(-0.3 * layer_idx)
    w = w_qkv.astype(BF16)
    qt = norm_matmul_t(x, mix_norm, w[:, :DIFF_QK].T, BF16, rope_t=rope_t, scale=DIFF_SCALE * LOG2E)
    k = norm_matmul(x, mix_norm, w[:, DIFF_QK:2 * DIFF_QK], BF16, rope=rope)
    vt = norm_matmul_t(x, mix_norm, w[:, 2 * DIFF_QK:].T, BF16)
    lam = diff_lambda(lq1, lk1, lq2, lk2, lam_init)
    o = diff_attention(qt, k, vt, lam, subln, lam_init)
    return matmul_residual(o, w_o.astype(BF16), x)


FFT_R = 128
FFT_N = FFT_R * FFT_R
FFT_G = 8
HY_TC = 512


def _dft_constants():
    r = np.arange(FFT_R)
    n2, k1, n1 = r[:, None, None], r[None, :, None], r[None, None, :]
    th = 2.0 * np.pi * ((k1 * (FFT_R * n1 + n2)) % FFT_N) / FFT_N
    w1 = np.concatenate([np.cos(th), -np.sin(th)], axis=1)
    tht = np.transpose(th, (0, 2, 1))
    v1 = np.concatenate([np.cos(tht), -np.sin(tht)], axis=2) / FFT_N
    ph = 2.0 * np.pi * ((r[:, None] * r[None, :]) % FFT_R) / FFT_R
    c, s = np.cos(ph), np.sin(ph)
    g_fwd = np.block([[c, s], [-s, c]])
    g_inv = np.block([[c, -s], [s, c]])
    as_bf16 = lambda a: a.astype(np.float32).astype(BF16)
    half = FFT_R // 2
    return dict(w1_full=as_bf16(w1), w1_half=as_bf16(w1[:, :, :half]), v1_half=as_bf16(v1[:, :half, :]),
                g_fwd=as_bf16(g_fwd), g_inv=as_bf16(g_inv))


def _short_conv_kernel(u_ref, prev_ref, next_ref, w_ref, b_ref, o_ref):
    i = pl.program_id(0)
    x = u_ref[...]
    tm = x.shape[0]
    row = lax.broadcasted_iota(jnp.int32, x.shape, 0)
    before = jnp.where(i > 0, prev_ref[7:8, :], 0.0)
    after = jnp.where(i < pl.num_programs(0) - 1, next_ref[0:1, :], 0.0)
    up = jnp.where(row == 0, before, pltpu.roll(x, 1, axis=0))
    dn = jnp.where(row == tm - 1, after, pltpu.roll(x, tm - 1, axis=0))
    o_ref[...] = up * w_ref[0:1, :] + x * w_ref[1:2, :] + dn * w_ref[2:3, :] + b_ref[...]


def short_conv(u, conv_w, conv_b, tm=1024, tc=512):
    s, n = u.shape
    nb = tm // 8
    return pl.pallas_call(
        _short_conv_kernel,
        grid=(s // tm, n // tc),
        in_specs=[
            pl.BlockSpec((tm, tc), lambda i, j: (i, j)),
            pl.BlockSpec((8, tc), lambda i, j: (jnp.maximum(i * nb - 1, 0), j)),
            pl.BlockSpec((8, tc), lambda i, j: (jnp.minimum((i + 1) * nb, s // 8 - 1), j)),
            pl.BlockSpec((3, tc), lambda i, j: (0, j)),
            pl.BlockSpec((1, tc), lambda i, j: (0, j)),
        ],
        out_specs=pl.BlockSpec((tm, tc), lambda i, j: (i, j)),
        out_shape=jax.ShapeDtypeStruct((s, n), F32),
        compiler_params=_params("parallel", "parallel"),
        name="hy_short_conv",
    )(u, u, u, conv_w, conv_b.reshape(1, n))


def _filter_mlp_kernel(z_ref, w1_ref, b1_ref, f_ref, w2_ref, b2_ref, w3_ref, b3_ref, o_ref):
    hp = lax.Precision.HIGHEST
    f = f_ref[...]
    h = jnp.sin(f * (jnp.dot(z_ref[...], w1_ref[...], precision=hp, preferred_element_type=F32) + b1_ref[...]))
    h = jnp.sin(f * (jnp.dot(h, w2_ref[...], precision=hp, preferred_element_type=F32) + b2_ref[...]))
    o_ref[...] = jnp.sin(f * (jnp.dot(h, w3_ref[...], precision=hp, preferred_element_type=F32) + b3_ref[...]))


def filter_mlp(zfeat, f_w1, f_b1, f_freq, f_w2, f_b2, f_w3, f_b3, tm=2048):
    n, e = zfeat.shape
    wd = f_w2.shape[0]
    row = lambda i: (i, 0)
    fixed = lambda i: (0, 0)
    vec = lambda v: v.reshape(1, wd)
    return pl.pallas_call(
        _filter_mlp_kernel,
        grid=(n // tm,),
        in_specs=[pl.BlockSpec((tm, e), row), pl.BlockSpec((e, wd), fixed), pl.BlockSpec((1, wd), fixed),
                  pl.BlockSpec((1, wd), fixed), pl.BlockSpec((wd, wd), fixed), pl.BlockSpec((1, wd), fixed),
                  pl.BlockSpec((wd, wd), fixed), pl.BlockSpec((1, wd), fixed)],
        out_specs=pl.BlockSpec((tm, wd), row),
        out_shape=jax.ShapeDtypeStruct((n, wd), F32),
        compiler_params=_params("parallel"),
        name="hy_filter_mlp",
    )(zfeat, f_w1, vec(f_b1), vec(f_freq), f_w2, vec(f_b2), f_w3, vec(f_b3))


def _kern_stage1_kernel(h_ref, wf_ref, wb_ref, dl_ref, w1_ref, kb_ref, nrm_ref, *, seq):
    g = pl.program_id(2)
    half = FFT_R // 2
    wf = wf_ref[...]
    wb = wb_ref[...]
    rate = dl_ref[...] * (-1.0 / (seq - 1))
    n1 = lax.broadcasted_iota(jnp.int32, (half, 1), 0)
    acc = jnp.zeros(nrm_ref.shape[1:], F32)
    for r in range(FFT_G):
        n2 = g * FFT_G + r
        hs = h_ref[:, r, :].astype(BF16)
        lag_f = FFT_R * n1 + n2
        lag_b = seq - FFT_R * n1 - n2
        kf = jnp.dot(hs[:half], wf, preferred_element_type=F32) * jnp.exp(lag_f.astype(F32) * rate)
        kb = jnp.dot(hs[half:], wb, preferred_element_type=F32) * jnp.exp(lag_b.astype(F32) * rate)
        kb = jnp.where(lag_b == seq, 0.0, kb)
        slab = jnp.concatenate([kf, kb], axis=0)
        acc += jnp.sum(jnp.abs(slab), axis=0, keepdims=True)
        kb_ref[0, :, r, :] = jnp.dot(w1_ref[r], slab.astype(BF16), preferred_element_type=F32)

    @pl.when(g == 0)
    def _():
        nrm_ref[0] = acc

    @pl.when(g > 0)
    def _():
        nrm_ref[0] += acc


def kern_stage1(h3, w4, deltas, w1_full, seq, tc=HY_TC):
    d = deltas.shape[0]
    jn = d // tc
    width = h3.shape[1]
    h3v = h3.reshape(FFT_R, FFT_R, width)
    kern = functools.partial(_kern_stage1_kernel, seq=seq)
    return pl.pallas_call(
        kern,
        grid=(jn, HY_ORDER, FFT_R // FFT_G),
        in_specs=[
            pl.BlockSpec((FFT_R, FFT_G, width), lambda j, o, g: (0, g, 0)),
            pl.BlockSpec((width, tc), lambda j, o, g: (0, o * jn + j)),
            pl.BlockSpec((width, tc), lambda j, o, g: (0, (HY_ORDER + o) * jn + j)),
            pl.BlockSpec((1, tc), lambda j, o, g: (0, j)),
            pl.BlockSpec((FFT_G, 2 * FFT_R, FFT_R), lambda j, o, g: (g, 0, 0)),
        ],
        out_specs=[
            pl.BlockSpec((1, 2 * FFT_R, FFT_G, tc), lambda j, o, g: (o, 0, g, j)),
            pl.BlockSpec((1, 1, tc), lambda j, o, g: (o, 0, j)),
        ],
        out_shape=[
            jax.ShapeDtypeStruct((HY_ORDER, 2 * FFT_R, FFT_R, d), F32),
            jax.ShapeDtypeStruct((HY_ORDER, 1, d), F32),
        ],
        compiler_params=_params("parallel", "parallel", "arbitrary"),
        name="hy_kern_stage1",
    )(h3v, w4, w4, deltas.reshape(1, d), w1_full)


def _kern_stage3_kernel(re_ref, im_ref, g_ref, o_ref):
    gm = g_ref[...]
    for r in range(FFT_G):
        b = jnp.concatenate([re_ref[0, 0, r], im_ref[0, 0, r]], axis=0).astype(BF16)
        o_ref[0, r] = jnp.dot(gm, b, preferred_element_type=F32).astype(o_ref.dtype)


def kern_stage3(kb, g_fwd, tc=HY_TC):
    d = kb.shape[-1]
    kbv = kb.reshape(HY_ORDER, 2, FFT_R, FFT_R, d)
    blk = (1, 1, FFT_G, FFT_R, tc)
    return pl.pallas_call(
        _kern_stage3_kernel,
        grid=(HY_ORDER, FFT_R // FFT_G, d // tc),
        in_specs=[
            pl.BlockSpec(blk, lambda o, kg, j: (o, 0, kg, 0, j)),
            pl.BlockSpec(blk, lambda o, kg, j: (o, 1, kg, 0, j)),
            pl.BlockSpec((2 * FFT_R, 2 * FFT_R), lambda o, kg, j: (0, 0)),
        ],
        out_specs=pl.BlockSpec((1, FFT_G, 2 * FFT_R, tc), lambda o, kg, j: (o, kg, 0, j)),
        out_shape=jax.ShapeDtypeStruct((HY_ORDER, FFT_R, 2 * FFT_R, d), BF16),
        compiler_params=_params("parallel", "parallel", "parallel"),
        name="hy_kern_stage3",
    )(kbv, kbv, g_fwd)


def _sig_stage1_kernel(z_ref, w1_ref, o_ref):
    for r in range(FFT_G):
        o_ref[:, r, :] = jnp.dot(w1_ref[r], z_ref[:, r, :].astype(BF16), preferred_element_type=F32)


def sig_stage1(z_src, col0, d, w1_half, tc=HY_TC):
    s = z_src.shape[0]
    zv = z_src.reshape(s // FFT_R, FFT_R, z_src.shape[1])
    return pl.pallas_call(
        _sig_stage1_kernel,
        grid=(FFT_R // FFT_G, d // tc),
        in_specs=[
            pl.BlockSpec((s // FFT_R, FFT_G, tc), lambda g, j: (0, g, col0 + j)),
            pl.BlockSpec((FFT_G, 2 * FFT_R, s // FFT_R), lambda g, j: (g, 0, 0)),
        ],
        out_specs=pl.BlockSpec((2 * FFT_R, FFT_G, tc), lambda g, j: (0, g, j)),
        out_shape=jax.ShapeDtypeStruct((2 * FFT_R, FFT_R, d), F32),
        compiler_params=_params("parallel", "parallel"),
        name="hy_sig_stage1",
    )(zv, w1_half)


def _sig_stage3_kernel(re_ref, im_ref, ks_ref, gf_ref, gi_ref, o_ref):
    gf = gf_ref[...]
    gi = gi_ref[...]
    for r in range(FFT_G):
        b = jnp.concatenate([re_ref[0, r], im_ref[0, r]], axis=0).astype(BF16)
        x = jnp.dot(gf, b, preferred_element_type=F32)
        ks = ks_ref[0, r].astype(F32)
        xr, xi = x[:FFT_R], x[FFT_R:]
        kr, ki = ks[:FFT_R], ks[FFT_R:]
        y = jnp.concatenate([xr * kr - xi * ki, xr * ki + xi * kr], axis=0).astype(BF16)
        o_ref[r] = jnp.dot(gi, y, preferred_element_type=F32)


def sig_stage3(zb, kspec, order, g_fwd, g_inv, tc=HY_TC):
    d = zb.shape[-1]
    zbv = zb.reshape(2, FFT_R, FFT_R, d)
    blk = (1, FFT_G, FFT_R, tc)
    mat = pl.BlockSpec((2 * FFT_R, 2 * FFT_R), lambda kg, j: (0, 0))
    return pl.pallas_call(
        _sig_stage3_kernel,
        grid=(FFT_R // FFT_G, d // tc),
        in_specs=[
            pl.BlockSpec(blk, lambda kg, j: (0, kg, 0, j)),
            pl.BlockSpec(blk, lambda kg, j: (1, kg, 0, j)),
            pl.BlockSpec((1, FFT_G, 2 * FFT_R, tc), lambda kg, j: (order, kg, 0, j)),
            mat, mat,
        ],
        out_specs=pl.BlockSpec((FFT_G, 2 * FFT_R, tc), lambda kg, j: (kg, 0, j)),
        out_shape=jax.ShapeDtypeStruct((FFT_R, 2 * FFT_R, d), F32),
        compiler_params=_params("parallel", "parallel"),
        name="hy_sig_stage3",
    )(zbv, zbv, kspec, g_fwd, g_inv)


def _sig_inverse1_kernel(yb_ref, v_ref, gate_ref, z_ref, nrm_ref, bias_ref, o_ref):
    inv = 1.0 / nrm_ref[0]
    bias = bias_ref[0]
    for r in range(FFT_G):
        b = jnp.concatenate([yb_ref[:, 0, r, :], yb_ref[:, 1, r, :]], axis=0).astype(BF16)
        conv = jnp.dot(v_ref[r], b, preferred_element_type=F32) * inv
        o_ref[:, r, :] = gate_ref[:, r, :] * (conv + z_ref[:, r, :] * bias)


def sig_inverse1(yb, v1_half, gate_src, gate_col0, z_src, z_col0, nrm, bias, order, tc=HY_TC):
    d = yb.shape[-1]
    s = gate_src.shape[0]
    rows = s // FFT_R
    ybv = yb.reshape(FFT_R, 2, FFT_R, d)
    view = lambda a: a.reshape(rows, FFT_R, a.shape[1])
    sig = (rows, FFT_G, tc)
    return pl.pallas_call(
        _sig_inverse1_kernel,
        grid=(FFT_R // FFT_G, d // tc),
        in_specs=[
            pl.BlockSpec((FFT_R, 2, FFT_G, tc), lambda g, j: (0, 0, g, j)),
            pl.BlockSpec((FFT_G, rows, 2 * FFT_R), lambda g, j: (g, 0, 0)),
            pl.BlockSpec(sig, lambda g, j: (0, g, gate_col0 + j)),
            pl.BlockSpec(sig, lambda g, j: (0, g, z_col0 + j)),
            pl.BlockSpec((1, 1, tc), lambda g, j: (order, 0, j)),
            pl.BlockSpec((1, 1, tc), lambda g, j: (order, 0, j)),
        ],
        out_specs=pl.BlockSpec(sig, lambda g, j: (0, g, j)),
        out_shape=jax.ShapeDtypeStruct((rows, FFT_R, d), F32),
        compiler_params=_params("parallel", "parallel"),
        name="hy_sig_inverse1",
    )(ybv, v1_half, view(gate_src), view(z_src), nrm, bias.reshape(HY_ORDER, 1, d)).reshape(s, d)


def _hyena_positions(seq):
    t = jnp.linspace(0.0, 1.0, seq, dtype=F32)[:, None]
    bands = (HY_EMB_DIM - 1) // 2
    w = 2.0 * math.pi * jnp.arange(seq, dtype=F32)[:, None] / seq
    f = jnp.linspace(1e-4, bands - 1, bands, dtype=F32)[None, :]
    z = jnp.concatenate([t, jnp.cos(f * w), -jnp.sin(f * w)], axis=-1)
    z = jnp.concatenate([z, z[:1], z[:0:-1]], axis=0)
    return jnp.pad(z, ((0, 0), (0, 128 - HY_EMB_DIM)))


def hyena_mixer(u, conv_w, conv_b, f_w1, f_b1, f_freq, f_w2, f_b2, f_w3, f_b3, f_w4, bias):
    s = u.shape[0]
    d = u.shape[1] // 3
    assert 2 * s == FFT_N and d % HY_TC == 0
    jn = d // HY_TC
    c = _dft_constants()
    u = short_conv(u, conv_w, conv_b)
    h3 = filter_mlp(_hyena_positions(s), jnp.pad(f_w1, ((0, 128 - HY_EMB_DIM), (0, 0))), f_b1, f_freq, f_w2, f_b2, f_w3, f_b3)
    deltas = jnp.abs(jnp.linspace(HY_MIN_DECAY, HY_MAX_DECAY, d, dtype=F32))
    kb, nrm = kern_stage1(h3, f_w4.astype(BF16), deltas, c["w1_full"], s)
    kspec = kern_stage3(kb, c["g_fwd"])
    z_src, z_col0 = u, 2 * jn
    for o in range(HY_ORDER):
        zb = sig_stage1(z_src, z_col0, d, c["w1_half"])
        yb = sig_stage3(zb, kspec, o, c["g_fwd"], c["g_inv"])
        z_src = sig_inverse1(yb, c["v1_half"], u, o * jn, z_src, z_col0, nrm, bias, o)
        z_col0 = 0
    return z_src


def hyena_layer(x, mix_norm, w_in, conv_w, conv_b, f_w1, f_b1, f_freq, f_w2, f_b2, f_w3, f_b3, f_w4, bias, w_out):
    u = norm_matmul(x, mix_norm, w_in.astype(BF16), F32)
    z = hyena_mixer(u, conv_w, conv_b, f_w1, f_b1, f_freq, f_w2, f_b2, f_w3, f_b3, f_w4, bias)
    return matmul_residual(z, w_out.astype(BF16), x)


def _rope_tables(seq, dim):
    inv = 1.0 / (ROPE_THETA ** (jnp.arange(0, dim, 2, dtype=F32) / dim))
    ang = jnp.arange(seq, dtype=F32)[:, None] * inv[None, :]
    return jnp.cos(ang), jnp.sin(ang)


def kernel(x, l0_mix_norm, l0_mla_w_in, l0_mla_q_norm, l0_mla_w_uq, l0_mla_kv_norm, l0_mla_w_ukv, l0_mla_w_o, l0_ffn_norm, l0_ffn_w_gate, l0_ffn_w_up, l0_ffn_w_down, l1_mix_norm, l1_hy_w_in, l1_hy_conv_w, l1_hy_conv_b, l1_hy_f_w1, l1_hy_f_b1, l1_hy_f_freq, l1_hy_f_w2, l1_hy_f_b2, l1_hy_f_w3, l1_hy_f_b3, l1_hy_f_w4, l1_hy_bias, l1_hy_w_out, l1_ffn_norm, l1_ffn_w_gate, l1_ffn_w_up, l1_ffn_w_down, l2_mix_norm, l2_da_w_qkv, l2_da_lq1, l2_da_lk1, l2_da_lq2, l2_da_lk2, l2_da_subln, l2_da_w_o, l2_ffn_norm, l2_ffn_w_gate, l2_ffn_w_up, l2_ffn_w_down, l3_mix_norm, l3_mla_w_in, l3_mla_q_norm, l3_mla_w_uq, l3_mla_kv_norm, l3_mla_w_ukv, l3_mla_w_o, l3_ffn_norm, l3_ffn_w_gate, l3_ffn_w_up, l3_ffn_w_down, final_norm):
    b, s, d = x.shape
    assert b == 1
    h = x.reshape(s, d)

    mc, ms = _rope_tables(s, MLA_ROPE)
    mla_tab = jnp.concatenate([mc, mc, ms, ms], axis=-1)
    mla = (mla_tab, mc.T, ms.T)
    dc, ds = _rope_tables(s, DIFF_HEAD_DIM)
    diff_rope = (jnp.concatenate([dc, dc], axis=-1), jnp.concatenate([-ds, ds], axis=-1))
    diff_rope_t = (dc.T, ds.T)

    def ffn(h, g, wg, wu, wd):
        return ffn_block(h, g, wg.astype(BF16), wu.astype(BF16), wd.astype(BF16))

    h = mla_layer(h, l0_mix_norm, l0_mla_w_in, l0_mla_q_norm, l0_mla_w_uq, l0_mla_kv_norm, l0_mla_w_ukv, l0_mla_w_o, *mla)
    h = ffn(h, l0_ffn_norm, l0_ffn_w_gate, l0_ffn_w_up, l0_ffn_w_down)
    h = hyena_layer(h, l1_mix_norm, l1_hy_w_in, l1_hy_conv_w, l1_hy_conv_b, l1_hy_f_w1, l1_hy_f_b1, l1_hy_f_freq, l1_hy_f_w2, l1_hy_f_b2, l1_hy_f_w3, l1_hy_f_b3, l1_hy_f_w4, l1_hy_bias, l1_hy_w_out)
    h = ffn(h, l1_ffn_norm, l1_ffn_w_gate, l1_ffn_w_up, l1_ffn_w_down)
    h = diff_layer(h, l2_mix_norm, l2_da_w_qkv, l2_da_lq1, l2_da_lk1, l2_da_lq2, l2_da_lk2, l2_da_subln, l2_da_w_o, diff_rope, diff_rope_t, 2)
    h = ffn(h, l2_ffn_norm, l2_ffn_w_gate, l2_ffn_w_up, l2_ffn_w_down)
    h = mla_layer(h, l3_mix_norm, l3_mla_w_in, l3_mla_q_norm, l3_mla_w_uq, l3_mla_kv_norm, l3_mla_w_ukv, l3_mla_w_o, *mla)
    h = ffn(h, l3_ffn_norm, l3_ffn_w_gate, l3_ffn_w_up, l3_ffn_w_down)
    return rmsnorm(h, final_norm).reshape(b, s, d)
```

```python
import functools
import math

import jax
import jax.numpy as jnp
import numpy as np
from jax import lax
from jax.experimental import pallas as pl
from jax.experimental.pallas import tpu as pltpu

F32 = jnp.float32
BF16 = jnp.bfloat16

D_MODEL = 2048
SEQ = 8192
RMS_EPS = 1e-6
ROPE_THETA = 10000.0
LOG2E = math.log2(math.e)

MLA_HEADS = 16
MLA_Q_LORA = 768
MLA_KV_LORA = 512
MLA_NOPE = 128
MLA_ROPE = 64
MLA_V = 128
MLA_QK = MLA_NOPE + MLA_ROPE
MLA_SCALE = MLA_QK ** -0.5
MLA_QK_PAD = 256

HY_ORDER = 2
HY_EMB_DIM = 33
HY_TARGET = 1e-2
HY_MAX_DECAY = math.log(HY_TARGET) / 0.3
HY_MIN_DECAY = math.log(HY_TARGET) / 1.5

DIFF_HEAD_DIM = 128
DIFF_HEADS = D_MODEL // (2 * DIFF_HEAD_DIM)
DIFF_QK = DIFF_HEADS * 2 * DIFF_HEAD_DIM
DIFF_SCALE = DIFF_HEAD_DIM ** -0.5
DIFF_EPS = 1e-5

D_FF = -(-8 * D_MODEL // (3 * 256)) * 256

VMEM_LIMIT = 56 * 1024 * 1024
NEG_BIG = -1e30


def _params(*sem):
    return pltpu.CompilerParams(dimension_semantics=sem, vmem_limit_bytes=VMEM_LIMIT)


def _rms(x, g, eps):
    return x * lax.rsqrt(jnp.mean(x * x, axis=-1, keepdims=True) + eps) * g


def _dot_nt(a, b):
    return lax.dot_general(a, b, (((1,), (1,)), ((), ())), preferred_element_type=F32)


def _rope_rows(y, c, s):
    h = c.shape[0]
    y1, y2 = y[:h], y[h:]
    return y1 * c - y2 * s, y2 * c + y1 * s


def _norm_matmul_kernel(x_ref, g_ref, w_ref, *rest, rope, scale):
    o_ref, xn_ref = rest[-2:]

    @pl.when(pl.program_id(1) == 0)
    def _():
        xn_ref[...] = _rms(x_ref[...], g_ref[...], RMS_EPS).astype(BF16)

    y = jnp.dot(xn_ref[...], w_ref[...], preferred_element_type=F32)
    if rope:
        c = rest[0][...] * scale
        s = rest[1][...] * scale
        for g in range(y.shape[1] // 128):
            t = y[:, g * 128:(g + 1) * 128]
            o_ref[:, g * 128:(g + 1) * 128] = (t * c + pltpu.roll(t, 64, axis=1) * s).astype(o_ref.dtype)
    else:
        o_ref[...] = (y * scale).astype(o_ref.dtype)


def norm_matmul(x, g, w, out_dtype, rope=None, scale=1.0, tm=1024, tn=1024):
    m, k = x.shape
    n = w.shape[1]
    in_specs = [
        pl.BlockSpec((tm, k), lambda i, j: (i, 0)),
        pl.BlockSpec((1, k), lambda i, j: (0, 0)),
        pl.BlockSpec((k, tn), lambda i, j: (0, j)),
    ]
    args = [x, g.reshape(1, k), w]
    if rope is not None:
        in_specs += [pl.BlockSpec((tm, 128), lambda i, j: (i, 0))] * 2
        args += list(rope)
    return pl.pallas_call(
        functools.partial(_norm_matmul_kernel, rope=rope is not None, scale=scale),
        grid=(m // tm, n // tn),
        in_specs=in_specs,
        out_specs=pl.BlockSpec((tm, tn), lambda i, j: (i, j)),
        out_shape=jax.ShapeDtypeStruct((m, n), out_dtype),
        scratch_shapes=[pltpu.VMEM((tm, k), BF16)],
        compiler_params=_params("parallel", "arbitrary"),
        name="norm_matmul",
    )(*args)


def _norm_matmul_t_kernel(x_ref, g_ref, wt_ref, *rest, rope, scale):
    o_ref, xn_ref = rest[-2:]

    @pl.when(pl.program_id(1) == 0)
    def _():
        xn_ref[...] = _rms(x_ref[...], g_ref[...], RMS_EPS).astype(BF16)

    y = _dot_nt(wt_ref[...], xn_ref[...]) * scale
    if rope:
        c = rest[0][...]
        s = rest[1][...]
        for g in range(y.shape[0] // 128):
            r1, r2 = _rope_rows(y[g * 128:(g + 1) * 128], c, s)
            o_ref[g * 128:g * 128 + 64, :] = r1.astype(o_ref.dtype)
            o_ref[g * 128 + 64:(g + 1) * 128, :] = r2.astype(o_ref.dtype)
    else:
        o_ref[...] = y.astype(o_ref.dtype)


def norm_matmul_t(x, g, wt, out_dtype, rope_t=None, scale=1.0, tm=1024, tn=1024):
    m, k = x.shape
    n = wt.shape[0]
    in_specs = [
        pl.BlockSpec((tm, k), lambda i, j: (i, 0)),
        pl.BlockSpec((1, k), lambda i, j: (0, 0)),
        pl.BlockSpec((tn, k), lambda i, j: (j, 0)),
    ]
    args = [x, g.reshape(1, k), wt]
    if rope_t is not None:
        in_specs += [pl.BlockSpec((64, tm), lambda i, j: (0, i))] * 2
        args += list(rope_t)
    return pl.pallas_call(
        functools.partial(_norm_matmul_t_kernel, rope=rope_t is not None, scale=scale),
        grid=(m // tm, n // tn),
        in_specs=in_specs,
        out_specs=pl.BlockSpec((tn, tm), lambda i, j: (j, i)),
        out_shape=jax.ShapeDtypeStruct((n, m), out_dtype),
        scratch_shapes=[pltpu.VMEM((tm, k), BF16)],
        compiler_params=_params("parallel", "arbitrary"),
        name="norm_matmul_t",
    )(*args)


def _matmul_res_kernel(a_ref, w_ref, r_ref, o_ref):
    o_ref[...] = r_ref[...] + jnp.dot(a_ref[...].astype(BF16), w_ref[...], preferred_element_type=F32)


def matmul_residual(a, w, res, tm=1024, tn=1024):
    m, k = a.shape
    n = w.shape[1]
    return pl.pallas_call(
        _matmul_res_kernel,
        grid=(m // tm, n // tn),
        in_specs=[
            pl.BlockSpec((tm, k), lambda i, j: (i, 0)),
            pl.BlockSpec((k, tn), lambda i, j: (0, j)),
            pl.BlockSpec((tm, tn), lambda i, j: (i, j)),
        ],
        out_specs=pl.BlockSpec((tm, tn), lambda i, j: (i, j)),
        out_shape=jax.ShapeDtypeStruct((m, n), F32),
        compiler_params=_params("parallel", "parallel"),
        name="matmul_residual",
    )(a, w, res)


def _ffn_kernel(x_ref, g_ref, wg_ref, wu_ref, wd_ref, *rest, out_norm):
    o_ref, xn_ref = rest[-2:]

    @pl.when(pl.program_id(1) == 0)
    def _():
        x = x_ref[...]
        xn_ref[...] = _rms(x, g_ref[...], RMS_EPS).astype(BF16)
        o_ref[...] = x

    xn = xn_ref[...]
    a = jnp.dot(xn, wg_ref[...], preferred_element_type=F32)
    b = jnp.dot(xn, wu_ref[...], preferred_element_type=F32)
    h = (a * jax.nn.sigmoid(a) * b).astype(BF16)
    o_ref[...] += jnp.dot(h, wd_ref[...], preferred_element_type=F32)

    if out_norm:
        @pl.when(pl.program_id(1) == pl.num_programs(1) - 1)
        def _():
            o_ref[...] = _rms(o_ref[...], rest[0][...], RMS_EPS)


def ffn_block(x, g, wg, wu, wd, out_norm_gain=None, tm=512, tf=512):
    m, d = x.shape
    f = wg.shape[1]
    vec = pl.BlockSpec((1, d), lambda i, j: (0, 0))
    in_specs = [
        pl.BlockSpec((tm, d), lambda i, j: (i, 0)),
        vec,
        pl.BlockSpec((d, tf), lambda i, j: (0, j)),
        pl.BlockSpec((d, tf), lambda i, j: (0, j)),
        pl.BlockSpec((tf, d), lambda i, j: (j, 0)),
    ]
    args = [x, g.reshape(1, d), wg, wu, wd]
    if out_norm_gain is not None:
        in_specs.append(vec)
        args.append(out_norm_gain.reshape(1, d))
    return pl.pallas_call(
        functools.partial(_ffn_kernel, out_norm=out_norm_gain is not None),
        grid=(m // tm, f // tf),
        in_specs=in_specs,
        out_specs=pl.BlockSpec((tm, d), lambda i, j: (i, 0)),
        out_shape=jax.ShapeDtypeStruct((m, d), F32),
        scratch_shapes=[pltpu.VMEM((tm, d), BF16)],
        compiler_params=_params("parallel", "arbitrary"),
        name="ffn_block",
    )(*args)


def _flash_t(qs, k_of_part, k_ref, vt_ref, sa_ref, sb_ref, acc_ref, *, tk):
    parts = len(qs)
    tp = qs[0].shape[1]
    nk = k_ref.shape[0] // tk
    assert nk % 2 == 0 and nk >= 4

    def chunk(c):
        return pl.ds(c * tk if isinstance(c, int) else pl.multiple_of(c * tk, tk), tk)

    def scores(c, s_ref):
        k = k_ref[chunk(c), :]
        mc = ()
        for i in range(parts):
            st = jnp.dot(k_of_part(k, i), qs[i], preferred_element_type=F32)
            s_ref[i] = st
            mc += (jnp.max(st, axis=0, keepdims=True),)
        return mc

    def softmax_pv(c, s_ref, mc, state):
        vt = vt_ref[:, chunk(c)]
        new = ()
        for i in range(parts):
            m, l = state[2 * i:2 * i + 2]
            m_new = jnp.maximum(m, mc[i])
            alpha = jnp.exp2(m - m_new)
            p = jnp.exp2(s_ref[i] - m_new)
            l = alpha * l + jnp.sum(p, axis=0, keepdims=True)
            acc_ref[i] = alpha * acc_ref[i] + jnp.dot(vt, p.astype(BF16), preferred_element_type=F32)
            new += (m_new, l)
        return new

    def body(c2, carry):
        mc_a, state = carry[:parts], carry[parts:]
        c = 2 * c2
        mc_b = scores(c + 1, sb_ref)
        state = softmax_pv(c, sa_ref, mc_a, state)
        mc_a = scores(c + 2, sa_ref)
        state = softmax_pv(c + 1, sb_ref, mc_b, state)
        return mc_a + state

    acc_ref[...] = jnp.zeros(acc_ref.shape, F32)
    state0 = (jnp.full((1, tp), NEG_BIG, F32), jnp.zeros((1, tp), F32)) * parts
    carry = lax.fori_loop(0, nk // 2 - 1, body, scores(0, sa_ref) + state0)
    mc_a, state = carry[:parts], carry[parts:]
    mc_b = scores(nk - 1, sb_ref)
    state = softmax_pv(nk - 2, sa_ref, mc_a, state)
    state = softmax_pv(nk - 1, sb_ref, mc_b, state)
    return state[1::2]


def _rope_pair(t, tab):
    t = t * tab
    return t + pltpu.roll(t, 64, axis=1)


def _mla_in_kernel(x_ref, g_ref, w_ref, qn_ref, kvn_ref, tab_ref, cq_ref, ckv_ref, kr_ref):
    xn = _rms(x_ref[...], g_ref[...], RMS_EPS).astype(BF16)
    h = jnp.dot(xn, w_ref[...], preferred_element_type=F32)
    cq_ref[...] = _rms(h[:, :MLA_Q_LORA], qn_ref[...], RMS_EPS).astype(BF16)
    ckv_ref[...] = _rms(h[:, MLA_Q_LORA:MLA_Q_LORA + MLA_KV_LORA], kvn_ref[...], RMS_EPS).astype(BF16)
    kr = _rope_pair(h[:, MLA_Q_LORA + MLA_KV_LORA:], tab_ref[...])
    lane = lax.broadcasted_iota(jnp.int32, kr.shape, 1)
    kr_ref[...] = jnp.where(lane < MLA_ROPE, kr, 0.0).astype(BF16)


def mla_in(x, g, w_in_ext, q_norm, kv_norm, tab, tm=512):
    m, d = x.shape
    n = w_in_ext.shape[1]
    row = lambda i: (i, 0)
    fixed = lambda i: (0, 0)
    return pl.pallas_call(
        _mla_in_kernel,
        grid=(m // tm,),
        in_specs=[
            pl.BlockSpec((tm, d), row),
            pl.BlockSpec((1, d), fixed),
            pl.BlockSpec((d, n), fixed),
            pl.BlockSpec((1, MLA_Q_LORA), fixed),
            pl.BlockSpec((1, MLA_KV_LORA), fixed),
            pl.BlockSpec((tm, 128), row),
        ],
        out_specs=[
            pl.BlockSpec((tm, MLA_Q_LORA), row),
            pl.BlockSpec((tm, MLA_KV_LORA), row),
            pl.BlockSpec((tm, 128), row),
        ],
        out_shape=[
            jax.ShapeDtypeStruct((m, MLA_Q_LORA), BF16),
            jax.ShapeDtypeStruct((m, MLA_KV_LORA), BF16),
            jax.ShapeDtypeStruct((m, 128), BF16),
        ],
        compiler_params=_params("parallel"),
        name="mla_in",
    )(x, g.reshape(1, d), w_in_ext, q_norm.reshape(1, -1), kv_norm.reshape(1, -1), tab)


def _mla_qt_kernel(c_ref, wt_ref, cos_ref, sin_ref, q_ref, *, heads):
    y = _dot_nt(wt_ref[...], c_ref[...]) * (MLA_SCALE * LOG2E)
    c = cos_ref[...]
    s = sin_ref[...]
    zeros = jnp.zeros((MLA_QK_PAD - MLA_QK, y.shape[1]), BF16)
    for h in range(heads):
        src = h * MLA_QK
        dst = h * MLA_QK_PAD
        q_ref[dst:dst + MLA_NOPE, :] = y[src:src + MLA_NOPE].astype(BF16)
        r1, r2 = _rope_rows(y[src + MLA_NOPE:src + MLA_QK], c, s)
        q_ref[dst + MLA_NOPE:dst + MLA_NOPE + 32, :] = r1.astype(BF16)
        q_ref[dst + MLA_NOPE + 32:dst + MLA_QK, :] = r2.astype(BF16)
        q_ref[dst + MLA_QK:dst + MLA_QK_PAD, :] = zeros


def mla_qt(cq, w_uq_t, cos_t, sin_t, tm=1024, heads_per_step=4):
    m, k = cq.shape
    tn_in = heads_per_step * MLA_QK
    tn_out = heads_per_step * MLA_QK_PAD
    return pl.pallas_call(
        functools.partial(_mla_qt_kernel, heads=heads_per_step),
        grid=(m // tm, MLA_HEADS // heads_per_step),
        in_specs=[
            pl.BlockSpec((tm, k), lambda i, j: (i, 0)),
            pl.BlockSpec((tn_in, k), lambda i, j: (j, 0)),
            pl.BlockSpec((32, tm), lambda i, j: (0, i)),
            pl.BlockSpec((32, tm), lambda i, j: (0, i)),
        ],
        out_specs=pl.BlockSpec((tn_out, tm), lambda i, j: (j, i)),
        out_shape=jax.ShapeDtypeStruct((MLA_HEADS * MLA_QK_PAD, m), BF16),
        compiler_params=_params("parallel", "parallel"),
        name="mla_qt",
    )(cq, w_uq_t, cos_t, sin_t)


def _mla_kv_kernel(c_ref, wk_ref, wvt_ref, kr_ref, k_ref, vt_ref, *, heads):
    c = c_ref[...]
    yk = jnp.dot(c, wk_ref[...], preferred_element_type=F32)
    kr = kr_ref[...]
    for h in range(heads):
        k_ref[:, h * 256:h * 256 + 128] = yk[:, h * 128:(h + 1) * 128].astype(BF16)
        k_ref[:, h * 256 + 128:h * 256 + 256] = kr
    vt_ref[...] = _dot_nt(wvt_ref[...], c).astype(BF16)


def mla_kv(ckv, w_k, w_v_t, kr, tm=1024, heads_per_step=4):
    m, k = ckv.shape
    hp = heads_per_step
    return pl.pallas_call(
        functools.partial(_mla_kv_kernel, heads=hp),
        grid=(m // tm, MLA_HEADS // hp),
        in_specs=[
            pl.BlockSpec((tm, k), lambda i, j: (i, 0)),
            pl.BlockSpec((k, hp * MLA_NOPE), lambda i, j: (0, j)),
            pl.BlockSpec((hp * MLA_V, k), lambda i, j: (j, 0)),
            pl.BlockSpec((tm, 128), lambda i, j: (i, 0)),
        ],
        out_specs=[
            pl.BlockSpec((tm, hp * MLA_QK_PAD), lambda i, j: (i, j)),
            pl.BlockSpec((hp * MLA_V, tm), lambda i, j: (j, i)),
        ],
        out_shape=[
            jax.ShapeDtypeStruct((m, MLA_HEADS * MLA_QK_PAD), BF16),
            jax.ShapeDtypeStruct((MLA_HEADS * MLA_V, m), BF16),
        ],
        compiler_params=_params("parallel", "parallel"),
        name="mla_kv",
    )(ckv, w_k, w_v_t, kr)


def _mla_attn_kernel(qt_ref, k_ref, vt_ref, o_ref, sa_ref, sb_ref, acc_ref, *, tk):
    parts, _, tp = sa_ref.shape
    qs = [qt_ref[:, i * tp:(i + 1) * tp] for i in range(parts)]
    ls = _flash_t(qs, lambda k, i: k, k_ref, vt_ref, sa_ref, sb_ref, acc_ref, tk=tk)
    for i in range(parts):
        o_ref[i * tp:(i + 1) * tp, :] = (acc_ref[i] / ls[i]).T.astype(o_ref.dtype)


def mla_attention(qt, k, vt, tq=512, tk=512, parts=2):
    s = k.shape[0]
    return pl.pallas_call(
        functools.partial(_mla_attn_kernel, tk=tk),
        grid=(MLA_HEADS, s // tq),
        in_specs=[
            pl.BlockSpec((MLA_QK_PAD, tq), lambda h, i: (h, i)),
            pl.BlockSpec((s, MLA_QK_PAD), lambda h, i: (0, h)),
            pl.BlockSpec((MLA_V, s), lambda h, i: (h, 0)),
        ],
        out_specs=pl.BlockSpec((tq, MLA_V), lambda h, i: (i, h)),
        out_shape=jax.ShapeDtypeStruct((s, MLA_HEADS * MLA_V), BF16),
        scratch_shapes=[
            pltpu.VMEM((parts, tk, tq // parts), F32),
            pltpu.VMEM((parts, tk, tq // parts), F32),
            pltpu.VMEM((parts, MLA_V, tq // parts), F32),
        ],
        compiler_params=_params("parallel", "parallel"),
        name="mla_attention",
    )(qt, k, vt)


def _rot_cols(w):
    half = w.shape[-1] // 2
    return jnp.concatenate([-w[..., half:], w[..., :half]], axis=-1)


def mla_layer(x, mix_norm, w_in, q_norm, w_uq, kv_norm, w_ukv, w_o, tab, cos_t, sin_t):
    w_kr = w_in[:, MLA_Q_LORA + MLA_KV_LORA:]
    w_in_ext = jnp.concatenate([w_in, _rot_cols(w_kr)], axis=1).astype(BF16)
    w_uq_t = w_uq.astype(BF16).T
    wkv = w_ukv.astype(BF16).reshape(MLA_KV_LORA, MLA_HEADS, MLA_NOPE + MLA_V)
    w_k = wkv[..., :MLA_NOPE].reshape(MLA_KV_LORA, -1)
    w_v_t = wkv[..., MLA_NOPE:].reshape(MLA_KV_LORA, -1).T
    cq, ckv, kr = mla_in(x, mix_norm, w_in_ext, q_norm, kv_norm, tab)
    qt = mla_qt(cq, w_uq_t, cos_t, sin_t)
    k, vt = mla_kv(ckv, w_k, w_v_t, kr)
    o = mla_attention(qt, k, vt)
    return matmul_residual(o, w_o.astype(BF16), x)


def _diff_attn_kernel(lam_ref, qt_ref, k_ref, vt_ref, g_ref, o_ref, sa_ref, sb_ref, acc_ref, *, tk, lam_init):
    d = DIFF_HEAD_DIM
    qs = [qt_ref[:d, :], qt_ref[d:, :]]
    l0, l1 = _flash_t(qs, lambda k, i: k[:, i * d:(i + 1) * d], k_ref, vt_ref, sa_ref, sb_ref, acc_ref, tk=tk)
    ot = acc_ref[0] / l0 - lam_ref[0, 0] * (acc_ref[1] / l1)
    o = _rms(ot.T, g_ref[...], DIFF_EPS) * (1.0 - lam_init)
    o_ref[...] = o.astype(o_ref.dtype)


def diff_attention(qt, k, vt, lam, subln, lam_init, tq=256, tk=512):
    s = k.shape[0]
    hd = 2 * DIFF_HEAD_DIM
    return pl.pallas_call(
        functools.partial(_diff_attn_kernel, tk=tk, lam_init=lam_init),
        grid=(DIFF_HEADS, s // tq),
        in_specs=[
            pl.BlockSpec(memory_space=pltpu.SMEM),
            pl.BlockSpec((hd, tq), lambda h, i: (h, i)),
            pl.BlockSpec((s, hd), lambda h, i: (0, h)),
            pl.BlockSpec((hd, s), lambda h, i: (h, 0)),
            pl.BlockSpec((1, hd), lambda h, i: (0, 0)),
        ],
        out_specs=pl.BlockSpec((tq, hd), lambda h, i: (i, h)),
        out_shape=jax.ShapeDtypeStruct((s, DIFF_HEADS * hd), BF16),
        scratch_shapes=[
            pltpu.VMEM((2, tk, tq), F32),
            pltpu.VMEM((2, tk, tq), F32),
            pltpu.VMEM((2, hd, tq), F32),
        ],
        compiler_params=_params("parallel", "parallel"),
        name="diff_attention",
    )(lam, qt, k, vt, subln.reshape(1, hd))


def _lambda_kernel(lq1_ref, lk1_ref, lq2_ref, lk2_ref, o_ref, *, lam_init):
    a = jnp.sum(lq1_ref[...] * lk1_ref[...], axis=-1, keepdims=True)
    b = jnp.sum(lq2_ref[...] * lk2_ref[...], axis=-1, keepdims=True)
    o_ref[...] = jnp.exp(a) - jnp.exp(b) + lam_init


def diff_lambda(lq1, lk1, lq2, lk2, lam_init):
    args = [v.reshape(1, -1) for v in (lq1, lk1, lq2, lk2)]
    return pl.pallas_call(
        functools.partial(_lambda_kernel, lam_init=lam_init),
        out_shape=jax.ShapeDtypeStruct((1, 1), F32),
        name="diff_lambda",
    )(*args)


def diff_layer(x, mix_norm, w_qkv, lq1, lk1, lq2, lk2, subln, w_o, rope, rope_t, layer_idx):
    lam_init = 0.8 - 0.6 * math.exp(-0.3 * layer_idx)
    w = w_qkv.astype(BF16)
    qt = norm_matmul_t(x, mix_norm, w[:, :DIFF_QK].T, BF16, rope_t=rope_t, scale=DIFF_SCALE * LOG2E)
    k = norm_matmul(x, mix_norm, w[:, DIFF_QK:2 * DIFF_QK], BF16, rope=rope)
    vt = norm_matmul_t(x, mix_norm, w[:, 2 * DIFF_QK:].T, BF16)
    lam = diff_lambda(lq1, lk1, lq2, lk2, lam_init)
    o = diff_attention(qt, k, vt, lam, subln, lam_init)
    return matmul_residual(o, w_o.astype(BF16), x)


FFT_R = 128
FFT_N = FFT_R * FFT_R
FFT_G = 8
HY_TC = 512


def _dft_constants():
    r = np.arange(FFT_R)
    n2, k1, n1 = r[:, None, None], r[None, :, None], r[None, None, :]
    th = 2.0 * np.pi * ((k1 * (FFT_R * n1 + n2)) % FFT_N) / FFT_N
    w1 = np.concatenate([np.cos(th), -np.sin(th)], axis=1)
    tht = np.transpose(th, (0, 2, 1))
    v1 = np.concatenate([np.cos(tht), -np.sin(tht)], axis=2) / FFT_N
    ph = 2.0 * np.pi * ((r[:, None] * r[None, :]) % FFT_R) / FFT_R
    c, s = np.cos(ph), np.sin(ph)
    g_fwd = np.block([[c, s], [-s, c]])
    g_inv = np.block([[c, -s], [s, c]])
    as_bf16 = lambda a: a.astype(np.float32).astype(BF16)
    half = FFT_R // 2
    return dict(w1_full=as_bf16(w1), w1_half=as_bf16(w1[:, :, :half]), v1_half=as_bf16(v1[:, :half, :]),
                g_fwd=as_bf16(g_fwd), g_inv=as_bf16(g_inv))


def _pack_pair(x):
    h = x.shape[0] // 2
    hi = lax.bitcast_convert_type(x[:h].astype(BF16).astype(F32), jnp.uint32)
    lo = lax.bitcast_convert_type(x[h:].astype(BF16).astype(F32), jnp.uint32)
    return hi | (lo >> 16)


def _unpack_pair(p):
    hi = lax.bitcast_convert_type(p & jnp.uint32(0xFFFF0000), F32)
    lo = lax.bitcast_convert_type(p << 16, F32)
    return jnp.concatenate([hi, lo], axis=0).astype(BF16)


def _short_conv_kernel(u_ref, prev_ref, next_ref, w_ref, b_ref, o_ref):
    i = pl.program_id(0)
    x = u_ref[...]
    tm = x.shape[0]
    row = lax.broadcasted_iota(jnp.int32, x.shape, 0)
    before = jnp.where(i > 0, prev_ref[7:8, :], 0.0)
    after = jnp.where(i < pl.num_programs(0) - 1, next_ref[0:1, :], 0.0)
    up = jnp.where(row == 0, before, pltpu.roll(x, 1, axis=0))
    dn = jnp.where(row == tm - 1, after, pltpu.roll(x, tm - 1, axis=0))
    o_ref[...] = up * w_ref[0:1, :] + x * w_ref[1:2, :] + dn * w_ref[2:3, :] + b_ref[...]


def short_conv(u, conv_w, conv_b, tm=1024, tc=512):
    s, n = u.shape
    nb = tm // 8
    return pl.pallas_call(
        _short_conv_kernel,
        grid=(s // tm, n // tc),
        in_specs=[
            pl.BlockSpec((tm, tc), lambda i, j: (i, j)),
            pl.BlockSpec((8, tc), lambda i, j: (jnp.maximum(i * nb - 1, 0), j)),
            pl.BlockSpec((8, tc), lambda i, j: (jnp.minimum((i + 1) * nb, s // 8 - 1), j)),
            pl.BlockSpec((3, tc), lambda i, j: (0, j)),
            pl.BlockSpec((1, tc), lambda i, j: (0, j)),
        ],
        out_specs=pl.BlockSpec((tm, tc), lambda i, j: (i, j)),
        out_shape=jax.ShapeDtypeStruct((s, n), F32),
        compiler_params=_params("parallel", "parallel"),
        name="hy_short_conv",
    )(u, u, u, conv_w, conv_b.reshape(1, n))


def _filter_mlp_kernel(z_ref, w1_ref, b1_ref, f_ref, w2_ref, b2_ref, w3_ref, b3_ref, o_ref):
    hp = lax.Precision.HIGHEST
    f = f_ref[...]
    h = jnp.sin(f * (jnp.dot(z_ref[...], w1_ref[...], precision=hp, preferred_element_type=F32) + b1_ref[...]))
    h = jnp.sin(f * (jnp.dot(h, w2_ref[...], precision=hp, preferred_element_type=F32) + b2_ref[...]))
    o_ref[...] = jnp.sin(f * (jnp.dot(h, w3_ref[...], precision=hp, preferred_element_type=F32) + b3_ref[...]))


def filter_mlp(zfeat, f_w1, f_b1, f_freq, f_w2, f_b2, f_w3, f_b3, tm=2048):
    n, e = zfeat.shape
    wd = f_w2.shape[0]
    row = lambda i: (i, 0)
    fixed = lambda i: (0, 0)
    vec = lambda v: v.reshape(1, wd)
    return pl.pallas_call(
        _filter_mlp_kernel,
        grid=(n // tm,),
        in_specs=[pl.BlockSpec((tm, e), row), pl.BlockSpec((e, wd), fixed), pl.BlockSpec((1, wd), fixed),
                  pl.BlockSpec((1, wd), fixed), pl.BlockSpec((wd, wd), fixed), pl.BlockSpec((1, wd), fixed),
                  pl.BlockSpec((wd, wd), fixed), pl.BlockSpec((1, wd), fixed)],
        out_specs=pl.BlockSpec((tm, wd), row),
        out_shape=jax.ShapeDtypeStruct((n, wd), F32),
        compiler_params=_params("parallel"),
        name="hy_filter_mlp",
    )(zfeat, f_w1, vec(f_b1), vec(f_freq), f_w2, vec(f_b2), f_w3, vec(f_b3))


def _kern_stage1_kernel(h_ref, wf_ref, wb_ref, dl_ref, w1_ref, kb_ref, nrm_ref, *, seq):
    g = pl.program_id(2)
    half = FFT_R // 2
    wf = wf_ref[...]
    wb = wb_ref[...]
    rate = dl_ref[...] * (-1.0 / (seq - 1))
    n1 = lax.broadcasted_iota(jnp.int32, (half, 1), 0)
    acc = jnp.zeros(nrm_ref.shape[1:], F32)
    for r in range(FFT_G):
        n2 = g * FFT_G + r
        hs = h_ref[:, r, :].astype(BF16)
        lag_f = FFT_R * n1 + n2
        lag_b = seq - FFT_R * n1 - n2
        kf = jnp.dot(hs[:half], wf, preferred_element_type=F32) * jnp.exp(lag_f.astype(F32) * rate)
        kb = jnp.dot(hs[half:], wb, preferred_element_type=F32) * jnp.exp(lag_b.astype(F32) * rate)
        kb = jnp.where(lag_b == seq, 0.0, kb)
        slab = jnp.concatenate([kf, kb], axis=0)
        acc += jnp.sum(jnp.abs(slab), axis=0, keepdims=True)
        kb_ref[0, :, r, :] = _pack_pair(jnp.dot(w1_ref[r], slab.astype(BF16), preferred_element_type=F32))

    @pl.when(g == 0)
    def _():
        nrm_ref[0] = acc

    @pl.when(g > 0)
    def _():
        nrm_ref[0] += acc


def kern_stage1(h3, w4, deltas, w1_full, seq, tc=HY_TC):
    d = deltas.shape[0]
    jn = d // tc
    width = h3.shape[1]
    h3v = h3.reshape(FFT_R, FFT_R, width)
    kern = functools.partial(_kern_stage1_kernel, seq=seq)
    return pl.pallas_call(
        kern,
        grid=(jn, HY_ORDER, FFT_R // FFT_G),
        in_specs=[
            pl.BlockSpec((FFT_R, FFT_G, width), lambda j, o, g: (0, g, 0)),
            pl.BlockSpec((width, tc), lambda j, o, g: (0, o * jn + j)),
            pl.BlockSpec((width, tc), lambda j, o, g: (0, (HY_ORDER + o) * jn + j)),
            pl.BlockSpec((1, tc), lambda j, o, g: (0, j)),
            pl.BlockSpec((FFT_G, 2 * FFT_R, FFT_R), lambda j, o, g: (g, 0, 0)),
        ],
        out_specs=[
            pl.BlockSpec((1, FFT_R, FFT_G, tc), lambda j, o, g: (o, 0, g, j)),
            pl.BlockSpec((1, 1, tc), lambda j, o, g: (o, 0, j)),
        ],
        out_shape=[
            jax.ShapeDtypeStruct((HY_ORDER, FFT_R, FFT_R, d), jnp.uint32),
            jax.ShapeDtypeStruct((HY_ORDER, 1, d), F32),
        ],
        compiler_params=_params("parallel", "parallel", "arbitrary"),
        name="hy_kern_stage1",
    )(h3v, w4, w4, deltas.reshape(1, d), w1_full)


def _kern_stage3_kernel(kb_ref, g_ref, o_ref):
    gm = g_ref[...]
    for r in range(FFT_G):
        o_ref[0, r] = jnp.dot(gm, _unpack_pair(kb_ref[0, r]), preferred_element_type=F32).astype(o_ref.dtype)


def kern_stage3(kb, g_fwd, tc=HY_TC):
    d = kb.shape[-1]
    return pl.pallas_call(
        _kern_stage3_kernel,
        grid=(HY_ORDER, FFT_R // FFT_G, d // tc),
        in_specs=[
            pl.BlockSpec((1, FFT_G, FFT_R, tc), lambda o, kg, j: (o, kg, 0, j)),
            pl.BlockSpec((2 * FFT_R, 2 * FFT_R), lambda o, kg, j: (0, 0)),
        ],
        out_specs=pl.BlockSpec((1, FFT_G, 2 * FFT_R, tc), lambda o, kg, j: (o, kg, 0, j)),
        out_shape=jax.ShapeDtypeStruct((HY_ORDER, FFT_R, 2 * FFT_R, d), BF16),
        compiler_params=_params("parallel", "parallel", "parallel"),
        name="hy_kern_stage3",
    )(kb, g_fwd)


def _sig_stage1_kernel(z_ref, w1_ref, o_ref):
    for r in range(FFT_G):
        o_ref[:, r, :] = _pack_pair(jnp.dot(w1_ref[r], z_ref[:, r, :].astype(BF16), preferred_element_type=F32))


def sig_stage1(z_src, col0, d, w1_half, tc=HY_TC):
    s = z_src.shape[0]
    zv = z_src.reshape(s // FFT_R, FFT_R, z_src.shape[1])
    return pl.pallas_call(
        _sig_stage1_kernel,
        grid=(FFT_R // FFT_G, d // tc),
        in_specs=[
            pl.BlockSpec((s // FFT_R, FFT_G, tc), lambda g, j: (0, g, col0 + j)),
            pl.BlockSpec((FFT_G, 2 * FFT_R, s // FFT_R), lambda g, j: (g, 0, 0)),
        ],
        out_specs=pl.BlockSpec((FFT_R, FFT_G, tc), lambda g, j: (0, g, j)),
        out_shape=jax.ShapeDtypeStruct((FFT_R, FFT_R, d), jnp.uint32),
        compiler_params=_params("parallel", "parallel"),
        name="hy_sig_stage1",
    )(zv, w1_half)


def _sig_stage3_kernel(zb_ref, ks_ref, gf_ref, gi_ref, o_ref):
    gf = gf_ref[...]
    gi = gi_ref[...]
    for r in range(FFT_G):
        x = jnp.dot(gf, _unpack_pair(zb_ref[r]), preferred_element_type=F32)
        ks = ks_ref[0, r].astype(F32)
        xr, xi = x[:FFT_R], x[FFT_R:]
        kr, ki = ks[:FFT_R], ks[FFT_R:]
        y = jnp.concatenate([xr * kr - xi * ki, xr * ki + xi * kr], axis=0).astype(BF16)
        o_ref[r] = _pack_pair(jnp.dot(gi, y, preferred_element_type=F32))


def sig_stage3(zb, kspec, order, g_fwd, g_inv, tc=HY_TC):
    d = zb.shape[-1]
    blk = (FFT_G, FFT_R, tc)
    mat = pl.BlockSpec((2 * FFT_R, 2 * FFT_R), lambda kg, j: (0, 0))
    return pl.pallas_call(
        _sig_stage3_kernel,
        grid=(FFT_R // FFT_G, d // tc),
        in_specs=[
            pl.BlockSpec(blk, lambda kg, j: (kg, 0, j)),
            pl.BlockSpec((1, FFT_G, 2 * FFT_R, tc), lambda kg, j: (order, kg, 0, j)),
            mat, mat,
        ],
        out_specs=pl.BlockSpec(blk, lambda kg, j: (kg, 0, j)),
        out_shape=jax.ShapeDtypeStruct((FFT_R, FFT_R, d), jnp.uint32),
        compiler_params=_params("parallel", "parallel"),
        name="hy_sig_stage3",
    )(zb, kspec, g_fwd, g_inv)


def _sig_inverse1_kernel(yb_ref, v_ref, gate_ref, z_ref, nrm_ref, bias_ref, o_ref):
    inv = 1.0 / nrm_ref[0]
    bias = bias_ref[0]
    for r in range(FFT_G):
        conv = jnp.dot(v_ref[r], _unpack_pair(yb_ref[:, r, :]), preferred_element_type=F32) * inv
        o_ref[:, r, :] = gate_ref[:, r, :] * (conv + z_ref[:, r, :] * bias)


def sig_inverse1(yb, v1_half, gate_src, gate_col0, z_src, z_col0, nrm, bias, order, tc=HY_TC):
    d = yb.shape[-1]
    s = gate_src.shape[0]
    rows = s // FFT_R
    view = lambda a: a.reshape(rows, FFT_R, a.shape[1])
    sig = (rows, FFT_G, tc)
    return pl.pallas_call(
        _sig_inverse1_kernel,
        grid=(FFT_R // FFT_G, d // tc),
        in_specs=[
            pl.BlockSpec((FFT_R, FFT_G, tc), lambda g, j: (0, g, j)),
            pl.BlockSpec((FFT_G, rows, 2 * FFT_R), lambda g, j: (g, 0, 0)),
            pl.BlockSpec(sig, lambda g, j: (0, g, gate_col0 + j)),
            pl.BlockSpec(sig, lambda g, j: (0, g, z_col0 + j)),
            pl.BlockSpec((1, 1, tc), lambda g, j: (order, 0, j)),
            pl.BlockSpec((1, 1, tc), lambda g, j: (order, 0, j)),
        ],
        out_specs=pl.BlockSpec(sig, lambda g, j: (0, g, j)),
        out_shape=jax.ShapeDtypeStruct((rows, FFT_R, d), F32),
        compiler_params=_params("parallel", "parallel"),
        name="hy_sig_inverse1",
    )(yb, v1_half, view(gate_src), view(z_src), nrm, bias.reshape(HY_ORDER, 1, d)).reshape(s, d)


def _hyena_positions(seq):
    t = jnp.linspace(0.0, 1.0, seq, dtype=F32)[:, None]
    bands = (HY_EMB_DIM - 1) // 2
    w = 2.0 * math.pi * jnp.arange(seq, dtype=F32)[:, None] / seq
    f = jnp.linspace(1e-4, bands - 1, bands, dtype=F32)[None, :]
    z = jnp.concatenate([t, jnp.cos(f * w), -jnp.sin(f * w)], axis=-1)
    z = jnp.concatenate([z, z[:1], z[:0:-1]], axis=0)
    return jnp.pad(z, ((0, 0), (0, 128 - HY_EMB_DIM)))


def hyena_mixer(u, conv_w, conv_b, f_w1, f_b1, f_freq, f_w2, f_b2, f_w3, f_b3, f_w4, bias):
    s = u.shape[0]
    d = u.shape[1] // 3
    assert 2 * s == FFT_N and d % HY_TC == 0
    jn = d // HY_TC
    c = _dft_constants()
    u = short_conv(u, conv_w, conv_b)
    h3 = filter_mlp(_hyena_positions(s), jnp.pad(f_w1, ((0, 128 - HY_EMB_DIM), (0, 0))), f_b1, f_freq, f_w2, f_b2, f_w3, f_b3)
    deltas = jnp.abs(jnp.linspace(HY_MIN_DECAY, HY_MAX_DECAY, d, dtype=F32))
    kb, nrm = kern_stage1(h3, f_w4.astype(BF16), deltas, c["w1_full"], s)
    kspec = kern_stage3(kb, c["g_fwd"])
    z_src, z_col0 = u, 2 * jn
    for o in range(HY_ORDER):
        zb = sig_stage1(z_src, z_col0, d, c["w1_half"])
        yb = sig_stage3(zb, kspec, o, c["g_fwd"], c["g_inv"])
        z_src = sig_inverse1(yb, c["v1_half"], u, o * jn, z_src, z_col0, nrm, bias, o)
        z_col0 = 0
    return z_src


def hyena_layer(x, mix_norm, w_in, conv_w, conv_b, f_w1, f_b1, f_freq, f_w2, f_b2, f_w3, f_b3, f_w4, bias, w_out):
    u = norm_matmul(x, mix_norm, w_in.astype(BF16), F32)
    z = hyena_mixer(u, conv_w, conv_b, f_w1, f_b1, f_freq, f_w2, f_b2, f_w3, f_b3, f_w4, bias)
    return matmul_residual(z, w_out.astype(BF16), x)


def _rope_tables(seq, dim):
    inv = 1.0 / (ROPE_THETA ** (jnp.arange(0, dim, 2, dtype=F32) / dim))
    ang = jnp.arange(seq, dtype=F32)[:, None] * inv[None, :]
    return jnp.cos(ang), jnp.sin(ang)


def kernel(x, l0_mix_norm, l0_mla_w_in, l0_mla_q_norm, l0_mla_w_uq, l0_mla_kv_norm, l0_mla_w_ukv, l0_mla_w_o, l0_ffn_norm, l0_ffn_w_gate, l0_ffn_w_up, l0_ffn_w_down, l1_mix_norm, l1_hy_w_in, l1_hy_conv_w, l1_hy_conv_b, l1_hy_f_w1, l1_hy_f_b1, l1_hy_f_freq, l1_hy_f_w2, l1_hy_f_b2, l1_hy_f_w3, l1_hy_f_b3, l1_hy_f_w4, l1_hy_bias, l1_hy_w_out, l1_ffn_norm, l1_ffn_w_gate, l1_ffn_w_up, l1_ffn_w_down, l2_mix_norm, l2_da_w_qkv, l2_da_lq1, l2_da_lk1, l2_da_lq2, l2_da_lk2, l2_da_subln, l2_da_w_o, l2_ffn_norm, l2_ffn_w_gate, l2_ffn_w_up, l2_ffn_w_down, l3_mix_norm, l3_mla_w_in, l3_mla_q_norm, l3_mla_w_uq, l3_mla_kv_norm, l3_mla_w_ukv, l3_mla_w_o, l3_ffn_norm, l3_ffn_w_gate, l3_ffn_w_up, l3_ffn_w_down, final_norm):
    b, s, d = x.shape
    assert b == 1
    h = x.reshape(s, d)

    mc, ms = _rope_tables(s, MLA_ROPE)
    mla_tab = jnp.concatenate([mc, mc, ms, ms], axis=-1)
    mla = (mla_tab, mc.T, ms.T)
    dc, ds = _rope_tables(s, DIFF_HEAD_DIM)
    diff_rope = (jnp.concatenate([dc, dc], axis=-1), jnp.concatenate([-ds, ds], axis=-1))
    diff_rope_t = (dc.T, ds.T)

    def ffn(h, g, wg, wu, wd, out_norm_gain=None):
        return ffn_block(h, g, wg.astype(BF16), wu.astype(BF16), wd.astype(BF16), out_norm_gain)

    h = mla_layer(h, l0_mix_norm, l0_mla_w_in, l0_mla_q_norm, l0_mla_w_uq, l0_mla_kv_norm, l0_mla_w_ukv, l0_mla_w_o, *mla)
    h = ffn(h, l0_ffn_norm, l0_ffn_w_gate, l0_ffn_w_up, l0_ffn_w_down)
    h = hyena_layer(h, l1_mix_norm, l1_hy_w_in, l1_hy_conv_w, l1_hy_conv_b, l1_hy_f_w1, l1_hy_f_b1, l1_hy_f_freq, l1_hy_f_w2, l1_hy_f_b2, l1_hy_f_w3, l1_hy_f_b3, l1_hy_f_w4, l1_hy_bias, l1_hy_w_out)
    h = ffn(h, l1_ffn_norm, l1_ffn_w_gate, l1_ffn_w_up, l1_ffn_w_down)
    h = diff_layer(h, l2_mix_norm, l2_da_w_qkv, l2_da_lq1, l2_da_lk1, l2_da_lq2, l2_da_lk2, l2_da_subln, l2_da_w_o, diff_rope, diff_rope_t, 2)
    h = ffn(h, l2_ffn_norm, l2_ffn_w_gate, l2_ffn_w_up, l2_ffn_w_down)
    h = mla_layer(h, l3_mix_norm, l3_mla_w_in, l3_mla_q_norm, l3_mla_w_uq, l3_mla_kv_norm, l3_mla_w_ukv, l3_mla_w_o, *mla)
    h = ffn(h, l3_ffn_norm, l3_ffn_w_gate, l3_ffn_w_up, l3_ffn_w_down, final_norm)
    return h.reshape(b, s, d)
```

```python
import functools
import math

import jax
import jax.numpy as jnp
import numpy as np
from jax import lax
from jax.experimental import pallas as pl
from jax.experimental.pallas import tpu as pltpu

F32 = jnp.float32
BF16 = jnp.bfloat16

D_MODEL = 2048
SEQ = 8192
RMS_EPS = 1e-6
ROPE_THETA = 10000.0
LOG2E = math.log2(math.e)

MLA_HEADS = 16
MLA_Q_LORA = 768
MLA_KV_LORA = 512
MLA_NOPE = 128
MLA_ROPE = 64
MLA_V = 128
MLA_QK = MLA_NOPE + MLA_ROPE
MLA_SCALE = MLA_QK ** -0.5
MLA_QK_PAD = 256

HY_ORDER = 2
HY_EMB_DIM = 33
HY_TARGET = 1e-2
HY_MAX_DECAY = math.log(HY_TARGET) / 0.3
HY_MIN_DECAY = math.log(HY_TARGET) / 1.5

DIFF_HEAD_DIM = 128
DIFF_HEADS = D_MODEL // (2 * DIFF_HEAD_DIM)
DIFF_QK = DIFF_HEADS * 2 * DIFF_HEAD_DIM
DIFF_SCALE = DIFF_HEAD_DIM ** -0.5
DIFF_EPS = 1e-5

D_FF = -(-8 * D_MODEL // (3 * 256)) * 256

LANE = 128
VMEM_LIMIT = 56 * 1024 * 1024
NEG_BIG = -1e30


def _params(*sem):
    return pltpu.CompilerParams(dimension_semantics=sem, vmem_limit_bytes=VMEM_LIMIT)


def _rms(x, g, eps):
    return x * lax.rsqrt(jnp.mean(x * x, axis=-1, keepdims=True) + eps) * g


def _dot_nt(a, b):
    return lax.dot_general(a, b, (((1,), (1,)), ((), ())), preferred_element_type=F32)


def _rope_rows(y, c, s):
    h = c.shape[0]
    y1, y2 = y[:h], y[h:]
    return y1 * c - y2 * s, y2 * c + y1 * s


def _norm_matmul_kernel(x_ref, g_ref, w_ref, *rest, rope, scale):
    o_ref, xn_ref = rest[-2:]

    @pl.when(pl.program_id(1) == 0)
    def _():
        xn_ref[...] = _rms(x_ref[...], g_ref[...], RMS_EPS).astype(BF16)

    y = jnp.dot(xn_ref[...], w_ref[...], preferred_element_type=F32)
    if rope:
        c = rest[0][...] * scale
        s = rest[1][...] * scale
        for g in range(y.shape[1] // 128):
            t = y[:, g * 128:(g + 1) * 128]
            o_ref[:, g * 128:(g + 1) * 128] = (t * c + pltpu.roll(t, 64, axis=1) * s).astype(o_ref.dtype)
    else:
        o_ref[...] = (y * scale).astype(o_ref.dtype)


def norm_matmul(x, g, w, out_dtype, rope=None, scale=1.0, tm=1024, tn=1024):
    m, k = x.shape
    n = w.shape[1]
    in_specs = [
        pl.BlockSpec((tm, k), lambda i, j: (i, 0)),
        pl.BlockSpec((1, k), lambda i, j: (0, 0)),
        pl.BlockSpec((k, tn), lambda i, j: (0, j)),
    ]
    args = [x, g.reshape(1, k), w]
    if rope is not None:
        in_specs += [pl.BlockSpec((tm, 128), lambda i, j: (i, 0))] * 2
        args += list(rope)
    return pl.pallas_call(
        functools.partial(_norm_matmul_kernel, rope=rope is not None, scale=scale),
        grid=(m // tm, n // tn),
        in_specs=in_specs,
        out_specs=pl.BlockSpec((tm, tn), lambda i, j: (i, j)),
        out_shape=jax.ShapeDtypeStruct((m, n), out_dtype),
        scratch_shapes=[pltpu.VMEM((tm, k), BF16)],
        compiler_params=_params("parallel", "arbitrary"),
        name="norm_matmul",
    )(*args)


def _norm_matmul_t_kernel(x_ref, g_ref, wt_ref, *rest, rope, scale):
    o_ref, xn_ref = rest[-2:]

    @pl.when(pl.program_id(1) == 0)
    def _():
        xn_ref[...] = _rms(x_ref[...], g_ref[...], RMS_EPS).astype(BF16)

    y = _dot_nt(wt_ref[...], xn_ref[...]) * scale
    if rope:
        c = rest[0][...]
        s = rest[1][...]
        for g in range(y.shape[0] // 128):
            r1, r2 = _rope_rows(y[g * 128:(g + 1) * 128], c, s)
            o_ref[g * 128:g * 128 + 64, :] = r1.astype(o_ref.dtype)
            o_ref[g * 128 + 64:(g + 1) * 128, :] = r2.astype(o_ref.dtype)
    else:
        o_ref[...] = y.astype(o_ref.dtype)


def norm_matmul_t(x, g, wt, out_dtype, rope_t=None, scale=1.0, tm=1024, tn=1024):
    m, k = x.shape
    n = wt.shape[0]
    in_specs = [
        pl.BlockSpec((tm, k), lambda i, j: (i, 0)),
        pl.BlockSpec((1, k), lambda i, j: (0, 0)),
        pl.BlockSpec((tn, k), lambda i, j: (j, 0)),
    ]
    args = [x, g.reshape(1, k), wt]
    if rope_t is not None:
        in_specs += [pl.BlockSpec((64, tm), lambda i, j: (0, i))] * 2
        args += list(rope_t)
    return pl.pallas_call(
        functools.partial(_norm_matmul_t_kernel, rope=rope_t is not None, scale=scale),
        grid=(m // tm, n // tn),
        in_specs=in_specs,
        out_specs=pl.BlockSpec((tn, tm), lambda i, j: (j, i)),
        out_shape=jax.ShapeDtypeStruct((n, m), out_dtype),
        scratch_shapes=[pltpu.VMEM((tm, k), BF16)],
        compiler_params=_params("parallel", "arbitrary"),
        name="norm_matmul_t",
    )(*args)


def _matmul_res_kernel(a_ref, w_ref, r_ref, o_ref):
    if len(a_ref.shape) == 3:
        a = jnp.concatenate([a_ref[q] for q in range(a_ref.shape[0])], axis=1)
    else:
        a = a_ref[...]
    o_ref[...] = r_ref[...] + jnp.dot(a.astype(BF16), w_ref[...], preferred_element_type=F32)


def matmul_residual(a, w, res, a_in_slabs=False, tm=1024, tn=1024):
    k, n = w.shape
    m = res.shape[0]
    if a_in_slabs:
        a_spec = pl.BlockSpec((k // LANE, tm, LANE), lambda i, j: (0, i, 0))
    else:
        a_spec = pl.BlockSpec((tm, k), lambda i, j: (i, 0))
    return pl.pallas_call(
        _matmul_res_kernel,
        grid=(m // tm, n // tn),
        in_specs=[
            a_spec,
            pl.BlockSpec((k, tn), lambda i, j: (0, j)),
            pl.BlockSpec((tm, tn), lambda i, j: (i, j)),
        ],
        out_specs=pl.BlockSpec((tm, tn), lambda i, j: (i, j)),
        out_shape=jax.ShapeDtypeStruct((m, n), F32),
        compiler_params=_params("parallel", "parallel"),
        name="matmul_residual",
    )(a, w, res)


def _ffn_kernel(x_ref, g_ref, wg_ref, wu_ref, wd_ref, *rest, out_norm):
    o_ref, xn_ref = rest[-2:]

    @pl.when(pl.program_id(1) == 0)
    def _():
        x = x_ref[...]
        xn_ref[...] = _rms(x, g_ref[...], RMS_EPS).astype(BF16)
        o_ref[...] = x

    xn = xn_ref[...]
    a = jnp.dot(xn, wg_ref[...], preferred_element_type=F32)
    b = jnp.dot(xn, wu_ref[...], preferred_element_type=F32)
    h = (a * jax.nn.sigmoid(a) * b).astype(BF16)
    o_ref[...] += jnp.dot(h, wd_ref[...], preferred_element_type=F32)

    if out_norm:
        @pl.when(pl.program_id(1) == pl.num_programs(1) - 1)
        def _():
            o_ref[...] = _rms(o_ref[...], rest[0][...], RMS_EPS)


def ffn_block(x, g, wg, wu, wd, out_norm_gain=None, tm=512, tf=512):
    m, d = x.shape
    f = wg.shape[1]
    vec = pl.BlockSpec((1, d), lambda i, j: (0, 0))
    in_specs = [
        pl.BlockSpec((tm, d), lambda i, j: (i, 0)),
        vec,
        pl.BlockSpec((d, tf), lambda i, j: (0, j)),
        pl.BlockSpec((d, tf), lambda i, j: (0, j)),
        pl.BlockSpec((tf, d), lambda i, j: (j, 0)),
    ]
    args = [x, g.reshape(1, d), wg, wu, wd]
    if out_norm_gain is not None:
        in_specs.append(vec)
        args.append(out_norm_gain.reshape(1, d))
    return pl.pallas_call(
        functools.partial(_ffn_kernel, out_norm=out_norm_gain is not None),
        grid=(m // tm, f // tf),
        in_specs=in_specs,
        out_specs=pl.BlockSpec((tm, d), lambda i, j: (i, 0)),
        out_shape=jax.ShapeDtypeStruct((m, d), F32),
        scratch_shapes=[pltpu.VMEM((tm, d), BF16)],
        compiler_params=_params("parallel", "arbitrary"),
        name="ffn_block",
    )(*args)


def _flash_t(qs, k_of_part, k_ref, vt_ref, sa_ref, sb_ref, acc_ref, *, tk):
    parts = len(qs)
    tp = qs[0].shape[1]
    nk = k_ref.shape[0] // tk
    assert nk % 2 == 0 and nk >= 4

    def chunk(c):
        return pl.ds(c * tk if isinstance(c, int) else pl.multiple_of(c * tk, tk), tk)

    def scores(c, s_ref):
        k = k_ref[chunk(c), :]
        mc = ()
        for i in range(parts):
            st = jnp.dot(k_of_part(k, i), qs[i], preferred_element_type=F32)
            s_ref[i] = st
            mc += (jnp.max(st, axis=0, keepdims=True),)
        return mc

    def softmax_pv(c, s_ref, mc, state):
        vt = vt_ref[:, chunk(c)]
        new = ()
        for i in range(parts):
            m, l = state[2 * i:2 * i + 2]
            m_new = jnp.maximum(m, mc[i])
            alpha = jnp.exp2(m - m_new)
            p = jnp.exp2(s_ref[i] - m_new)
            l = alpha * l + jnp.sum(p, axis=0, keepdims=True)
            acc_ref[i] = alpha * acc_ref[i] + jnp.dot(vt, p.astype(BF16), preferred_element_type=F32)
            new += (m_new, l)
        return new

    def body(c2, carry):
        mc_a, state = carry[:parts], carry[parts:]
        c = 2 * c2
        mc_b = scores(c + 1, sb_ref)
        state = softmax_pv(c, sa_ref, mc_a, state)
        mc_a = scores(c + 2, sa_ref)
        state = softmax_pv(c + 1, sb_ref, mc_b, state)
        return mc_a + state

    acc_ref[...] = jnp.zeros(acc_ref.shape, F32)
    state0 = (jnp.full((1, tp), NEG_BIG, F32), jnp.zeros((1, tp), F32)) * parts
    carry = lax.fori_loop(0, nk // 2 - 1, body, scores(0, sa_ref) + state0)
    mc_a, state = carry[:parts], carry[parts:]
    mc_b = scores(nk - 1, sb_ref)
    state = softmax_pv(nk - 2, sa_ref, mc_a, state)
    state = softmax_pv(nk - 1, sb_ref, mc_b, state)
    return state[1::2]


def _rope_pair(t, tab):
    t = t * tab
    return t + pltpu.roll(t, 64, axis=1)


def _mla_in_kernel(x_ref, g_ref, w_ref, qn_ref, kvn_ref, tab_ref, cq_ref, ckv_ref, kr_ref):
    xn = _rms(x_ref[...], g_ref[...], RMS_EPS).astype(BF16)
    h = jnp.dot(xn, w_ref[...], preferred_element_type=F32)
    cq_ref[...] = _rms(h[:, :MLA_Q_LORA], qn_ref[...], RMS_EPS).astype(BF16)
    ckv_ref[...] = _rms(h[:, MLA_Q_LORA:MLA_Q_LORA + MLA_KV_LORA], kvn_ref[...], RMS_EPS).astype(BF16)
    kr = _rope_pair(h[:, MLA_Q_LORA + MLA_KV_LORA:], tab_ref[...])
    lane = lax.broadcasted_iota(jnp.int32, kr.shape, 1)
    kr_ref[...] = jnp.where(lane < MLA_ROPE, kr, 0.0).astype(BF16)


def mla_in(x, g, w_in_ext, q_norm, kv_norm, tab, tm=512):
    m, d = x.shape
    n = w_in_ext.shape[1]
    row = lambda i: (i, 0)
    fixed = lambda i: (0, 0)
    return pl.pallas_call(
        _mla_in_kernel,
        grid=(m // tm,),
        in_specs=[
            pl.BlockSpec((tm, d), row),
            pl.BlockSpec((1, d), fixed),
            pl.BlockSpec((d, n), fixed),
            pl.BlockSpec((1, MLA_Q_LORA), fixed),
            pl.BlockSpec((1, MLA_KV_LORA), fixed),
            pl.BlockSpec((tm, 128), row),
        ],
        out_specs=[
            pl.BlockSpec((tm, MLA_Q_LORA), row),
            pl.BlockSpec((tm, MLA_KV_LORA), row),
            pl.BlockSpec((tm, 128), row),
        ],
        out_shape=[
            jax.ShapeDtypeStruct((m, MLA_Q_LORA), BF16),
            jax.ShapeDtypeStruct((m, MLA_KV_LORA), BF16),
            jax.ShapeDtypeStruct((m, 128), BF16),
        ],
        compiler_params=_params("parallel"),
        name="mla_in",
    )(x, g.reshape(1, d), w_in_ext, q_norm.reshape(1, -1), kv_norm.reshape(1, -1), tab)


def _mla_qt_kernel(c_ref, wt_ref, cos_ref, sin_ref, q_ref, *, heads):
    y = _dot_nt(wt_ref[...], c_ref[...]) * (MLA_SCALE * LOG2E)
    c = cos_ref[...]
    s = sin_ref[...]
    zeros = jnp.zeros((MLA_QK_PAD - MLA_QK, y.shape[1]), BF16)
    for h in range(heads):
        src = h * MLA_QK
        dst = h * MLA_QK_PAD
        q_ref[dst:dst + MLA_NOPE, :] = y[src:src + MLA_NOPE].astype(BF16)
        r1, r2 = _rope_rows(y[src + MLA_NOPE:src + MLA_QK], c, s)
        q_ref[dst + MLA_NOPE:dst + MLA_NOPE + 32, :] = r1.astype(BF16)
        q_ref[dst + MLA_NOPE + 32:dst + MLA_QK, :] = r2.astype(BF16)
        q_ref[dst + MLA_QK:dst + MLA_QK_PAD, :] = zeros


def mla_qt(cq, w_uq_t, cos_t, sin_t, tm=1024, heads_per_step=4):
    m, k = cq.shape
    tn_in = heads_per_step * MLA_QK
    tn_out = heads_per_step * MLA_QK_PAD
    return pl.pallas_call(
        functools.partial(_mla_qt_kernel, heads=heads_per_step),
        grid=(m // tm, MLA_HEADS // heads_per_step),
        in_specs=[
            pl.BlockSpec((tm, k), lambda i, j: (i, 0)),
            pl.BlockSpec((tn_in, k), lambda i, j: (j, 0)),
            pl.BlockSpec((32, tm), lambda i, j: (0, i)),
            pl.BlockSpec((32, tm), lambda i, j: (0, i)),
        ],
        out_specs=pl.BlockSpec((tn_out, tm), lambda i, j: (j, i)),
        out_shape=jax.ShapeDtypeStruct((MLA_HEADS * MLA_QK_PAD, m), BF16),
        compiler_params=_params("parallel", "parallel"),
        name="mla_qt",
    )(cq, w_uq_t, cos_t, sin_t)


def _mla_kv_kernel(c_ref, wk_ref, wvt_ref, kr_ref, k_ref, vt_ref, *, heads):
    c = c_ref[...]
    yk = jnp.dot(c, wk_ref[...], preferred_element_type=F32)
    kr = kr_ref[...]
    for h in range(heads):
        k_ref[:, h * 256:h * 256 + 128] = yk[:, h * 128:(h + 1) * 128].astype(BF16)
        k_ref[:, h * 256 + 128:h * 256 + 256] = kr
    vt_ref[...] = _dot_nt(wvt_ref[...], c).astype(BF16)


def mla_kv(ckv, w_k, w_v_t, kr, tm=1024, heads_per_step=4):
    m, k = ckv.shape
    hp = heads_per_step
    return pl.pallas_call(
        functools.partial(_mla_kv_kernel, heads=hp),
        grid=(m // tm, MLA_HEADS // hp),
        in_specs=[
            pl.BlockSpec((tm, k), lambda i, j: (i, 0)),
            pl.BlockSpec((k, hp * MLA_NOPE), lambda i, j: (0, j)),
            pl.BlockSpec((hp * MLA_V, k), lambda i, j: (j, 0)),
            pl.BlockSpec((tm, 128), lambda i, j: (i, 0)),
        ],
        out_specs=[
            pl.BlockSpec((tm, hp * MLA_QK_PAD), lambda i, j: (i, j)),
            pl.BlockSpec((hp * MLA_V, tm), lambda i, j: (j, i)),
        ],
        out_shape=[
            jax.ShapeDtypeStruct((m, MLA_HEADS * MLA_QK_PAD), BF16),
            jax.ShapeDtypeStruct((MLA_HEADS * MLA_V, m), BF16),
        ],
        compiler_params=_params("parallel", "parallel"),
        name="mla_kv",
    )(ckv, w_k, w_v_t, kr)


def _mla_attn_kernel(qt_ref, k_ref, vt_ref, o_ref, sa_ref, sb_ref, acc_ref, *, tk):
    parts, _, tp = sa_ref.shape
    qs = [qt_ref[:, i * tp:(i + 1) * tp] for i in range(parts)]
    ls = _flash_t(qs, lambda k, i: k, k_ref, vt_ref, sa_ref, sb_ref, acc_ref, tk=tk)
    for i in range(parts):
        o_ref[i * tp:(i + 1) * tp, :] = (acc_ref[i] / ls[i]).T.astype(o_ref.dtype)


def mla_attention(qt, k, vt, tq=2048, tk=512, parts=2):
    s = k.shape[0]
    return pl.pallas_call(
        functools.partial(_mla_attn_kernel, tk=tk),
        grid=(MLA_HEADS, s // tq),
        in_specs=[
            pl.BlockSpec((MLA_QK_PAD, tq), lambda h, i: (h, i)),
            pl.BlockSpec((s, MLA_QK_PAD), lambda h, i: (0, h)),
            pl.BlockSpec((MLA_V, s), lambda h, i: (h, 0)),
        ],
        out_specs=pl.BlockSpec((tq, MLA_V), lambda h, i: (i, h)),
        out_shape=jax.ShapeDtypeStruct((s, MLA_HEADS * MLA_V), BF16),
        scratch_shapes=[
            pltpu.VMEM((parts, tk, tq // parts), F32),
            pltpu.VMEM((parts, tk, tq // parts), F32),
            pltpu.VMEM((parts, MLA_V, tq // parts), F32),
        ],
        compiler_params=_params("parallel", "parallel"),
        name="mla_attention",
    )(qt, k, vt)


def _rot_cols(w):
    half = w.shape[-1] // 2
    return jnp.concatenate([-w[..., half:], w[..., :half]], axis=-1)


def mla_layer(x, mix_norm, w_in, q_norm, w_uq, kv_norm, w_ukv, w_o, tab, cos_t, sin_t):
    w_kr = w_in[:, MLA_Q_LORA + MLA_KV_LORA:]
    w_in_ext = jnp.concatenate([w_in, _rot_cols(w_kr)], axis=1).astype(BF16)
    w_uq_t = w_uq.astype(BF16).T
    wkv = w_ukv.astype(BF16).reshape(MLA_KV_LORA, MLA_HEADS, MLA_NOPE + MLA_V)
    w_k = wkv[..., :MLA_NOPE].reshape(MLA_KV_LORA, -1)
    w_v_t = wkv[..., MLA_NOPE:].reshape(MLA_KV_LORA, -1).T
    cq, ckv, kr = mla_in(x, mix_norm, w_in_ext, q_norm, kv_norm, tab)
    qt = mla_qt(cq, w_uq_t, cos_t, sin_t)
    k, vt = mla_kv(ckv, w_k, w_v_t, kr)
    o = mla_attention(qt, k, vt)
    return matmul_residual(o, w_o.astype(BF16), x)


def _diff_attn_kernel(lam_ref, qt_ref, k_ref, vt_ref, g_ref, o_ref, sa_ref, sb_ref, acc_ref, *, tk, lam_init):
    d = DIFF_HEAD_DIM
    qs = [qt_ref[:d, :], qt_ref[d:, :]]
    l0, l1 = _flash_t(qs, lambda k, i: k[:, i * d:(i + 1) * d], k_ref, vt_ref, sa_ref, sb_ref, acc_ref, tk=tk)
    ot = acc_ref[0] / l0 - lam_ref[0, 0] * (acc_ref[1] / l1)
    o = _rms(ot.T, g_ref[...], DIFF_EPS) * (1.0 - lam_init)
    o_ref[...] = o.astype(o_ref.dtype)


def diff_attention(qt, k, vt, lam, subln, lam_init, tq=1024, tk=512):
    s = k.shape[0]
    hd = 2 * DIFF_HEAD_DIM
    return pl.pallas_call(
        functools.partial(_diff_attn_kernel, tk=tk, lam_init=lam_init),
        grid=(DIFF_HEADS, s // tq),
        in_specs=[
            pl.BlockSpec(memory_space=pltpu.SMEM),
            pl.BlockSpec((hd, tq), lambda h, i: (h, i)),
            pl.BlockSpec((s, hd), lambda h, i: (0, h)),
            pl.BlockSpec((hd, s), lambda h, i: (h, 0)),
            pl.BlockSpec((1, hd), lambda h, i: (0, 0)),
        ],
        out_specs=pl.BlockSpec((tq, hd), lambda h, i: (i, h)),
        out_shape=jax.ShapeDtypeStruct((s, DIFF_HEADS * hd), BF16),
        scratch_shapes=[
            pltpu.VMEM((2, tk, tq), F32),
            pltpu.VMEM((2, tk, tq), F32),
            pltpu.VMEM((2, hd, tq), F32),
        ],
        compiler_params=_params("parallel", "parallel"),
        name="diff_attention",
    )(lam, qt, k, vt, subln.reshape(1, hd))


def _lambda_kernel(lq1_ref, lk1_ref, lq2_ref, lk2_ref, o_ref, *, lam_init):
    a = jnp.sum(lq1_ref[...] * lk1_ref[...], axis=-1, keepdims=True)
    b = jnp.sum(lq2_ref[...] * lk2_ref[...], axis=-1, keepdims=True)
    o_ref[...] = jnp.exp(a) - jnp.exp(b) + lam_init


def diff_lambda(lq1, lk1, lq2, lk2, lam_init):
    args = [v.reshape(1, -1) for v in (lq1, lk1, lq2, lk2)]
    return pl.pallas_call(
        functools.partial(_lambda_kernel, lam_init=lam_init),
        out_shape=jax.ShapeDtypeStruct((1, 1), F32),
        name="diff_lambda",
    )(*args)


def diff_layer(x, mix_norm, w_qkv, lq1, lk1, lq2, lk2, subln, w_o, rope, rope_t, layer_idx):
    lam_init = 0.8 - 0.6 * math.exp(-0.3 * layer_idx)
    w = w_qkv.astype(BF16)
    qt = norm_matmul_t(x, mix_norm, w[:, :DIFF_QK].T, BF16, rope_t=rope_t, scale=DIFF_SCALE * LOG2E)
    k = norm_matmul(x, mix_norm, w[:, DIFF_QK:2 * DIFF_QK], BF16, rope=rope)
    vt = norm_matmul_t(x, mix_norm, w[:, 2 * DIFF_QK:].T, BF16)
    lam = diff_lambda(lq1, lk1, lq2, lk2, lam_init)
    o = diff_attention(qt, k, vt, lam, subln, lam_init)
    return matmul_residual(o, w_o.astype(BF16), x)


FFT_R = 128
FFT_N = FFT_R * FFT_R
FFT_G = 8
HY_TC = 512
HY_NQ = HY_TC // LANE


def _dft_constants():
    r = np.arange(FFT_R)
    n2, k1, n1 = r[:, None, None], r[None, :, None], r[None, None, :]
    th = 2.0 * np.pi * ((k1 * (FFT_R * n1 + n2)) % FFT_N) / FFT_N
    w1 = np.concatenate([np.cos(th), -np.sin(th)], axis=1)
    tht = np.transpose(th, (0, 2, 1))
    v1 = np.concatenate([np.cos(tht), -np.sin(tht)], axis=2) / FFT_N
    ph = 2.0 * np.pi * ((r[:, None] * r[None, :]) % FFT_R) / FFT_R
    c, s = np.cos(ph), np.sin(ph)
    g_fwd = np.block([[c, s], [-s, c]])
    g_inv = np.block([[c, -s], [s, c]])
    as_bf16 = lambda a: a.astype(np.float32).astype(BF16)
    half = FFT_R // 2
    return dict(w1_full=as_bf16(w1), w1_half=as_bf16(w1[:, :, :half]), v1_half=as_bf16(v1[:, :half, :]),
                g_fwd=as_bf16(g_fwd), g_inv=as_bf16(g_inv))


def _pack_pair(x):
    h = x.shape[0] // 2
    hi = lax.bitcast_convert_type(x[:h].astype(BF16).astype(F32), jnp.uint32)
    lo = lax.bitcast_convert_type(x[h:].astype(BF16).astype(F32), jnp.uint32)
    return hi | (lo >> 16)


def _unpack_pair(p):
    hi = lax.bitcast_convert_type(p & jnp.uint32(0xFFFF0000), F32)
    lo = lax.bitcast_convert_type(p << 16, F32)
    return jnp.concatenate([hi, lo], axis=0).astype(BF16)


def _short_conv_kernel(u_ref, prev_ref, next_ref, w_ref, b_ref, o_ref):
    i = pl.program_id(0)
    x = u_ref[...]
    tm = x.shape[0]
    row = lax.broadcasted_iota(jnp.int32, x.shape, 0)
    before = jnp.where(i > 0, prev_ref[7:8, :], 0.0)
    after = jnp.where(i < pl.num_programs(0) - 1, next_ref[0:1, :], 0.0)
    up = jnp.where(row == 0, before, pltpu.roll(x, 1, axis=0))
    dn = jnp.where(row == tm - 1, after, pltpu.roll(x, tm - 1, axis=0))
    y = up * w_ref[0:1, :] + x * w_ref[1:2, :] + dn * w_ref[2:3, :] + b_ref[...]
    for q in range(o_ref.shape[0]):
        o_ref[q] = y[:, q * LANE:(q + 1) * LANE]


def short_conv(u, conv_w, conv_b, tm=1024, tc=HY_TC):
    s, n = u.shape
    nb = tm // 8
    return pl.pallas_call(
        _short_conv_kernel,
        grid=(s // tm, n // tc),
        in_specs=[
            pl.BlockSpec((tm, tc), lambda i, j: (i, j)),
            pl.BlockSpec((8, tc), lambda i, j: (jnp.maximum(i * nb - 1, 0), j)),
            pl.BlockSpec((8, tc), lambda i, j: (jnp.minimum((i + 1) * nb, s // 8 - 1), j)),
            pl.BlockSpec((3, tc), lambda i, j: (0, j)),
            pl.BlockSpec((1, tc), lambda i, j: (0, j)),
        ],
        out_specs=pl.BlockSpec((tc // LANE, tm, LANE), lambda i, j: (j, i, 0)),
        out_shape=jax.ShapeDtypeStruct((n // LANE, s, LANE), F32),
        compiler_params=_params("parallel", "parallel"),
        name="hy_short_conv",
    )(u, u, u, conv_w, conv_b.reshape(1, n))


def _filter_mlp_kernel(z_ref, w1_ref, b1_ref, f_ref, w2_ref, b2_ref, w3_ref, b3_ref, o_ref):
    hp = lax.Precision.HIGHEST
    f = f_ref[...]
    h = jnp.sin(f * (jnp.dot(z_ref[...], w1_ref[...], precision=hp, preferred_element_type=F32) + b1_ref[...]))
    h = jnp.sin(f * (jnp.dot(h, w2_ref[...], precision=hp, preferred_element_type=F32) + b2_ref[...]))
    o_ref[...] = jnp.sin(f * (jnp.dot(h, w3_ref[...], precision=hp, preferred_element_type=F32) + b3_ref[...]))


def filter_mlp(zfeat, f_w1, f_b1, f_freq, f_w2, f_b2, f_w3, f_b3, tm=2048):
    n, e = zfeat.shape
    wd = f_w2.shape[0]
    row = lambda i: (i, 0)
    fixed = lambda i: (0, 0)
    vec = lambda v: v.reshape(1, wd)
    return pl.pallas_call(
        _filter_mlp_kernel,
        grid=(n // tm,),
        in_specs=[pl.BlockSpec((tm, e), row), pl.BlockSpec((e, wd), fixed), pl.BlockSpec((1, wd), fixed),
                  pl.BlockSpec((1, wd), fixed), pl.BlockSpec((wd, wd), fixed), pl.BlockSpec((1, wd), fixed),
                  pl.BlockSpec((wd, wd), fixed), pl.BlockSpec((1, wd), fixed)],
        out_specs=pl.BlockSpec((tm, wd), row),
        out_shape=jax.ShapeDtypeStruct((n, wd), F32),
        compiler_params=_params("parallel"),
        name="hy_filter_mlp",
    )(zfeat, f_w1, vec(f_b1), vec(f_freq), f_w2, vec(f_b2), f_w3, vec(f_b3))


def _rows_by_group(ref):
    t = [jnp.swapaxes(ref[i], 0, 1) for i in range(ref.shape[0])]
    return [jnp.concatenate([ti[r] for ti in t], axis=1) for r in range(FFT_G)]


def _store_by_group(ref, vals):
    for i in range(ref.shape[0]):
        ref[i] = jnp.swapaxes(jnp.stack([v[:, i * LANE:(i + 1) * LANE] for v in vals], axis=0), 0, 1)


def _kern_stage1_kernel(h_ref, wf_ref, wb_ref, dl_ref, w1_ref, kb_ref, nrm_ref, *, seq):
    h_rows = _rows_by_group(h_ref)
    packed = []
    g = pl.program_id(2)
    half = FFT_R // 2
    wf = wf_ref[...]
    wb = wb_ref[...]
    rate = dl_ref[...] * (-1.0 / (seq - 1))
    n1 = lax.broadcasted_iota(jnp.int32, (half, 1), 0)
    acc = jnp.zeros(nrm_ref.shape[1:], F32)
    for r in range(FFT_G):
        n2 = g * FFT_G + r
        hs = h_rows[r].astype(BF16)
        lag_f = FFT_R * n1 + n2
        lag_b = seq - FFT_R * n1 - n2
        kf = jnp.dot(hs[:half], wf, preferred_element_type=F32) * jnp.exp(lag_f.astype(F32) * rate)
        kb = jnp.dot(hs[half:], wb, preferred_element_type=F32) * jnp.exp(lag_b.astype(F32) * rate)
        kb = jnp.where(lag_b == seq, 0.0, kb)
        slab = jnp.concatenate([kf, kb], axis=0)
        acc += jnp.sum(jnp.abs(slab), axis=0, keepdims=True)
        packed.append(_pack_pair(jnp.dot(w1_ref[r], slab.astype(BF16), preferred_element_type=F32)))
    _store_by_group(kb_ref.at[0], packed)

    @pl.when(g == 0)
    def _():
        nrm_ref[0] = acc

    @pl.when(g > 0)
    def _():
        nrm_ref[0] += acc


def kern_stage1(h3, w4, deltas, w1_full, seq, tc=HY_TC):
    d = deltas.shape[0]
    jn = d // tc
    h3v = h3.reshape(1, FFT_R, FFT_R, LANE)
    kern = functools.partial(_kern_stage1_kernel, seq=seq)
    return pl.pallas_call(
        kern,
        grid=(jn, HY_ORDER, FFT_R // FFT_G),
        in_specs=[
            pl.BlockSpec((1, FFT_R, FFT_G, LANE), lambda j, o, g: (0, 0, g, 0)),
            pl.BlockSpec((LANE, tc), lambda j, o, g: (0, o * jn + j)),
            pl.BlockSpec((LANE, tc), lambda j, o, g: (0, (HY_ORDER + o) * jn + j)),
            pl.BlockSpec((1, tc), lambda j, o, g: (0, j)),
            pl.BlockSpec((FFT_G, 2 * FFT_R, FFT_R), lambda j, o, g: (g, 0, 0)),
        ],
        out_specs=[
            pl.BlockSpec((1, HY_NQ, FFT_R, FFT_G, LANE), lambda j, o, g: (o, j, 0, g, 0)),
            pl.BlockSpec((1, 1, tc), lambda j, o, g: (o, 0, j)),
        ],
        out_shape=[
            jax.ShapeDtypeStruct((HY_ORDER, d // LANE, FFT_R, FFT_R, LANE), jnp.uint32),
            jax.ShapeDtypeStruct((HY_ORDER, 1, d), F32),
        ],
        compiler_params=_params("parallel", "parallel", "arbitrary"),
        name="hy_kern_stage1",
    )(h3v, w4, w4, deltas.reshape(1, d), w1_full)


def _join_lanes(ref, idx):
    return jnp.concatenate([ref[(q,) + idx] for q in range(ref.shape[0])], axis=1)


def _kern_stage3_kernel(kb_ref, g_ref, o_ref):
    gm = g_ref[...]
    for r in range(FFT_G):
        b = _unpack_pair(_join_lanes(kb_ref.at[0], (r,)))
        o_ref[0, r] = jnp.dot(gm, b, preferred_element_type=F32).astype(o_ref.dtype)


def kern_stage3(kb, g_fwd, tc=HY_TC):
    d = kb.shape[1] * LANE
    return pl.pallas_call(
        _kern_stage3_kernel,
        grid=(HY_ORDER, FFT_R // FFT_G, d // tc),
        in_specs=[
            pl.BlockSpec((1, HY_NQ, FFT_G, FFT_R, LANE), lambda o, kg, j: (o, j, kg, 0, 0)),
            pl.BlockSpec((2 * FFT_R, 2 * FFT_R), lambda o, kg, j: (0, 0)),
        ],
        out_specs=pl.BlockSpec((1, FFT_G, 2 * FFT_R, tc), lambda o, kg, j: (o, kg, 0, j)),
        out_shape=jax.ShapeDtypeStruct((HY_ORDER, FFT_R, 2 * FFT_R, d), BF16),
        compiler_params=_params("parallel", "parallel", "parallel"),
        name="hy_kern_stage3",
    )(kb, g_fwd)


def _sig_stage1_kernel(z_ref, w1_ref, o_ref):
    z_rows = _rows_by_group(z_ref)
    _store_by_group(o_ref, [_pack_pair(jnp.dot(w1_ref[r], z_rows[r].astype(BF16), preferred_element_type=F32))
                            for r in range(FFT_G)])


def sig_stage1(z_src, slab0, d, w1_half, tc=HY_TC):
    s = z_src.shape[1]
    rows = s // FFT_R
    zv = z_src.reshape(z_src.shape[0], rows, FFT_R, LANE)
    return pl.pallas_call(
        _sig_stage1_kernel,
        grid=(FFT_R // FFT_G, d // tc),
        in_specs=[
            pl.BlockSpec((HY_NQ, rows, FFT_G, LANE), lambda g, j: (slab0 // HY_NQ + j, 0, g, 0)),
            pl.BlockSpec((FFT_G, 2 * FFT_R, rows), lambda g, j: (g, 0, 0)),
        ],
        out_specs=pl.BlockSpec((HY_NQ, FFT_R, FFT_G, LANE), lambda g, j: (j, 0, g, 0)),
        out_shape=jax.ShapeDtypeStruct((d // LANE, FFT_R, FFT_R, LANE), jnp.uint32),
        compiler_params=_params("parallel", "parallel"),
        name="hy_sig_stage1",
    )(zv, w1_half)


def _sig_stage3_kernel(zb_ref, ks_ref, gf_ref, gi_ref, o_ref):
    gf = gf_ref[...]
    gi = gi_ref[...]
    for r in range(FFT_G):
        x = jnp.dot(gf, _unpack_pair(_join_lanes(zb_ref, (r,))), preferred_element_type=F32)
        ks = ks_ref[0, r].astype(F32)
        xr, xi = x[:FFT_R], x[FFT_R:]
        kr, ki = ks[:FFT_R], ks[FFT_R:]
        y = jnp.concatenate([xr * kr - xi * ki, xr * ki + xi * kr], axis=0).astype(BF16)
        packed = _pack_pair(jnp.dot(gi, y, preferred_element_type=F32))
        for q in range(o_ref.shape[0]):
            o_ref[q, r] = packed[:, q * LANE:(q + 1) * LANE]


def sig_stage3(zb, kspec, order, g_fwd, g_inv, tc=HY_TC):
    d = zb.shape[0] * LANE
    blk = pl.BlockSpec((HY_NQ, FFT_G, FFT_R, LANE), lambda kg, j: (j, kg, 0, 0))
    mat = pl.BlockSpec((2 * FFT_R, 2 * FFT_R), lambda kg, j: (0, 0))
    return pl.pallas_call(
        _sig_stage3_kernel,
        grid=(FFT_R // FFT_G, d // tc),
        in_specs=[blk, pl.BlockSpec((1, FFT_G, 2 * FFT_R, tc), lambda kg, j: (order, kg, 0, j)), mat, mat],
        out_specs=blk,
        out_shape=jax.ShapeDtypeStruct(zb.shape, jnp.uint32),
        compiler_params=_params("parallel", "parallel"),
        name="hy_sig_stage3",
    )(zb, kspec, g_fwd, g_inv)


def _sig_inverse1_kernel(yb_ref, v_ref, gate_ref, z_ref, nrm_ref, bias_ref, o_ref):
    inv = 1.0 / nrm_ref[0]
    bias = bias_ref[0]
    yb_rows, gate_rows, z_rows = _rows_by_group(yb_ref), _rows_by_group(gate_ref), _rows_by_group(z_ref)
    out = []
    for r in range(FFT_G):
        conv = jnp.dot(v_ref[r], _unpack_pair(yb_rows[r]), preferred_element_type=F32) * inv
        out.append(gate_rows[r] * (conv + z_rows[r] * bias))
    _store_by_group(o_ref, out)


def sig_inverse1(yb, v1_half, gate_src, gate_slab0, z_src, z_slab0, nrm, bias, order, tc=HY_TC):
    d = yb.shape[0] * LANE
    s = gate_src.shape[1]
    rows = s // FFT_R
    view = lambda a: a.reshape(a.shape[0], rows, FFT_R, LANE)
    sig = (HY_NQ, rows, FFT_G, LANE)
    return pl.pallas_call(
        _sig_inverse1_kernel,
        grid=(FFT_R // FFT_G, d // tc),
        in_specs=[
            pl.BlockSpec((HY_NQ, FFT_R, FFT_G, LANE), lambda g, j: (j, 0, g, 0)),
            pl.BlockSpec((FFT_G, rows, 2 * FFT_R), lambda g, j: (g, 0, 0)),
            pl.BlockSpec(sig, lambda g, j: (gate_slab0 // HY_NQ + j, 0, g, 0)),
            pl.BlockSpec(sig, lambda g, j: (z_slab0 // HY_NQ + j, 0, g, 0)),
            pl.BlockSpec((1, 1, tc), lambda g, j: (order, 0, j)),
            pl.BlockSpec((1, 1, tc), lambda g, j: (order, 0, j)),
        ],
        out_specs=pl.BlockSpec(sig, lambda g, j: (j, 0, g, 0)),
        out_shape=jax.ShapeDtypeStruct((d // LANE, rows, FFT_R, LANE), F32),
        compiler_params=_params("parallel", "parallel"),
        name="hy_sig_inverse1",
    )(yb, v1_half, view(gate_src), view(z_src), nrm, bias.reshape(HY_ORDER, 1, d)).reshape(d // LANE, s, LANE)


def _hyena_positions(seq):
    t = jnp.linspace(0.0, 1.0, seq, dtype=F32)[:, None]
    bands = (HY_EMB_DIM - 1) // 2
    w = 2.0 * math.pi * jnp.arange(seq, dtype=F32)[:, None] / seq
    f = jnp.linspace(1e-4, bands - 1, bands, dtype=F32)[None, :]
    z = jnp.concatenate([t, jnp.cos(f * w), -jnp.sin(f * w)], axis=-1)
    z = jnp.concatenate([z, z[:1], z[:0:-1]], axis=0)
    return jnp.pad(z, ((0, 0), (0, 128 - HY_EMB_DIM)))


def hyena_mixer(u, conv_w, conv_b, f_w1, f_b1, f_freq, f_w2, f_b2, f_w3, f_b3, f_w4, bias):
    s = u.shape[0]
    d = u.shape[1] // 3
    assert 2 * s == FFT_N and d % HY_TC == 0
    c = _dft_constants()
    u = short_conv(u, conv_w, conv_b)
    h3 = filter_mlp(_hyena_positions(s), jnp.pad(f_w1, ((0, LANE - HY_EMB_DIM), (0, 0))), f_b1, f_freq, f_w2, f_b2, f_w3, f_b3)
    width = h3.shape[1]
    h3 = jnp.pad(h3, ((0, 0), (0, LANE - width)))
    w4 = jnp.pad(f_w4.astype(BF16), ((0, LANE - width), (0, 0)))
    deltas = jnp.abs(jnp.linspace(HY_MIN_DECAY, HY_MAX_DECAY, d, dtype=F32))
    kb, nrm = kern_stage1(h3, w4, deltas, c["w1_full"], s)
    kspec = kern_stage3(kb, c["g_fwd"])
    nslab = d // LANE
    z_src, z_slab0 = u, 2 * nslab
    for o in range(HY_ORDER):
        zb = sig_stage1(z_src, z_slab0, d, c["w1_half"])
        yb = sig_stage3(zb, kspec, o, c["g_fwd"], c["g_inv"])
        z_src = sig_inverse1(yb, c["v1_half"], u, o * nslab, z_src, z_slab0, nrm, bias, o)
        z_slab0 = 0
    return z_src


def hyena_layer(x, mix_norm, w_in, conv_w, conv_b, f_w1, f_b1, f_freq, f_w2, f_b2, f_w3, f_b3, f_w4, bias, w_out):
    u = norm_matmul(x, mix_norm, w_in.astype(BF16), F32)
    z = hyena_mixer(u, conv_w, conv_b, f_w1, f_b1, f_freq, f_w2, f_b2, f_w3, f_b3, f_w4, bias)
    return matmul_residual(z, w_out.astype(BF16), x, a_in_slabs=True)


def _rope_tables(seq, dim):
    inv = 1.0 / (ROPE_THETA ** (jnp.arange(0, dim, 2, dtype=F32) / dim))
    ang = jnp.arange(seq, dtype=F32)[:, None] * inv[None, :]
    return jnp.cos(ang), jnp.sin(ang)


def kernel(x, l0_mix_norm, l0_mla_w_in, l0_mla_q_norm, l0_mla_w_uq, l0_mla_kv_norm, l0_mla_w_ukv, l0_mla_w_o, l0_ffn_norm, l0_ffn_w_gate, l0_ffn_w_up, l0_ffn_w_down, l1_mix_norm, l1_hy_w_in, l1_hy_conv_w, l1_hy_conv_b, l1_hy_f_w1, l1_hy_f_b1, l1_hy_f_freq, l1_hy_f_w2, l1_hy_f_b2, l1_hy_f_w3, l1_hy_f_b3, l1_hy_f_w4, l1_hy_bias, l1_hy_w_out, l1_ffn_norm, l1_ffn_w_gate, l1_ffn_w_up, l1_ffn_w_down, l2_mix_norm, l2_da_w_qkv, l2_da_lq1, l2_da_lk1, l2_da_lq2, l2_da_lk2, l2_da_subln, l2_da_w_o, l2_ffn_norm, l2_ffn_w_gate, l2_ffn_w_up, l2_ffn_w_down, l3_mix_norm, l3_mla_w_in, l3_mla_q_norm, l3_mla_w_uq, l3_mla_kv_norm, l3_mla_w_ukv, l3_mla_w_o, l3_ffn_norm, l3_ffn_w_gate, l3_ffn_w_up, l3_ffn_w_down, final_norm):
    b, s, d = x.shape
    assert b == 1
    h = x.reshape(s, d)

    mc, ms = _rope_tables(s, MLA_ROPE)
    mla_tab = jnp.concatenate([mc, mc, ms, ms], axis=-1)
    mla = (mla_tab, mc.T, ms.T)
    dc, ds = _rope_tables(s, DIFF_HEAD_DIM)
    diff_rope = (jnp.concatenate([dc, dc], axis=-1), jnp.concatenate([-ds, ds], axis=-1))
    diff_rope_t = (dc.T, ds.T)

    def ffn(h, g, wg, wu, wd, out_norm_gain=None):
        return ffn_block(h, g, wg.astype(BF16), wu.astype(BF16), wd.astype(BF16), out_norm_gain)

    h = mla_layer(h, l0_mix_norm, l0_mla_w_in, l0_mla_q_norm, l0_mla_w_uq, l0_mla_kv_norm, l0_mla_w_ukv, l0_mla_w_o, *mla)
    h = ffn(h, l0_ffn_norm, l0_ffn_w_gate, l0_ffn_w_up, l0_ffn_w_down)
    h = hyena_layer(h, l1_mix_norm, l1_hy_w_in, l1_hy_conv_w, l1_hy_conv_b, l1_hy_f_w1, l1_hy_f_b1, l1_hy_f_freq, l1_hy_f_w2, l1_hy_f_b2, l1_hy_f_w3, l1_hy_f_b3, l1_hy_f_w4, l1_hy_bias, l1_hy_w_out)
    h = ffn(h, l1_ffn_norm, l1_ffn_w_gate, l1_ffn_w_up, l1_ffn_w_down)
    h = diff_layer(h, l2_mix_norm, l2_da_w_qkv, l2_da_lq1, l2_da_lk1, l2_da_lq2, l2_da_lk2, l2_da_subln, l2_da_w_o, diff_rope, diff_rope_t, 2)
    h = ffn(h, l2_ffn_norm, l2_ffn_w_gate, l2_ffn_w_up, l2_ffn_w_down)
    h = mla_layer(h, l3_mix_norm, l3_mla_w_in, l3_mla_q_norm, l3_mla_w_uq, l3_mla_kv_norm, l3_mla_w_ukv, l3_mla_w_o, *mla)
    h = ffn(h, l3_ffn_norm, l3_ffn_w_gate, l3_ffn_w_up, l3_ffn_w_down, final_norm)
    return h.reshape(b, s, d)
```

```python
import functools
import math

import jax
import jax.numpy as jnp
import numpy as np
from jax import lax
from jax.experimental import pallas as pl
from jax.experimental.pallas import tpu as pltpu

F32 = jnp.float32
BF16 = jnp.bfloat16

D_MODEL = 2048
SEQ = 8192
RMS_EPS = 1e-6
ROPE_THETA = 10000.0
LOG2E = math.log2(math.e)

MLA_HEADS = 16
MLA_Q_LORA = 768
MLA_KV_LORA = 512
MLA_NOPE = 128
MLA_ROPE = 64
MLA_V = 128
MLA_QK = MLA_NOPE + MLA_ROPE
MLA_SCALE = MLA_QK ** -0.5
MLA_QK_PAD = 256

HY_ORDER = 2
HY_EMB_DIM = 33
HY_TARGET = 1e-2
HY_MAX_DECAY = math.log(HY_TARGET) / 0.3
HY_MIN_DECAY = math.log(HY_TARGET) / 1.5

DIFF_HEAD_DIM = 128
DIFF_HEADS = D_MODEL // (2 * DIFF_HEAD_DIM)
DIFF_QK = DIFF_HEADS * 2 * DIFF_HEAD_DIM
DIFF_SCALE = DIFF_HEAD_DIM ** -0.5
DIFF_EPS = 1e-5

D_FF = -(-8 * D_MODEL // (3 * 256)) * 256

LANE = 128
VMEM_LIMIT = 56 * 1024 * 1024
NEG_BIG = -1e30


def _params(*sem):
    return pltpu.CompilerParams(dimension_semantics=sem, vmem_limit_bytes=VMEM_LIMIT)


def _rms(x, g, eps):
    return x * lax.rsqrt(jnp.mean(x * x, axis=-1, keepdims=True) + eps) * g


def _dot_nt(a, b):
    return lax.dot_general(a, b, (((1,), (1,)), ((), ())), preferred_element_type=F32)


def _rope_rows(y, c, s):
    h = c.shape[0]
    y1, y2 = y[:h], y[h:]
    return y1 * c - y2 * s, y2 * c + y1 * s


def _norm_matmul_kernel(x_ref, g_ref, w_ref, *rest, rope, scale):
    o_ref, xn_ref = rest[-2:]

    @pl.when(pl.program_id(1) == 0)
    def _():
        xn_ref[...] = _rms(x_ref[...], g_ref[...], RMS_EPS).astype(BF16)

    y = jnp.dot(xn_ref[...], w_ref[...], preferred_element_type=F32)
    if rope:
        c = rest[0][...] * scale
        s = rest[1][...] * scale
        for g in range(y.shape[1] // 128):
            t = y[:, g * 128:(g + 1) * 128]
            o_ref[:, g * 128:(g + 1) * 128] = (t * c + pltpu.roll(t, 64, axis=1) * s).astype(o_ref.dtype)
    else:
        o_ref[...] = (y * scale).astype(o_ref.dtype)


def norm_matmul(x, g, w, out_dtype, rope=None, scale=1.0, tm=1024, tn=1024):
    m, k = x.shape
    n = w.shape[1]
    in_specs = [
        pl.BlockSpec((tm, k), lambda i, j: (i, 0)),
        pl.BlockSpec((1, k), lambda i, j: (0, 0)),
        pl.BlockSpec((k, tn), lambda i, j: (0, j)),
    ]
    args = [x, g.reshape(1, k), w]
    if rope is not None:
        in_specs += [pl.BlockSpec((tm, 128), lambda i, j: (i, 0))] * 2
        args += list(rope)
    return pl.pallas_call(
        functools.partial(_norm_matmul_kernel, rope=rope is not None, scale=scale),
        grid=(m // tm, n // tn),
        in_specs=in_specs,
        out_specs=pl.BlockSpec((tm, tn), lambda i, j: (i, j)),
        out_shape=jax.ShapeDtypeStruct((m, n), out_dtype),
        scratch_shapes=[pltpu.VMEM((tm, k), BF16)],
        compiler_params=_params("parallel", "arbitrary"),
        name="norm_matmul",
    )(*args)


def _norm_matmul_t_kernel(x_ref, g_ref, wt_ref, *rest, rope, scale):
    o_ref, xn_ref = rest[-2:]

    @pl.when(pl.program_id(1) == 0)
    def _():
        xn_ref[...] = _rms(x_ref[...], g_ref[...], RMS_EPS).astype(BF16)

    y = _dot_nt(wt_ref[...], xn_ref[...]) * scale
    if rope:
        c = rest[0][...]
        s = rest[1][...]
        for g in range(y.shape[0] // 128):
            r1, r2 = _rope_rows(y[g * 128:(g + 1) * 128], c, s)
            o_ref[g * 128:g * 128 + 64, :] = r1.astype(o_ref.dtype)
            o_ref[g * 128 + 64:(g + 1) * 128, :] = r2.astype(o_ref.dtype)
    else:
        o_ref[...] = y.astype(o_ref.dtype)


def norm_matmul_t(x, g, wt, out_dtype, rope_t=None, scale=1.0, tm=1024, tn=1024):
    m, k = x.shape
    n = wt.shape[0]
    in_specs = [
        pl.BlockSpec((tm, k), lambda i, j: (i, 0)),
        pl.BlockSpec((1, k), lambda i, j: (0, 0)),
        pl.BlockSpec((tn, k), lambda i, j: (j, 0)),
    ]
    args = [x, g.reshape(1, k), wt]
    if rope_t is not None:
        in_specs += [pl.BlockSpec((64, tm), lambda i, j: (0, i))] * 2
        args += list(rope_t)
    return pl.pallas_call(
        functools.partial(_norm_matmul_t_kernel, rope=rope_t is not None, scale=scale),
        grid=(m // tm, n // tn),
        in_specs=in_specs,
        out_specs=pl.BlockSpec((tn, tm), lambda i, j: (j, i)),
        out_shape=jax.ShapeDtypeStruct((n, m), out_dtype),
        scratch_shapes=[pltpu.VMEM((tm, k), BF16)],
        compiler_params=_params("parallel", "arbitrary"),
        name="norm_matmul_t",
    )(*args)


def _matmul_res_kernel(a_ref, w_ref, r_ref, o_ref):
    if len(a_ref.shape) == 3:
        a = jnp.concatenate([a_ref[q] for q in range(a_ref.shape[0])], axis=1)
    else:
        a = a_ref[...]
    o_ref[...] = r_ref[...] + jnp.dot(a.astype(BF16), w_ref[...], preferred_element_type=F32)


def matmul_residual(a, w, res, a_in_slabs=False, tm=1024, tn=1024):
    k, n = w.shape
    m = res.shape[0]
    if a_in_slabs:
        a_spec = pl.BlockSpec((k // LANE, tm, LANE), lambda i, j: (0, i, 0))
    else:
        a_spec = pl.BlockSpec((tm, k), lambda i, j: (i, 0))
    return pl.pallas_call(
        _matmul_res_kernel,
        grid=(m // tm, n // tn),
        in_specs=[
            a_spec,
            pl.BlockSpec((k, tn), lambda i, j: (0, j)),
            pl.BlockSpec((tm, tn), lambda i, j: (i, j)),
        ],
        out_specs=pl.BlockSpec((tm, tn), lambda i, j: (i, j)),
        out_shape=jax.ShapeDtypeStruct((m, n), F32),
        compiler_params=_params("parallel", "parallel"),
        name="matmul_residual",
    )(a, w, res)


def _ffn_kernel(x_ref, g_ref, wg_ref, wu_ref, wd_ref, *rest, out_norm):
    o_ref, xn_ref = rest[-2:]

    @pl.when(pl.program_id(1) == 0)
    def _():
        x = x_ref[...]
        xn_ref[...] = _rms(x, g_ref[...], RMS_EPS).astype(BF16)
        o_ref[...] = x

    xn = xn_ref[...]
    a = jnp.dot(xn, wg_ref[...], preferred_element_type=F32)
    b = jnp.dot(xn, wu_ref[...], preferred_element_type=F32)
    h = (a * jax.nn.sigmoid(a) * b).astype(BF16)
    o_ref[...] += jnp.dot(h, wd_ref[...], preferred_element_type=F32)

    if out_norm:
        @pl.when(pl.program_id(1) == pl.num_programs(1) - 1)
        def _():
            o_ref[...] = _rms(o_ref[...], rest[0][...], RMS_EPS)


def ffn_block(x, g, wg, wu, wd, out_norm_gain=None, tm=1024, tf=512):
    m, d = x.shape
    f = wg.shape[1]
    vec = pl.BlockSpec((1, d), lambda i, j: (0, 0))
    in_specs = [
        pl.BlockSpec((tm, d), lambda i, j: (i, 0), pipeline_mode=pl.Buffered(1)),
        vec,
        pl.BlockSpec((d, tf), lambda i, j: (0, j)),
        pl.BlockSpec((d, tf), lambda i, j: (0, j)),
        pl.BlockSpec((tf, d), lambda i, j: (j, 0)),
    ]
    args = [x, g.reshape(1, d), wg, wu, wd]
    if out_norm_gain is not None:
        in_specs.append(vec)
        args.append(out_norm_gain.reshape(1, d))
    return pl.pallas_call(
        functools.partial(_ffn_kernel, out_norm=out_norm_gain is not None),
        grid=(m // tm, f // tf),
        in_specs=in_specs,
        out_specs=pl.BlockSpec((tm, d), lambda i, j: (i, 0)),
        out_shape=jax.ShapeDtypeStruct((m, d), F32),
        scratch_shapes=[pltpu.VMEM((tm, d), BF16)],
        compiler_params=_params("parallel", "arbitrary"),
        name="ffn_block",
    )(*args)


def _flash_t(qs, k_of_part, k_ref, vt_ref, sa_ref, sb_ref, acc_ref, *, tk):
    parts = len(qs)
    tp = qs[0].shape[1]
    nk = k_ref.shape[0] // tk
    assert nk % 2 == 0 and nk >= 4

    def chunk(c):
        return pl.ds(c * tk if isinstance(c, int) else pl.multiple_of(c * tk, tk), tk)

    def scores(c, s_ref):
        k = k_ref[chunk(c), :]
        mc = ()
        for i in range(parts):
            st = jnp.dot(k_of_part(k, i), qs[i], preferred_element_type=F32)
            s_ref[i] = st
            mc += (jnp.max(st, axis=0, keepdims=True),)
        return mc

    def softmax_pv(c, s_ref, mc, state):
        vt = vt_ref[:, chunk(c)]
        new = ()
        for i in range(parts):
            m, l = state[2 * i:2 * i + 2]
            m_new = jnp.maximum(m, mc[i])
            alpha = jnp.exp2(m - m_new)
            p = jnp.exp2(s_ref[i] - m_new)
            l = alpha * l + jnp.sum(p, axis=0, keepdims=True)
            acc_ref[i] = alpha * acc_ref[i] + jnp.dot(vt, p.astype(BF16), preferred_element_type=F32)
            new += (m_new, l)
        return new

    def body(c2, carry):
        mc_a, state = carry[:parts], carry[parts:]
        c = 2 * c2
        mc_b = scores(c + 1, sb_ref)
        state = softmax_pv(c, sa_ref, mc_a, state)
        mc_a = scores(c + 2, sa_ref)
        state = softmax_pv(c + 1, sb_ref, mc_b, state)
        return mc_a + state

    acc_ref[...] = jnp.zeros(acc_ref.shape, F32)
    state0 = (jnp.full((1, tp), NEG_BIG, F32), jnp.zeros((1, tp), F32)) * parts
    carry = lax.fori_loop(0, nk // 2 - 1, body, scores(0, sa_ref) + state0)
    mc_a, state = carry[:parts], carry[parts:]
    mc_b = scores(nk - 1, sb_ref)
    state = softmax_pv(nk - 2, sa_ref, mc_a, state)
    state = softmax_pv(nk - 1, sb_ref, mc_b, state)
    return state[1::2]


def _rope_pair(t, tab):
    t = t * tab
    return t + pltpu.roll(t, 64, axis=1)


def _mla_in_kernel(x_ref, g_ref, w_ref, qn_ref, kvn_ref, tab_ref, cq_ref, ckv_ref, kr_ref):
    xn = _rms(x_ref[...], g_ref[...], RMS_EPS).astype(BF16)
    h = jnp.dot(xn, w_ref[...], preferred_element_type=F32)
    cq_ref[...] = _rms(h[:, :MLA_Q_LORA], qn_ref[...], RMS_EPS).astype(BF16)
    ckv_ref[...] = _rms(h[:, MLA_Q_LORA:MLA_Q_LORA + MLA_KV_LORA], kvn_ref[...], RMS_EPS).astype(BF16)
    kr = _rope_pair(h[:, MLA_Q_LORA + MLA_KV_LORA:], tab_ref[...])
    lane = lax.broadcasted_iota(jnp.int32, kr.shape, 1)
    kr_ref[...] = jnp.where(lane < MLA_ROPE, kr, 0.0).astype(BF16)


def mla_in(x, g, w_in_ext, q_norm, kv_norm, tab, tm=512):
    m, d = x.shape
    n = w_in_ext.shape[1]
    row = lambda i: (i, 0)
    fixed = lambda i: (0, 0)
    return pl.pallas_call(
        _mla_in_kernel,
        grid=(m // tm,),
        in_specs=[
            pl.BlockSpec((tm, d), row),
            pl.BlockSpec((1, d), fixed),
            pl.BlockSpec((d, n), fixed),
            pl.BlockSpec((1, MLA_Q_LORA), fixed),
            pl.BlockSpec((1, MLA_KV_LORA), fixed),
            pl.BlockSpec((tm, 128), row),
        ],
        out_specs=[
            pl.BlockSpec((tm, MLA_Q_LORA), row),
            pl.BlockSpec((tm, MLA_KV_LORA), row),
            pl.BlockSpec((tm, 128), row),
        ],
        out_shape=[
            jax.ShapeDtypeStruct((m, MLA_Q_LORA), BF16),
            jax.ShapeDtypeStruct((m, MLA_KV_LORA), BF16),
            jax.ShapeDtypeStruct((m, 128), BF16),
        ],
        compiler_params=_params("parallel"),
        name="mla_in",
    )(x, g.reshape(1, d), w_in_ext, q_norm.reshape(1, -1), kv_norm.reshape(1, -1), tab)


def _mla_qt_kernel(c_ref, wt_ref, cos_ref, sin_ref, q_ref, *, heads):
    y = _dot_nt(wt_ref[...], c_ref[...]) * (MLA_SCALE * LOG2E)
    c = cos_ref[...]
    s = sin_ref[...]
    zeros = jnp.zeros((MLA_QK_PAD - MLA_QK, y.shape[1]), BF16)
    for h in range(heads):
        src = h * MLA_QK
        dst = h * MLA_QK_PAD
        q_ref[dst:dst + MLA_NOPE, :] = y[src:src + MLA_NOPE].astype(BF16)
        r1, r2 = _rope_rows(y[src + MLA_NOPE:src + MLA_QK], c, s)
        q_ref[dst + MLA_NOPE:dst + MLA_NOPE + 32, :] = r1.astype(BF16)
        q_ref[dst + MLA_NOPE + 32:dst + MLA_QK, :] = r2.astype(BF16)
        q_ref[dst + MLA_QK:dst + MLA_QK_PAD, :] = zeros


def mla_qt(cq, w_uq_t, cos_t, sin_t, tm=1024, heads_per_step=4):
    m, k = cq.shape
    tn_in = heads_per_step * MLA_QK
    tn_out = heads_per_step * MLA_QK_PAD
    return pl.pallas_call(
        functools.partial(_mla_qt_kernel, heads=heads_per_step),
        grid=(m // tm, MLA_HEADS // heads_per_step),
        in_specs=[
            pl.BlockSpec((tm, k), lambda i, j: (i, 0)),
            pl.BlockSpec((tn_in, k), lambda i, j: (j, 0)),
            pl.BlockSpec((32, tm), lambda i, j: (0, i)),
            pl.BlockSpec((32, tm), lambda i, j: (0, i)),
        ],
        out_specs=pl.BlockSpec((tn_out, tm), lambda i, j: (j, i)),
        out_shape=jax.ShapeDtypeStruct((MLA_HEADS * MLA_QK_PAD, m), BF16),
        compiler_params=_params("parallel", "parallel"),
        name="mla_qt",
    )(cq, w_uq_t, cos_t, sin_t)


def _mla_kv_kernel(c_ref, wk_ref, wvt_ref, kr_ref, k_ref, vt_ref, *, heads):
    c = c_ref[...]
    yk = jnp.dot(c, wk_ref[...], preferred_element_type=F32)
    kr = kr_ref[...]
    for h in range(heads):
        k_ref[:, h * 256:h * 256 + 128] = yk[:, h * 128:(h + 1) * 128].astype(BF16)
        k_ref[:, h * 256 + 128:h * 256 + 256] = kr
    vt_ref[...] = _dot_nt(wvt_ref[...], c).astype(BF16)


def mla_kv(ckv, w_k, w_v_t, kr, tm=1024, heads_per_step=4):
    m, k = ckv.shape
    hp = heads_per_step
    return pl.pallas_call(
        functools.partial(_mla_kv_kernel, heads=hp),
        grid=(m // tm, MLA_HEADS // hp),
        in_specs=[
            pl.BlockSpec((tm, k), lambda i, j: (i, 0)),
            pl.BlockSpec((k, hp * MLA_NOPE), lambda i, j: (0, j)),
            pl.BlockSpec((hp * MLA_V, k), lambda i, j: (j, 0)),
            pl.BlockSpec((tm, 128), lambda i, j: (i, 0)),
        ],
        out_specs=[
            pl.BlockSpec((tm, hp * MLA_QK_PAD), lambda i, j: (i, j)),
            pl.BlockSpec((hp * MLA_V, tm), lambda i, j: (j, i)),
        ],
        out_shape=[
            jax.ShapeDtypeStruct((m, MLA_HEADS * MLA_QK_PAD), BF16),
            jax.ShapeDtypeStruct((MLA_HEADS * MLA_V, m), BF16),
        ],
        compiler_params=_params("parallel", "parallel"),
        name="mla_kv",
    )(ckv, w_k, w_v_t, kr)


def _mla_attn_kernel(qt_ref, k_ref, vt_ref, o_ref, sa_ref, sb_ref, acc_ref, *, tk):
    parts, _, tp = sa_ref.shape
    qs = [qt_ref[:, i * tp:(i + 1) * tp] for i in range(parts)]
    ls = _flash_t(qs, lambda k, i: k, k_ref, vt_ref, sa_ref, sb_ref, acc_ref, tk=tk)
    for i in range(parts):
        o_ref[i * tp:(i + 1) * tp, :] = (acc_ref[i] / ls[i]).T.astype(o_ref.dtype)


def mla_attention(qt, k, vt, tq=4096, tk=512, parts=2):
    s = k.shape[0]
    return pl.pallas_call(
        functools.partial(_mla_attn_kernel, tk=tk),
        grid=(MLA_HEADS, s // tq),
        in_specs=[
            pl.BlockSpec((MLA_QK_PAD, tq), lambda h, i: (h, i)),
            pl.BlockSpec((s, MLA_QK_PAD), lambda h, i: (0, h)),
            pl.BlockSpec((MLA_V, s), lambda h, i: (h, 0)),
        ],
        out_specs=pl.BlockSpec((tq, MLA_V), lambda h, i: (i, h)),
        out_shape=jax.ShapeDtypeStruct((s, MLA_HEADS * MLA_V), BF16),
        scratch_shapes=[
            pltpu.VMEM((parts, tk, tq // parts), F32),
            pltpu.VMEM((parts, tk, tq // parts), F32),
            pltpu.VMEM((parts, MLA_V, tq // parts), F32),
        ],
        compiler_params=_params("parallel", "parallel"),
        name="mla_attention",
    )(qt, k, vt)


def _rot_cols(w):
    half = w.shape[-1] // 2
    return jnp.concatenate([-w[..., half:], w[..., :half]], axis=-1)


def mla_layer(x, mix_norm, w_in, q_norm, w_uq, kv_norm, w_ukv, w_o, tab, cos_t, sin_t):
    w_kr = w_in[:, MLA_Q_LORA + MLA_KV_LORA:]
    w_in_ext = jnp.concatenate([w_in, _rot_cols(w_kr)], axis=1).astype(BF16)
    w_uq_t = w_uq.astype(BF16).T
    wkv = w_ukv.astype(BF16).reshape(MLA_KV_LORA, MLA_HEADS, MLA_NOPE + MLA_V)
    w_k = wkv[..., :MLA_NOPE].reshape(MLA_KV_LORA, -1)
    w_v_t = wkv[..., MLA_NOPE:].reshape(MLA_KV_LORA, -1).T
    cq, ckv, kr = mla_in(x, mix_norm, w_in_ext, q_norm, kv_norm, tab)
    qt = mla_qt(cq, w_uq_t, cos_t, sin_t)
    k, vt = mla_kv(ckv, w_k, w_v_t, kr)
    o = mla_attention(qt, k, vt)
    return matmul_residual(o, w_o.astype(BF16), x)


def _diff_attn_kernel(lam_ref, qt_ref, k_ref, vt_ref, g_ref, o_ref, sa_ref, sb_ref, acc_ref, *, tk, lam_init):
    d = DIFF_HEAD_DIM
    qs = [qt_ref[:d, :], qt_ref[d:, :]]
    l0, l1 = _flash_t(qs, lambda k, i: k[:, i * d:(i + 1) * d], k_ref, vt_ref, sa_ref, sb_ref, acc_ref, tk=tk)
    ot = acc_ref[0] / l0 - lam_ref[0, 0] * (acc_ref[1] / l1)
    o = _rms(ot.T, g_ref[...], DIFF_EPS) * (1.0 - lam_init)
    o_ref[...] = o.astype(o_ref.dtype)


def diff_attention(qt, k, vt, lam, subln, lam_init, tq=2048, tk=512):
    s = k.shape[0]
    hd = 2 * DIFF_HEAD_DIM
    return pl.pallas_call(
        functools.partial(_diff_attn_kernel, tk=tk, lam_init=lam_init),
        grid=(DIFF_HEADS, s // tq),
        in_specs=[
            pl.BlockSpec(memory_space=pltpu.SMEM),
            pl.BlockSpec((hd, tq), lambda h, i: (h, i)),
            pl.BlockSpec((s, hd), lambda h, i: (0, h)),
            pl.BlockSpec((hd, s), lambda h, i: (h, 0)),
            pl.BlockSpec((1, hd), lambda h, i: (0, 0)),
        ],
        out_specs=pl.BlockSpec((tq, hd), lambda h, i: (i, h)),
        out_shape=jax.ShapeDtypeStruct((s, DIFF_HEADS * hd), BF16),
        scratch_shapes=[
            pltpu.VMEM((2, tk, tq), F32),
            pltpu.VMEM((2, tk, tq), F32),
            pltpu.VMEM((2, hd, tq), F32),
        ],
        compiler_params=_params("parallel", "parallel"),
        name="diff_attention",
    )(lam, qt, k, vt, subln.reshape(1, hd))


def _lambda_kernel(lq1_ref, lk1_ref, lq2_ref, lk2_ref, o_ref, *, lam_init):
    a = jnp.sum(lq1_ref[...] * lk1_ref[...], axis=-1, keepdims=True)
    b = jnp.sum(lq2_ref[...] * lk2_ref[...], axis=-1, keepdims=True)
    o_ref[...] = jnp.exp(a) - jnp.exp(b) + lam_init


def diff_lambda(lq1, lk1, lq2, lk2, lam_init):
    args = [v.reshape(1, -1) for v in (lq1, lk1, lq2, lk2)]
    return pl.pallas_call(
        functools.partial(_lambda_kernel, lam_init=lam_init),
        out_shape=jax.ShapeDtypeStruct((1, 1), F32),
        name="diff_lambda",
    )(*args)


def diff_layer(x, mix_norm, w_qkv, lq1, lk1, lq2, lk2, subln, w_o, rope, rope_t, layer_idx):
    lam_init = 0.8 - 0.6 * math.exp(-0.3 * layer_idx)
    w = w_qkv.astype(BF16)
    qt = norm_matmul_t(x, mix_norm, w[:, :DIFF_QK].T, BF16, rope_t=rope_t, scale=DIFF_SCALE * LOG2E)
    k = norm_matmul(x, mix_norm, w[:, DIFF_QK:2 * DIFF_QK], BF16, rope=rope)
    vt = norm_matmul_t(x, mix_norm, w[:, 2 * DIFF_QK:].T, BF16)
    lam = diff_lambda(lq1, lk1, lq2, lk2, lam_init)
    o = diff_attention(qt, k, vt, lam, subln, lam_init)
    return matmul_residual(o, w_o.astype(BF16), x)


FFT_R = 128
FFT_N = FFT_R * FFT_R
FFT_G = 8
HY_TC = 512
HY_NQ = HY_TC // LANE


def _dft_constants():
    r = np.arange(FFT_R)
    n2, k1, n1 = r[:, None, None], r[None, :, None], r[None, None, :]
    th = 2.0 * np.pi * ((k1 * (FFT_R * n1 + n2)) % FFT_N) / FFT_N
    w1 = np.concatenate([np.cos(th), -np.sin(th)], axis=1)
    tht = np.transpose(th, (0, 2, 1))
    v1 = np.concatenate([np.cos(tht), -np.sin(tht)], axis=2) / FFT_N
    ph = 2.0 * np.pi * ((r[:, None] * r[None, :]) % FFT_R) / FFT_R
    c, s = np.cos(ph), np.sin(ph)
    g_fwd = np.block([[c, s], [-s, c]])
    g_inv = np.block([[c, -s], [s, c]])
    as_bf16 = lambda a: a.astype(np.float32).astype(BF16)
    half = FFT_R // 2
    return dict(w1_full=as_bf16(w1), w1_half=as_bf16(w1[:, :, :half]), v1_half=as_bf16(v1[:, :half, :]),
                g_fwd=as_bf16(g_fwd), g_inv=as_bf16(g_inv))


def _pack_pair(x):
    h = x.shape[0] // 2
    hi = lax.bitcast_convert_type(x[:h].astype(BF16).astype(F32), jnp.uint32)
    lo = lax.bitcast_convert_type(x[h:].astype(BF16).astype(F32), jnp.uint32)
    return hi | (lo >> 16)


def _unpack_pair(p):
    hi = lax.bitcast_convert_type(p & jnp.uint32(0xFFFF0000), F32)
    lo = lax.bitcast_convert_type(p << 16, F32)
    return jnp.concatenate([hi, lo], axis=0).astype(BF16)


def _short_conv_kernel(u_ref, prev_ref, next_ref, w_ref, b_ref, o_ref):
    i = pl.program_id(0)
    x = u_ref[...]
    tm = x.shape[0]
    row = lax.broadcasted_iota(jnp.int32, x.shape, 0)
    before = jnp.where(i > 0, prev_ref[7:8, :], 0.0)
    after = jnp.where(i < pl.num_programs(0) - 1, next_ref[0:1, :], 0.0)
    up = jnp.where(row == 0, before, pltpu.roll(x, 1, axis=0))
    dn = jnp.where(row == tm - 1, after, pltpu.roll(x, tm - 1, axis=0))
    y = up * w_ref[0:1, :] + x * w_ref[1:2, :] + dn * w_ref[2:3, :] + b_ref[...]
    for q in range(o_ref.shape[0]):
        o_ref[q] = y[:, q * LANE:(q + 1) * LANE]


def short_conv(u, conv_w, conv_b, tm=1024, tc=HY_TC):
    s, n = u.shape
    nb = tm // 8
    return pl.pallas_call(
        _short_conv_kernel,
        grid=(s // tm, n // tc),
        in_specs=[
            pl.BlockSpec((tm, tc), lambda i, j: (i, j)),
            pl.BlockSpec((8, tc), lambda i, j: (jnp.maximum(i * nb - 1, 0), j)),
            pl.BlockSpec((8, tc), lambda i, j: (jnp.minimum((i + 1) * nb, s // 8 - 1), j)),
            pl.BlockSpec((3, tc), lambda i, j: (0, j)),
            pl.BlockSpec((1, tc), lambda i, j: (0, j)),
        ],
        out_specs=pl.BlockSpec((tc // LANE, tm, LANE), lambda i, j: (j, i, 0)),
        out_shape=jax.ShapeDtypeStruct((n // LANE, s, LANE), F32),
        compiler_params=_params("parallel", "parallel"),
        name="hy_short_conv",
    )(u, u, u, conv_w, conv_b.reshape(1, n))


def _filter_mlp_kernel(z_ref, w1_ref, b1_ref, f_ref, w2_ref, b2_ref, w3_ref, b3_ref, o_ref):
    hp = lax.Precision.HIGHEST
    f = f_ref[...]
    h = jnp.sin(f * (jnp.dot(z_ref[...], w1_ref[...], precision=hp, preferred_element_type=F32) + b1_ref[...]))
    h = jnp.sin(f * (jnp.dot(h, w2_ref[...], precision=hp, preferred_element_type=F32) + b2_ref[...]))
    o_ref[...] = jnp.sin(f * (jnp.dot(h, w3_ref[...], precision=hp, preferred_element_type=F32) + b3_ref[...]))


def filter_mlp(zfeat, f_w1, f_b1, f_freq, f_w2, f_b2, f_w3, f_b3, tm=2048):
    n, e = zfeat.shape
    wd = f_w2.shape[0]
    row = lambda i: (i, 0)
    fixed = lambda i: (0, 0)
    vec = lambda v: v.reshape(1, wd)
    return pl.pallas_call(
        _filter_mlp_kernel,
        grid=(n // tm,),
        in_specs=[pl.BlockSpec((tm, e), row), pl.BlockSpec((e, wd), fixed), pl.BlockSpec((1, wd), fixed),
                  pl.BlockSpec((1, wd), fixed), pl.BlockSpec((wd, wd), fixed), pl.BlockSpec((1, wd), fixed),
                  pl.BlockSpec((wd, wd), fixed), pl.BlockSpec((1, wd), fixed)],
        out_specs=pl.BlockSpec((tm, wd), row),
        out_shape=jax.ShapeDtypeStruct((n, wd), F32),
        compiler_params=_params("parallel"),
        name="hy_filter_mlp",
    )(zfeat, f_w1, vec(f_b1), vec(f_freq), f_w2, vec(f_b2), f_w3, vec(f_b3))


def _rows_by_group(ref):
    t = [jnp.swapaxes(ref[i], 0, 1) for i in range(ref.shape[0])]
    return [jnp.concatenate([ti[r] for ti in t], axis=1) for r in range(FFT_G)]


def _store_by_group(ref, vals):
    for i in range(ref.shape[0]):
        ref[i] = jnp.swapaxes(jnp.stack([v[:, i * LANE:(i + 1) * LANE] for v in vals], axis=0), 0, 1)


def _kern_stage1_kernel(h_ref, w00_ref, w01_ref, w10_ref, w11_ref, dl_ref, w1_ref, kb_ref, nrm_ref, *, seq):
    h_rows = _rows_by_group(h_ref)
    g = pl.program_id(1)
    half = FFT_R // 2
    w_fwd = (w00_ref[...], w01_ref[...])
    w_bwd = (w10_ref[...], w11_ref[...])
    rate = dl_ref[...] * (-1.0 / (seq - 1))
    n1 = lax.broadcasted_iota(jnp.int32, (half, 1), 0)
    packed = [[] for _ in range(HY_ORDER)]
    acc = [jnp.zeros(nrm_ref.shape[1:], F32) for _ in range(HY_ORDER)]
    for r in range(FFT_G):
        n2 = g * FFT_G + r
        hs = h_rows[r].astype(BF16)
        lag_f = FFT_R * n1 + n2
        lag_b = seq - FFT_R * n1 - n2
        decay_f = jnp.exp(lag_f.astype(F32) * rate)
        decay_b = jnp.where(lag_b == seq, 0.0, jnp.exp(lag_b.astype(F32) * rate))
        for o in range(HY_ORDER):
            kf = jnp.dot(hs[:half], w_fwd[o], preferred_element_type=F32) * decay_f
            kb = jnp.dot(hs[half:], w_bwd[o], preferred_element_type=F32) * decay_b
            slab = jnp.concatenate([kf, kb], axis=0)
            acc[o] += jnp.sum(jnp.abs(slab), axis=0, keepdims=True)
            packed[o].append(_pack_pair(jnp.dot(w1_ref[r], slab.astype(BF16), preferred_element_type=F32)))
    for o in range(HY_ORDER):
        _store_by_group(kb_ref.at[o], packed[o])

    @pl.when(g == 0)
    def _():
        for o in range(HY_ORDER):
            nrm_ref[o] = acc[o]

    @pl.when(g > 0)
    def _():
        for o in range(HY_ORDER):
            nrm_ref[o] += acc[o]


def kern_stage1(h3, w4, deltas, w1_full, seq, tc=HY_TC):
    d = deltas.shape[0]
    jn = d // tc
    h3v = h3.reshape(1, FFT_R, FFT_R, LANE)
    assert HY_ORDER == 2
    kern = functools.partial(_kern_stage1_kernel, seq=seq)
    w4_spec = lambda direction, order: pl.BlockSpec((LANE, tc), lambda j, g: (0, (direction * HY_ORDER + order) * jn + j))
    return pl.pallas_call(
        kern,
        grid=(jn, FFT_R // FFT_G),
        in_specs=[
            pl.BlockSpec((1, FFT_R, FFT_G, LANE), lambda j, g: (0, 0, g, 0)),
            w4_spec(0, 0), w4_spec(0, 1), w4_spec(1, 0), w4_spec(1, 1),
            pl.BlockSpec((1, tc), lambda j, g: (0, j)),
            pl.BlockSpec((FFT_G, 2 * FFT_R, FFT_R), lambda j, g: (g, 0, 0)),
        ],
        out_specs=[
            pl.BlockSpec((HY_ORDER, HY_NQ, FFT_R, FFT_G, LANE), lambda j, g: (0, j, 0, g, 0)),
            pl.BlockSpec((HY_ORDER, 1, tc), lambda j, g: (0, 0, j)),
        ],
        out_shape=[
            jax.ShapeDtypeStruct((HY_ORDER, d // LANE, FFT_R, FFT_R, LANE), jnp.uint32),
            jax.ShapeDtypeStruct((HY_ORDER, 1, d), F32),
        ],
        compiler_params=_params("parallel", "arbitrary"),
        name="hy_kern_stage1",
    )(h3v, w4, w4, w4, w4, deltas.reshape(1, d), w1_full)


def _join_lanes(ref, idx):
    return jnp.concatenate([ref[(q,) + idx] for q in range(ref.shape[0])], axis=1)


def _kern_stage3_kernel(kb_ref, g_ref, o_ref):
    gm = g_ref[...]
    for r in range(FFT_G):
        b = _unpack_pair(_join_lanes(kb_ref.at[0], (r,)))
        o_ref[0, r] = jnp.dot(gm, b, preferred_element_type=F32).astype(o_ref.dtype)


def kern_stage3(kb, g_fwd, tc=HY_TC):
    d = kb.shape[1] * LANE
    return pl.pallas_call(
        _kern_stage3_kernel,
        grid=(HY_ORDER, FFT_R // FFT_G, d // tc),
        in_specs=[
            pl.BlockSpec((1, HY_NQ, FFT_G, FFT_R, LANE), lambda o, kg, j: (o, j, kg, 0, 0)),
            pl.BlockSpec((2 * FFT_R, 2 * FFT_R), lambda o, kg, j: (0, 0)),
        ],
        out_specs=pl.BlockSpec((1, FFT_G, 2 * FFT_R, tc), lambda o, kg, j: (o, kg, 0, j)),
        out_shape=jax.ShapeDtypeStruct((HY_ORDER, FFT_R, 2 * FFT_R, d), BF16),
        compiler_params=_params("parallel", "parallel", "parallel"),
        name="hy_kern_stage3",
    )(kb, g_fwd)


def _sig_stage1_kernel(z_ref, w1_ref, o_ref):
    z_rows = _rows_by_group(z_ref)
    _store_by_group(o_ref, [_pack_pair(jnp.dot(w1_ref[r], z_rows[r].astype(BF16), preferred_element_type=F32))
                            for r in range(FFT_G)])


def sig_stage1(z_src, slab0, d, w1_half, tc=HY_TC):
    s = z_src.shape[1]
    rows = s // FFT_R
    zv = z_src.reshape(z_src.shape[0], rows, FFT_R, LANE)
    return pl.pallas_call(
        _sig_stage1_kernel,
        grid=(FFT_R // FFT_G, d // tc),
        in_specs=[
            pl.BlockSpec((HY_NQ, rows, FFT_G, LANE), lambda g, j: (slab0 // HY_NQ + j, 0, g, 0)),
            pl.BlockSpec((FFT_G, 2 * FFT_R, rows), lambda g, j: (g, 0, 0)),
        ],
        out_specs=pl.BlockSpec((HY_NQ, FFT_R, FFT_G, LANE), lambda g, j: (j, 0, g, 0)),
        out_shape=jax.ShapeDtypeStruct((d // LANE, FFT_R, FFT_R, LANE), jnp.uint32),
        compiler_params=_params("parallel", "parallel"),
        name="hy_sig_stage1",
    )(zv, w1_half)


def _sig_stage3_kernel(zb_ref, ks_ref, gf_ref, gi_ref, o_ref):
    gf = gf_ref[...]
    gi = gi_ref[...]
    for r in range(FFT_G):
        x = jnp.dot(gf, _unpack_pair(_join_lanes(zb_ref, (r,))), preferred_element_type=F32)
        ks = ks_ref[0, r].astype(F32)
        xr, xi = x[:FFT_R], x[FFT_R:]
        kr, ki = ks[:FFT_R], ks[FFT_R:]
        y = jnp.concatenate([xr * kr - xi * ki, xr * ki + xi * kr], axis=0).astype(BF16)
        packed = _pack_pair(jnp.dot(gi, y, preferred_element_type=F32))
        for q in range(o_ref.shape[0]):
            o_ref[q, r] = packed[:, q * LANE:(q + 1) * LANE]


def sig_stage3(zb, kspec, order, g_fwd, g_inv, tc=HY_TC):
    d = zb.shape[0] * LANE
    blk = pl.BlockSpec((HY_NQ, FFT_G, FFT_R, LANE), lambda kg, j: (j, kg, 0, 0))
    mat = pl.BlockSpec((2 * FFT_R, 2 * FFT_R), lambda kg, j: (0, 0))
    return pl.pallas_call(
        _sig_stage3_kernel,
        grid=(FFT_R // FFT_G, d // tc),
        in_specs=[blk, pl.BlockSpec((1, FFT_G, 2 * FFT_R, tc), lambda kg, j: (order, kg, 0, j)), mat, mat],
        out_specs=blk,
        out_shape=jax.ShapeDtypeStruct(zb.shape, jnp.uint32),
        compiler_params=_params("parallel", "parallel"),
        name="hy_sig_stage3",
    )(zb, kspec, g_fwd, g_inv)


def _sig_inverse1_kernel(yb_ref, v_ref, gate_ref, z_ref, nrm_ref, bias_ref, o_ref):
    inv = 1.0 / nrm_ref[0]
    bias = bias_ref[0]
    yb_rows, gate_rows, z_rows = _rows_by_group(yb_ref), _rows_by_group(gate_ref), _rows_by_group(z_ref)
    out = []
    for r in range(FFT_G):
        conv = jnp.dot(v_ref[r], _unpack_pair(yb_rows[r]), preferred_element_type=F32) * inv
        out.append(gate_rows[r] * (conv + z_rows[r] * bias))
    _store_by_group(o_ref, out)


def sig_inverse1(yb, v1_half, gate_src, gate_slab0, z_src, z_slab0, nrm, bias, order, tc=HY_TC):
    d = yb.shape[0] * LANE
    s = gate_src.shape[1]
    rows = s // FFT_R
    view = lambda a: a.reshape(a.shape[0], rows, FFT_R, LANE)
    sig = (HY_NQ, rows, FFT_G, LANE)
    return pl.pallas_call(
        _sig_inverse1_kernel,
        grid=(FFT_R // FFT_G, d // tc),
        in_specs=[
            pl.BlockSpec((HY_NQ, FFT_R, FFT_G, LANE), lambda g, j: (j, 0, g, 0)),
            pl.BlockSpec((FFT_G, rows, 2 * FFT_R), lambda g, j: (g, 0, 0)),
            pl.BlockSpec(sig, lambda g, j: (gate_slab0 // HY_NQ + j, 0, g, 0)),
            pl.BlockSpec(sig, lambda g, j: (z_slab0 // HY_NQ + j, 0, g, 0)),
            pl.BlockSpec((1, 1, tc), lambda g, j: (order, 0, j)),
            pl.BlockSpec((1, 1, tc), lambda g, j: (order, 0, j)),
        ],
        out_specs=pl.BlockSpec(sig, lambda g, j: (j, 0, g, 0)),
        out_shape=jax.ShapeDtypeStruct((d // LANE, rows, FFT_R, LANE), F32),
        compiler_params=_params("parallel", "parallel"),
        name="hy_sig_inverse1",
    )(yb, v1_half, view(gate_src), view(z_src), nrm, bias.reshape(HY_ORDER, 1, d)).reshape(d // LANE, s, LANE)


def _hyena_positions(seq):
    t = jnp.linspace(0.0, 1.0, seq, dtype=F32)[:, None]
    bands = (HY_EMB_DIM - 1) // 2
    w = 2.0 * math.pi * jnp.arange(seq, dtype=F32)[:, None] / seq
    f = jnp.linspace(1e-4, bands - 1, bands, dtype=F32)[None, :]
    z = jnp.concatenate([t, jnp.cos(f * w), -jnp.sin(f * w)], axis=-1)
    return jnp.pad(z, ((0, 0), (0, LANE - HY_EMB_DIM)))


def hyena_mixer(u, conv_w, conv_b, f_w1, f_b1, f_freq, f_w2, f_b2, f_w3, f_b3, f_w4, bias):
    s = u.shape[0]
    d = u.shape[1] // 3
    assert 2 * s == FFT_N and d % HY_TC == 0
    c = _dft_constants()
    u = short_conv(u, conv_w, conv_b)
    h3 = filter_mlp(_hyena_positions(s), jnp.pad(f_w1, ((0, LANE - HY_EMB_DIM), (0, 0))), f_b1, f_freq, f_w2, f_b2, f_w3, f_b3)
    width = h3.shape[1]
    h3 = jnp.concatenate([h3, h3[:1], h3[:0:-1]], axis=0)
    h3 = jnp.pad(h3, ((0, 0), (0, LANE - width)))
    w4 = jnp.pad(f_w4.astype(BF16), ((0, LANE - width), (0, 0)))
    deltas = jnp.abs(jnp.linspace(HY_MIN_DECAY, HY_MAX_DECAY, d, dtype=F32))
    kb, nrm = kern_stage1(h3, w4, deltas, c["w1_full"], s)
    kspec = kern_stage3(kb, c["g_fwd"])
    nslab = d // LANE
    z_src, z_slab0 = u, 2 * nslab
    for o in range(HY_ORDER):
        zb = sig_stage1(z_src, z_slab0, d, c["w1_half"])
        yb = sig_stage3(zb, kspec, o, c["g_fwd"], c["g_inv"])
        z_src = sig_inverse1(yb, c["v1_half"], u, o * nslab, z_src, z_slab0, nrm, bias, o)
        z_slab0 = 0
    return z_src


def hyena_layer(x, mix_norm, w_in, conv_w, conv_b, f_w1, f_b1, f_freq, f_w2, f_b2, f_w3, f_b3, f_w4, bias, w_out):
    u = norm_matmul(x, mix_norm, w_in.astype(BF16), F32)
    z = hyena_mixer(u, conv_w, conv_b, f_w1, f_b1, f_freq, f_w2, f_b2, f_w3, f_b3, f_w4, bias)
    return matmul_residual(z, w_out.astype(BF16), x, a_in_slabs=True)


def _rope_tables(seq, dim):
    inv = 1.0 / (ROPE_THETA ** (jnp.arange(0, dim, 2, dtype=F32) / dim))
    ang = jnp.arange(seq, dtype=F32)[:, None] * inv[None, :]
    return jnp.cos(ang), jnp.sin(ang)


def kernel(x, l0_mix_norm, l0_mla_w_in, l0_mla_q_norm, l0_mla_w_uq, l0_mla_kv_norm, l0_mla_w_ukv, l0_mla_w_o, l0_ffn_norm, l0_ffn_w_gate, l0_ffn_w_up, l0_ffn_w_down, l1_mix_norm, l1_hy_w_in, l1_hy_conv_w, l1_hy_conv_b, l1_hy_f_w1, l1_hy_f_b1, l1_hy_f_freq, l1_hy_f_w2, l1_hy_f_b2, l1_hy_f_w3, l1_hy_f_b3, l1_hy_f_w4, l1_hy_bias, l1_hy_w_out, l1_ffn_norm, l1_ffn_w_gate, l1_ffn_w_up, l1_ffn_w_down, l2_mix_norm, l2_da_w_qkv, l2_da_lq1, l2_da_lk1, l2_da_lq2, l2_da_lk2, l2_da_subln, l2_da_w_o, l2_ffn_norm, l2_ffn_w_gate, l2_ffn_w_up, l2_ffn_w_down, l3_mix_norm, l3_mla_w_in, l3_mla_q_norm, l3_mla_w_uq, l3_mla_kv_norm, l3_mla_w_ukv, l3_mla_w_o, l3_ffn_norm, l3_ffn_w_gate, l3_ffn_w_up, l3_ffn_w_down, final_norm):
    b, s, d = x.shape
    assert b == 1
    h = x.reshape(s, d)

    mc, ms = _rope_tables(s, MLA_ROPE)
    mla_tab = jnp.concatenate([mc, mc, ms, ms], axis=-1)
    mla = (mla_tab, mc.T, ms.T)
    dc, ds = _rope_tables(s, DIFF_HEAD_DIM)
    diff_rope = (jnp.concatenate([dc, dc], axis=-1), jnp.concatenate([-ds, ds], axis=-1))
    diff_rope_t = (dc.T, ds.T)

    def ffn(h, g, wg, wu, wd, out_norm_gain=None):
        return ffn_block(h, g, wg.astype(BF16), wu.astype(BF16), wd.astype(BF16), out_norm_gain)

    h = mla_layer(h, l0_mix_norm, l0_mla_w_in, l0_mla_q_norm, l0_mla_w_uq, l0_mla_kv_norm, l0_mla_w_ukv, l0_mla_w_o, *mla)
    h = ffn(h, l0_ffn_norm, l0_ffn_w_gate, l0_ffn_w_up, l0_ffn_w_down)
    h = hyena_layer(h, l1_mix_norm, l1_hy_w_in, l1_hy_conv_w, l1_hy_conv_b, l1_hy_f_w1, l1_hy_f_b1, l1_hy_f_freq, l1_hy_f_w2, l1_hy_f_b2, l1_hy_f_w3, l1_hy_f_b3, l1_hy_f_w4, l1_hy_bias, l1_hy_w_out)
    h = ffn(h, l1_ffn_norm, l1_ffn_w_gate, l1_ffn_w_up, l1_ffn_w_down)
    h = diff_layer(h, l2_mix_norm, l2_da_w_qkv, l2_da_lq1, l2_da_lk1, l2_da_lq2, l2_da_lk2, l2_da_subln, l2_da_w_o, diff_rope, diff_rope_t, 2)
    h = ffn(h, l2_ffn_norm, l2_ffn_w_gate, l2_ffn_w_up, l2_ffn_w_down)
    h = mla_layer(h, l3_mix_norm, l3_mla_w_in, l3_mla_q_norm, l3_mla_w_uq, l3_mla_kv_norm, l3_mla_w_ukv, l3_mla_w_o, *mla)
    h = ffn(h, l3_ffn_norm, l3_ffn_w_gate, l3_ffn_w_up, l3_ffn_w_down, final_norm)
    return h.reshape(b, s, d)
```

```python
import functools
import math

import jax
import jax.numpy as jnp
import numpy as np
from jax import lax
from jax.experimental import pallas as pl
from jax.experimental.pallas import tpu as pltpu

F32 = jnp.float32
BF16 = jnp.bfloat16

D_MODEL = 2048
SEQ = 8192
RMS_EPS = 1e-6
ROPE_THETA = 10000.0
LOG2E = math.log2(math.e)

MLA_HEADS = 16
MLA_Q_LORA = 768
MLA_KV_LORA = 512
MLA_NOPE = 128
MLA_ROPE = 64
MLA_V = 128
MLA_QK = MLA_NOPE + MLA_ROPE
MLA_SCALE = MLA_QK ** -0.5
MLA_QK_PAD = 256

HY_ORDER = 2
HY_EMB_DIM = 33
HY_TARGET = 1e-2
HY_MAX_DECAY = math.log(HY_TARGET) / 0.3
HY_MIN_DECAY = math.log(HY_TARGET) / 1.5

DIFF_HEAD_DIM = 128
DIFF_HEADS = D_MODEL // (2 * DIFF_HEAD_DIM)
DIFF_QK = DIFF_HEADS * 2 * DIFF_HEAD_DIM
DIFF_SCALE = DIFF_HEAD_DIM ** -0.5
DIFF_EPS = 1e-5

D_FF = -(-8 * D_MODEL // (3 * 256)) * 256

LANE = 128
VMEM_LIMIT = 56 * 1024 * 1024
NEG_BIG = -1e30


def _params(*sem):
    return pltpu.CompilerParams(dimension_semantics=sem, vmem_limit_bytes=VMEM_LIMIT)


def _rms(x, g, eps):
    return x * lax.rsqrt(jnp.mean(x * x, axis=-1, keepdims=True) + eps) * g


def _dot_nt(a, b):
    return lax.dot_general(a, b, (((1,), (1,)), ((), ())), preferred_element_type=F32)


def _rope_rows(y, c, s):
    h = c.shape[0]
    y1, y2 = y[:h], y[h:]
    return y1 * c - y2 * s, y2 * c + y1 * s


def _norm_matmul_kernel(x_ref, g_ref, w_ref, *rest, rope, scale):
    o_ref, xn_ref = rest[-2:]

    @pl.when(pl.program_id(1) == 0)
    def _():
        xn_ref[...] = _rms(x_ref[...], g_ref[...], RMS_EPS).astype(BF16)

    y = jnp.dot(xn_ref[...], w_ref[...], preferred_element_type=F32)
    if rope:
        c = rest[0][...] * scale
        s = rest[1][...] * scale
        for g in range(y.shape[1] // 128):
            t = y[:, g * 128:(g + 1) * 128]
            o_ref[:, g * 128:(g + 1) * 128] = (t * c + pltpu.roll(t, 64, axis=1) * s).astype(o_ref.dtype)
    else:
        o_ref[...] = (y * scale).astype(o_ref.dtype)


def norm_matmul(x, g, w, out_dtype, rope=None, scale=1.0, tm=1024, tn=1024):
    m, k = x.shape
    n = w.shape[1]
    in_specs = [
        pl.BlockSpec((tm, k), lambda i, j: (i, 0)),
        pl.BlockSpec((1, k), lambda i, j: (0, 0)),
        pl.BlockSpec((k, tn), lambda i, j: (0, j)),
    ]
    args = [x, g.reshape(1, k), w]
    if rope is not None:
        in_specs += [pl.BlockSpec((tm, 128), lambda i, j: (i, 0))] * 2
        args += list(rope)
    return pl.pallas_call(
        functools.partial(_norm_matmul_kernel, rope=rope is not None, scale=scale),
        grid=(m // tm, n // tn),
        in_specs=in_specs,
        out_specs=pl.BlockSpec((tm, tn), lambda i, j: (i, j)),
        out_shape=jax.ShapeDtypeStruct((m, n), out_dtype),
        scratch_shapes=[pltpu.VMEM((tm, k), BF16)],
        compiler_params=_params("parallel", "arbitrary"),
        name="norm_matmul",
    )(*args)


def _norm_matmul_t_kernel(x_ref, g_ref, wt_ref, *rest, rope, scale):
    o_ref, xn_ref = rest[-2:]

    @pl.when(pl.program_id(1) == 0)
    def _():
        xn_ref[...] = _rms(x_ref[...], g_ref[...], RMS_EPS).astype(BF16)

    y = _dot_nt(wt_ref[...], xn_ref[...]) * scale
    if rope:
        c = rest[0][...]
        s = rest[1][...]
        for g in range(y.shape[0] // 128):
            r1, r2 = _rope_rows(y[g * 128:(g + 1) * 128], c, s)
            o_ref[g * 128:g * 128 + 64, :] = r1.astype(o_ref.dtype)
            o_ref[g * 128 + 64:(g + 1) * 128, :] = r2.astype(o_ref.dtype)
    else:
        o_ref[...] = y.astype(o_ref.dtype)


def norm_matmul_t(x, g, wt, out_dtype, rope_t=None, scale=1.0, tm=1024, tn=1024):
    m, k = x.shape
    n = wt.shape[0]
    in_specs = [
        pl.BlockSpec((tm, k), lambda i, j: (i, 0)),
        pl.BlockSpec((1, k), lambda i, j: (0, 0)),
        pl.BlockSpec((tn, k), lambda i, j: (j, 0)),
    ]
    args = [x, g.reshape(1, k), wt]
    if rope_t is not None:
        in_specs += [pl.BlockSpec((64, tm), lambda i, j: (0, i))] * 2
        args += list(rope_t)
    return pl.pallas_call(
        functools.partial(_norm_matmul_t_kernel, rope=rope_t is not None, scale=scale),
        grid=(m // tm, n // tn),
        in_specs=in_specs,
        out_specs=pl.BlockSpec((tn, tm), lambda i, j: (j, i)),
        out_shape=jax.ShapeDtypeStruct((n, m), out_dtype),
        scratch_shapes=[pltpu.VMEM((tm, k), BF16)],
        compiler_params=_params("parallel", "arbitrary"),
        name="norm_matmul_t",
    )(*args)


def _matmul_res_kernel(a_ref, w_ref, r_ref, o_ref):
    if len(a_ref.shape) == 3:
        a = jnp.concatenate([a_ref[q] for q in range(a_ref.shape[0])], axis=1)
    else:
        a = a_ref[...]
    o_ref[...] = r_ref[...] + jnp.dot(a.astype(BF16), w_ref[...], preferred_element_type=F32)


def matmul_residual(a, w, res, a_in_slabs=False, tm=1024, tn=1024):
    k, n = w.shape
    m = res.shape[0]
    if a_in_slabs:
        a_spec = pl.BlockSpec((k // LANE, tm, LANE), lambda i, j: (0, i, 0))
    else:
        a_spec = pl.BlockSpec((tm, k), lambda i, j: (i, 0))
    return pl.pallas_call(
        _matmul_res_kernel,
        grid=(m // tm, n // tn),
        in_specs=[
            a_spec,
            pl.BlockSpec((k, tn), lambda i, j: (0, j)),
            pl.BlockSpec((tm, tn), lambda i, j: (i, j)),
        ],
        out_specs=pl.BlockSpec((tm, tn), lambda i, j: (i, j)),
        out_shape=jax.ShapeDtypeStruct((m, n), F32),
        compiler_params=_params("parallel", "parallel"),
        name="matmul_residual",
    )(a, w, res)


def _ffn_kernel(x_ref, g_ref, wg_ref, wu_ref, wd_ref, *rest, out_norm):
    o_ref, xn_ref = rest[-2:]

    @pl.when(pl.program_id(1) == 0)
    def _():
        x = x_ref[...]
        xn_ref[...] = _rms(x, g_ref[...], RMS_EPS).astype(BF16)
        o_ref[...] = x

    xn = xn_ref[...]
    a = jnp.dot(xn, wg_ref[...], preferred_element_type=F32)
    b = jnp.dot(xn, wu_ref[...], preferred_element_type=F32)
    h = (a * jax.nn.sigmoid(a) * b).astype(BF16)
    o_ref[...] += jnp.dot(h, wd_ref[...], preferred_element_type=F32)

    if out_norm:
        @pl.when(pl.program_id(1) == pl.num_programs(1) - 1)
        def _():
            o_ref[...] = _rms(o_ref[...], rest[0][...], RMS_EPS)


def ffn_block(x, g, wg, wu, wd, out_norm_gain=None, tm=1024, tf=512):
    m, d = x.shape
    f = wg.shape[1]
    vec = pl.BlockSpec((1, d), lambda i, j: (0, 0))
    in_specs = [
        pl.BlockSpec((tm, d), lambda i, j: (i, 0)),
        vec,
        pl.BlockSpec((d, tf), lambda i, j: (0, j)),
        pl.BlockSpec((d, tf), lambda i, j: (0, j)),
        pl.BlockSpec((tf, d), lambda i, j: (j, 0)),
    ]
    args = [x, g.reshape(1, d), wg, wu, wd]
    if out_norm_gain is not None:
        in_specs.append(vec)
        args.append(out_norm_gain.reshape(1, d))
    return pl.pallas_call(
        functools.partial(_ffn_kernel, out_norm=out_norm_gain is not None),
        grid=(m // tm, f // tf),
        in_specs=in_specs,
        out_specs=pl.BlockSpec((tm, d), lambda i, j: (i, 0)),
        out_shape=jax.ShapeDtypeStruct((m, d), F32),
        scratch_shapes=[pltpu.VMEM((tm, d), BF16)],
        compiler_params=_params("parallel", "arbitrary"),
        name="ffn_block",
    )(*args)


def _flash_t(qs, k_of_part, k_ref, vt_ref, sa_ref, sb_ref, acc_ref, *, tk):
    parts = len(qs)
    tp = qs[0].shape[1]
    nk = k_ref.shape[0] // tk
    assert nk % 2 == 0 and nk >= 4

    def chunk(c):
        return pl.ds(c * tk if isinstance(c, int) else pl.multiple_of(c * tk, tk), tk)

    def scores(c, s_ref):
        k = k_ref[chunk(c), :]
        mc = ()
        for i in range(parts):
            st = jnp.dot(k_of_part(k, i), qs[i], preferred_element_type=F32)
            s_ref[i] = st
            mc += (jnp.max(st, axis=0, keepdims=True),)
        return mc

    def softmax_pv(c, s_ref, mc, state):
        vt = vt_ref[:, chunk(c)]
        new = ()
        for i in range(parts):
            m, l = state[2 * i:2 * i + 2]
            m_new = jnp.maximum(m, mc[i])
            alpha = jnp.exp2(m - m_new)
            p = jnp.exp2(s_ref[i] - m_new)
            l = alpha * l + jnp.sum(p, axis=0, keepdims=True)
            acc_ref[i] = alpha * acc_ref[i] + jnp.dot(vt, p.astype(BF16), preferred_element_type=F32)
            new += (m_new, l)
        return new

    def body(c2, carry):
        mc_a, state = carry[:parts], carry[parts:]
        c = 2 * c2
        mc_b = scores(c + 1, sb_ref)
        state = softmax_pv(c, sa_ref, mc_a, state)
        mc_a = scores(c + 2, sa_ref)
        state = softmax_pv(c + 1, sb_ref, mc_b, state)
        return mc_a + state

    acc_ref[...] = jnp.zeros(acc_ref.shape, F32)
    state0 = (jnp.full((1, tp), NEG_BIG, F32), jnp.zeros((1, tp), F32)) * parts
    carry = lax.fori_loop(0, nk // 2 - 1, body, scores(0, sa_ref) + state0)
    mc_a, state = carry[:parts], carry[parts:]
    mc_b = scores(nk - 1, sb_ref)
    state = softmax_pv(nk - 2, sa_ref, mc_a, state)
    state = softmax_pv(nk - 1, sb_ref, mc_b, state)
    return state[1::2]


def _rope_pair(t, tab):
    t = t * tab
    return t + pltpu.roll(t, 64, axis=1)


def _mla_in_kernel(x_ref, g_ref, w_ref, qn_ref, kvn_ref, tab_ref, cq_ref, ckv_ref, kr_ref):
    xn = _rms(x_ref[...], g_ref[...], RMS_EPS).astype(BF16)
    h = jnp.dot(xn, w_ref[...], preferred_element_type=F32)
    cq_ref[...] = _rms(h[:, :MLA_Q_LORA], qn_ref[...], RMS_EPS).astype(BF16)
    ckv_ref[...] = _rms(h[:, MLA_Q_LORA:MLA_Q_LORA + MLA_KV_LORA], kvn_ref[...], RMS_EPS).astype(BF16)
    kr = _rope_pair(h[:, MLA_Q_LORA + MLA_KV_LORA:], tab_ref[...])
    lane = lax.broadcasted_iota(jnp.int32, kr.shape, 1)
    kr_ref[...] = jnp.where(lane < MLA_ROPE, kr, 0.0).astype(BF16)


def mla_in(x, g, w_in_ext, q_norm, kv_norm, tab, tm=512):
    m, d = x.shape
    n = w_in_ext.shape[1]
    row = lambda i: (i, 0)
    fixed = lambda i: (0, 0)
    return pl.pallas_call(
        _mla_in_kernel,
        grid=(m // tm,),
        in_specs=[
            pl.BlockSpec((tm, d), row),
            pl.BlockSpec((1, d), fixed),
            pl.BlockSpec((d, n), fixed),
            pl.BlockSpec((1, MLA_Q_LORA), fixed),
            pl.BlockSpec((1, MLA_KV_LORA), fixed),
            pl.BlockSpec((tm, 128), row),
        ],
        out_specs=[
            pl.BlockSpec((tm, MLA_Q_LORA), row),
            pl.BlockSpec((tm, MLA_KV_LORA), row),
            pl.BlockSpec((tm, 128), row),
        ],
        out_shape=[
            jax.ShapeDtypeStruct((m, MLA_Q_LORA), BF16),
            jax.ShapeDtypeStruct((m, MLA_KV_LORA), BF16),
            jax.ShapeDtypeStruct((m, 128), BF16),
        ],
        compiler_params=_params("parallel"),
        name="mla_in",
    )(x, g.reshape(1, d), w_in_ext, q_norm.reshape(1, -1), kv_norm.reshape(1, -1), tab)


def _mla_qt_kernel(c_ref, wt_ref, cos_ref, sin_ref, q_ref, *, heads):
    y = _dot_nt(wt_ref[...], c_ref[...]) * (MLA_SCALE * LOG2E)
    c = cos_ref[...]
    s = sin_ref[...]
    zeros = jnp.zeros((MLA_QK_PAD - MLA_QK, y.shape[1]), BF16)
    for h in range(heads):
        src = h * MLA_QK
        dst = h * MLA_QK_PAD
        q_ref[dst:dst + MLA_NOPE, :] = y[src:src + MLA_NOPE].astype(BF16)
        r1, r2 = _rope_rows(y[src + MLA_NOPE:src + MLA_QK], c, s)
        q_ref[dst + MLA_NOPE:dst + MLA_NOPE + 32, :] = r1.astype(BF16)
        q_ref[dst + MLA_NOPE + 32:dst + MLA_QK, :] = r2.astype(BF16)
        q_ref[dst + MLA_QK:dst + MLA_QK_PAD, :] = zeros


def mla_qt(cq, w_uq_t, cos_t, sin_t, tm=1024, heads_per_step=4):
    m, k = cq.shape
    tn_in = heads_per_step * MLA_QK
    tn_out = heads_per_step * MLA_QK_PAD
    return pl.pallas_call(
        functools.partial(_mla_qt_kernel, heads=heads_per_step),
        grid=(m // tm, MLA_HEADS // heads_per_step),
        in_specs=[
            pl.BlockSpec((tm, k), lambda i, j: (i, 0)),
            pl.BlockSpec((tn_in, k), lambda i, j: (j, 0)),
            pl.BlockSpec((32, tm), lambda i, j: (0, i)),
            pl.BlockSpec((32, tm), lambda i, j: (0, i)),
        ],
        out_specs=pl.BlockSpec((tn_out, tm), lambda i, j: (j, i)),
        out_shape=jax.ShapeDtypeStruct((MLA_HEADS * MLA_QK_PAD, m), BF16),
        compiler_params=_params("parallel", "parallel"),
        name="mla_qt",
    )(cq, w_uq_t, cos_t, sin_t)


def _mla_kv_kernel(c_ref, wk_ref, wvt_ref, kr_ref, k_ref, vt_ref, *, heads):
    c = c_ref[...]
    yk = jnp.dot(c, wk_ref[...], preferred_element_type=F32)
    kr = kr_ref[...]
    for h in range(heads):
        k_ref[:, h * 256:h * 256 + 128] = yk[:, h * 128:(h + 1) * 128].astype(BF16)
        k_ref[:, h * 256 + 128:h * 256 + 256] = kr
    vt_ref[...] = _dot_nt(wvt_ref[...], c).astype(BF16)


def mla_kv(ckv, w_k, w_v_t, kr, tm=1024, heads_per_step=4):
    m, k = ckv.shape
    hp = heads_per_step
    return pl.pallas_call(
        functools.partial(_mla_kv_kernel, heads=hp),
        grid=(m // tm, MLA_HEADS // hp),
        in_specs=[
            pl.BlockSpec((tm, k), lambda i, j: (i, 0)),
            pl.BlockSpec((k, hp * MLA_NOPE), lambda i, j: (0, j)),
            pl.BlockSpec((hp * MLA_V, k), lambda i, j: (j, 0)),
            pl.BlockSpec((tm, 128), lambda i, j: (i, 0)),
        ],
        out_specs=[
            pl.BlockSpec((tm, hp * MLA_QK_PAD), lambda i, j: (i, j)),
            pl.BlockSpec((hp * MLA_V, tm), lambda i, j: (j, i)),
        ],
        out_shape=[
            jax.ShapeDtypeStruct((m, MLA_HEADS * MLA_QK_PAD), BF16),
            jax.ShapeDtypeStruct((MLA_HEADS * MLA_V, m), BF16),
        ],
        compiler_params=_params("parallel", "parallel"),
        name="mla_kv",
    )(ckv, w_k, w_v_t, kr)


def _mla_attn_kernel(qt_ref, k_ref, vt_ref, o_ref, sa_ref, sb_ref, acc_ref, *, tk):
    parts, _, tp = sa_ref.shape
    qs = [qt_ref[:, i * tp:(i + 1) * tp] for i in range(parts)]
    ls = _flash_t(qs, lambda k, i: k, k_ref, vt_ref, sa_ref, sb_ref, acc_ref, tk=tk)
    for i in range(parts):
        o_ref[i * tp:(i + 1) * tp, :] = (acc_ref[i] / ls[i]).T.astype(o_ref.dtype)


def mla_attention(qt, k, vt, tq=4096, tk=512, parts=2):
    s = k.shape[0]
    return pl.pallas_call(
        functools.partial(_mla_attn_kernel, tk=tk),
        grid=(MLA_HEADS, s // tq),
        in_specs=[
            pl.BlockSpec((MLA_QK_PAD, tq), lambda h, i: (h, i)),
            pl.BlockSpec((s, MLA_QK_PAD), lambda h, i: (0, h)),
            pl.BlockSpec((MLA_V, s), lambda h, i: (h, 0)),
        ],
        out_specs=pl.BlockSpec((tq, MLA_V), lambda h, i: (i, h)),
        out_shape=jax.ShapeDtypeStruct((s, MLA_HEADS * MLA_V), BF16),
        scratch_shapes=[
            pltpu.VMEM((parts, tk, tq // parts), F32),
            pltpu.VMEM((parts, tk, tq // parts), F32),
            pltpu.VMEM((parts, MLA_V, tq // parts), F32),
        ],
        compiler_params=_params("parallel", "parallel"),
        name="mla_attention",
    )(qt, k, vt)


def _rot_cols(w):
    half = w.shape[-1] // 2
    return jnp.concatenate([-w[..., half:], w[..., :half]], axis=-1)


def mla_layer(x, mix_norm, w_in, q_norm, w_uq, kv_norm, w_ukv, w_o, tab, cos_t, sin_t):
    w_kr = w_in[:, MLA_Q_LORA + MLA_KV_LORA:]
    w_in_ext = jnp.concatenate([w_in, _rot_cols(w_kr)], axis=1).astype(BF16)
    w_uq_t = w_uq.astype(BF16).T
    wkv = w_ukv.astype(BF16).reshape(MLA_KV_LORA, MLA_HEADS, MLA_NOPE + MLA_V)
    w_k = wkv[..., :MLA_NOPE].reshape(MLA_KV_LORA, -1)
    w_v_t = wkv[..., MLA_NOPE:].reshape(MLA_KV_LORA, -1).T
    cq, ckv, kr = mla_in(x, mix_norm, w_in_ext, q_norm, kv_norm, tab)
    qt = mla_qt(cq, w_uq_t, cos_t, sin_t)
    k, vt = mla_kv(ckv, w_k, w_v_t, kr)
    o = mla_attention(qt, k, vt)
    return matmul_residual(o, w_o.astype(BF16), x)


def _diff_attn_kernel(lam_ref, qt_ref, k_ref, vt_ref, g_ref, o_ref, sa_ref, sb_ref, acc_ref, *, tk, lam_init):
    d = DIFF_HEAD_DIM
    qs = [qt_ref[:d, :], qt_ref[d:, :]]
    l0, l1 = _flash_t(qs, lambda k, i: k[:, i * d:(i + 1) * d], k_ref, vt_ref, sa_ref, sb_ref, acc_ref, tk=tk)
    ot = acc_ref[0] / l0 - lam_ref[0, 0] * (acc_ref[1] / l1)
    o = _rms(ot.T, g_ref[...], DIFF_EPS) * (1.0 - lam_init)
    o_ref[...] = o.astype(o_ref.dtype)


def diff_attention(qt, k, vt, lam, subln, lam_init, tq=2048, tk=512):
    s = k.shape[0]
    hd = 2 * DIFF_HEAD_DIM
    return pl.pallas_call(
        functools.partial(_diff_attn_kernel, tk=tk, lam_init=lam_init),
        grid=(DIFF_HEADS, s // tq),
        in_specs=[
            pl.BlockSpec(memory_space=pltpu.SMEM),
            pl.BlockSpec((hd, tq), lambda h, i: (h, i)),
            pl.BlockSpec((s, hd), lambda h, i: (0, h)),
            pl.BlockSpec((hd, s), lambda h, i: (h, 0)),
            pl.BlockSpec((1, hd), lambda h, i: (0, 0)),
        ],
        out_specs=pl.BlockSpec((tq, hd), lambda h, i: (i, h)),
        out_shape=jax.ShapeDtypeStruct((s, DIFF_HEADS * hd), BF16),
        scratch_shapes=[
            pltpu.VMEM((2, tk, tq), F32),
            pltpu.VMEM((2, tk, tq), F32),
            pltpu.VMEM((2, hd, tq), F32),
        ],
        compiler_params=_params("parallel", "parallel"),
        name="diff_attention",
    )(lam, qt, k, vt, subln.reshape(1, hd))


def _lambda_kernel(lq1_ref, lk1_ref, lq2_ref, lk2_ref, o_ref, *, lam_init):
    a = jnp.sum(lq1_ref[...] * lk1_ref[...], axis=-1, keepdims=True)
    b = jnp.sum(lq2_ref[...] * lk2_ref[...], axis=-1, keepdims=True)
    o_ref[...] = jnp.exp(a) - jnp.exp(b) + lam_init


def diff_lambda(lq1, lk1, lq2, lk2, lam_init):
    args = [v.reshape(1, -1) for v in (lq1, lk1, lq2, lk2)]
    return pl.pallas_call(
        functools.partial(_lambda_kernel, lam_init=lam_init),
        out_shape=jax.ShapeDtypeStruct((1, 1), F32),
        name="diff_lambda",
    )(*args)


def diff_layer(x, mix_norm, w_qkv, lq1, lk1, lq2, lk2, subln, w_o, rope, rope_t, layer_idx):
    lam_init = 0.8 - 0.6 * math.exp(-0.3 * layer_idx)
    w = w_qkv.astype(BF16)
    qt = norm_matmul_t(x, mix_norm, w[:, :DIFF_QK].T, BF16, rope_t=rope_t, scale=DIFF_SCALE * LOG2E)
    k = norm_matmul(x, mix_norm, w[:, DIFF_QK:2 * DIFF_QK], BF16, rope=rope)
    vt = norm_matmul_t(x, mix_norm, w[:, 2 * DIFF_QK:].T, BF16)
    lam = diff_lambda(lq1, lk1, lq2, lk2, lam_init)
    o = diff_attention(qt, k, vt, lam, subln, lam_init)
    return matmul_residual(o, w_o.astype(BF16), x)


FFT_R = 128
FFT_N = FFT_R * FFT_R
FFT_G = 8
HY_TC = 512
HY_NQ = HY_TC // LANE


def _dft_constants():
    r = np.arange(FFT_R)
    n2, k1, n1 = r[:, None, None], r[None, :, None], r[None, None, :]
    th = 2.0 * np.pi * ((k1 * (FFT_R * n1 + n2)) % FFT_N) / FFT_N
    w1 = np.concatenate([np.cos(th), -np.sin(th)], axis=1)
    tht = np.transpose(th, (0, 2, 1))
    v1 = np.concatenate([np.cos(tht), -np.sin(tht)], axis=2) / FFT_N
    ph = 2.0 * np.pi * ((r[:, None] * r[None, :]) % FFT_R) / FFT_R
    c, s = np.cos(ph), np.sin(ph)
    g_fwd = np.block([[c, s], [-s, c]])
    g_inv = np.block([[c, -s], [s, c]])
    as_bf16 = lambda a: a.astype(np.float32).astype(BF16)
    half = FFT_R // 2
    return dict(w1_full=as_bf16(w1), w1_half=as_bf16(w1[:, :, :half]), v1_half=as_bf16(v1[:, :half, :]),
                g_fwd=as_bf16(g_fwd), g_inv=as_bf16(g_inv))


def _pack_pair(x):
    h = x.shape[0] // 2
    hi = lax.bitcast_convert_type(x[:h].astype(BF16).astype(F32), jnp.uint32)
    lo = lax.bitcast_convert_type(x[h:].astype(BF16).astype(F32), jnp.uint32)
    return hi | (lo >> 16)


def _unpack_pair(p):
    hi = lax.bitcast_convert_type(p & jnp.uint32(0xFFFF0000), F32)
    lo = lax.bitcast_convert_type(p << 16, F32)
    return jnp.concatenate([hi, lo], axis=0).astype(BF16)


HALO = 16


def _in_proj_conv_kernel(x_ref, prev_ref, next_ref, g_ref, w_ref, cw_ref, cb_ref, o_ref, xn_ref):
    i = pl.program_id(0)
    tm = x_ref.shape[0]

    @pl.when(pl.program_id(1) == 0)
    def _():
        g = g_ref[...]
        before = jnp.where(i > 0, _rms(prev_ref[...], g, RMS_EPS), 0.0)
        after = jnp.where(i < pl.num_programs(0) - 1, _rms(next_ref[...], g, RMS_EPS), 0.0)
        xn_ref[:HALO] = before.astype(BF16)
        xn_ref[HALO:HALO + tm] = _rms(x_ref[...], g, RMS_EPS).astype(BF16)
        xn_ref[HALO + tm:] = after.astype(BF16)

    u = jnp.dot(xn_ref[...], w_ref[...], preferred_element_type=F32)
    y = (u[HALO - 1:HALO - 1 + tm] * cw_ref[0:1, :] + u[HALO:HALO + tm] * cw_ref[1:2, :]
         + u[HALO + 1:HALO + 1 + tm] * cw_ref[2:3, :] + cb_ref[...])
    for q in range(o_ref.shape[0]):
        o_ref[q] = y[:, q * LANE:(q + 1) * LANE]


def in_proj_conv(x, g, w, conv_w, conv_b, tm=1024, tn=512):
    s, k = x.shape
    n = w.shape[1]
    nb = tm // HALO
    return pl.pallas_call(
        _in_proj_conv_kernel,
        grid=(s // tm, n // tn),
        in_specs=[
            pl.BlockSpec((tm, k), lambda i, j: (i, 0)),
            pl.BlockSpec((HALO, k), lambda i, j: (jnp.maximum(i * nb - 1, 0), 0)),
            pl.BlockSpec((HALO, k), lambda i, j: (jnp.minimum((i + 1) * nb, s // HALO - 1), 0)),
            pl.BlockSpec((1, k), lambda i, j: (0, 0)),
            pl.BlockSpec((k, tn), lambda i, j: (0, j)),
            pl.BlockSpec((3, tn), lambda i, j: (0, j)),
            pl.BlockSpec((1, tn), lambda i, j: (0, j)),
        ],
        out_specs=pl.BlockSpec((tn // LANE, tm, LANE), lambda i, j: (j, i, 0)),
        out_shape=jax.ShapeDtypeStruct((n // LANE, s, LANE), F32),
        scratch_shapes=[pltpu.VMEM((tm + 2 * HALO, k), BF16)],
        compiler_params=_params("parallel", "arbitrary"),
        name="hy_in_proj_conv",
    )(x, x, x, g.reshape(1, k), w, conv_w, conv_b.reshape(1, n))


def _filter_mlp_kernel(z_ref, w1_ref, b1_ref, f_ref, w2_ref, b2_ref, w3_ref, b3_ref, o_ref):
    hp = lax.Precision.HIGHEST
    f = f_ref[...]
    h = jnp.sin(f * (jnp.dot(z_ref[...], w1_ref[...], precision=hp, preferred_element_type=F32) + b1_ref[...]))
    h = jnp.sin(f * (jnp.dot(h, w2_ref[...], precision=hp, preferred_element_type=F32) + b2_ref[...]))
    o_ref[...] = jnp.sin(f * (jnp.dot(h, w3_ref[...], precision=hp, preferred_element_type=F32) + b3_ref[...]))


def filter_mlp(zfeat, f_w1, f_b1, f_freq, f_w2, f_b2, f_w3, f_b3, tm=2048):
    n, e = zfeat.shape
    wd = f_w2.shape[0]
    row = lambda i: (i, 0)
    fixed = lambda i: (0, 0)
    vec = lambda v: v.reshape(1, wd)
    return pl.pallas_call(
        _filter_mlp_kernel,
        grid=(n // tm,),
        in_specs=[pl.BlockSpec((tm, e), row), pl.BlockSpec((e, wd), fixed), pl.BlockSpec((1, wd), fixed),
                  pl.BlockSpec((1, wd), fixed), pl.BlockSpec((wd, wd), fixed), pl.BlockSpec((1, wd), fixed),
                  pl.BlockSpec((wd, wd), fixed), pl.BlockSpec((1, wd), fixed)],
        out_specs=pl.BlockSpec((tm, wd), row),
        out_shape=jax.ShapeDtypeStruct((n, wd), F32),
        compiler_params=_params("parallel"),
        name="hy_filter_mlp",
    )(zfeat, f_w1, vec(f_b1), vec(f_freq), f_w2, vec(f_b2), f_w3, vec(f_b3))


def _rows_by_group(ref):
    t = [jnp.swapaxes(ref[i], 0, 1) for i in range(ref.shape[0])]
    return [jnp.concatenate([ti[r] for ti in t], axis=1) for r in range(FFT_G)]


def _store_by_group(ref, vals):
    for i in range(ref.shape[0]):
        ref[i] = jnp.swapaxes(jnp.stack([v[:, i * LANE:(i + 1) * LANE] for v in vals], axis=0), 0, 1)


def _kern_stage1_kernel(h_ref, w00_ref, w01_ref, w10_ref, w11_ref, dl_ref, w1_ref, kb_ref, nrm_ref, *, seq):
    h_rows = _rows_by_group(h_ref)
    g = pl.program_id(1)
    half = FFT_R // 2
    w_fwd = (w00_ref[...], w01_ref[...])
    w_bwd = (w10_ref[...], w11_ref[...])
    rate = dl_ref[...] * (-1.0 / (seq - 1))
    n1 = lax.broadcasted_iota(jnp.int32, (half, 1), 0)
    packed = [[] for _ in range(HY_ORDER)]
    acc = [jnp.zeros(nrm_ref.shape[1:], F32) for _ in range(HY_ORDER)]
    for r in range(FFT_G):
        n2 = g * FFT_G + r
        hs = h_rows[r].astype(BF16)
        lag_f = FFT_R * n1 + n2
        lag_b = seq - FFT_R * n1 - n2
        decay_f = jnp.exp(lag_f.astype(F32) * rate)
        decay_b = jnp.where(lag_b == seq, 0.0, jnp.exp(lag_b.astype(F32) * rate))
        for o in range(HY_ORDER):
            kf = jnp.dot(hs[:half], w_fwd[o], preferred_element_type=F32) * decay_f
            kb = jnp.dot(hs[half:], w_bwd[o], preferred_element_type=F32) * decay_b
            slab = jnp.concatenate([kf, kb], axis=0)
            acc[o] += jnp.sum(jnp.abs(slab), axis=0, keepdims=True)
            packed[o].append(_pack_pair(jnp.dot(w1_ref[r], slab.astype(BF16), preferred_element_type=F32)))
    for o in range(HY_ORDER):
        _store_by_group(kb_ref.at[o], packed[o])

    @pl.when(g == 0)
    def _():
        for o in range(HY_ORDER):
            nrm_ref[o] = acc[o]

    @pl.when(g > 0)
    def _():
        for o in range(HY_ORDER):
            nrm_ref[o] += acc[o]


def kern_stage1(h3, w4, deltas, w1_full, seq, tc=HY_TC):
    d = deltas.shape[0]
    jn = d // tc
    h3v = h3.reshape(1, FFT_R, FFT_R, LANE)
    assert HY_ORDER == 2
    kern = functools.partial(_kern_stage1_kernel, seq=seq)
    w4_spec = lambda direction, order: pl.BlockSpec((LANE, tc), lambda j, g: (0, (direction * HY_ORDER + order) * jn + j))
    return pl.pallas_call(
        kern,
        grid=(jn, FFT_R // FFT_G),
        in_specs=[
            pl.BlockSpec((1, FFT_R, FFT_G, LANE), lambda j, g: (0, 0, g, 0)),
            w4_spec(0, 0), w4_spec(0, 1), w4_spec(1, 0), w4_spec(1, 1),
            pl.BlockSpec((1, tc), lambda j, g: (0, j)),
            pl.BlockSpec((FFT_G, 2 * FFT_R, FFT_R), lambda j, g: (g, 0, 0)),
        ],
        out_specs=[
            pl.BlockSpec((HY_ORDER, HY_NQ, FFT_R, FFT_G, LANE), lambda j, g: (0, j, 0, g, 0)),
            pl.BlockSpec((HY_ORDER, 1, tc), lambda j, g: (0, 0, j)),
        ],
        out_shape=[
            jax.ShapeDtypeStruct((HY_ORDER, d // LANE, FFT_R, FFT_R, LANE), jnp.uint32),
            jax.ShapeDtypeStruct((HY_ORDER, 1, d), F32),
        ],
        compiler_params=_params("parallel", "arbitrary"),
        name="hy_kern_stage1",
    )(h3v, w4, w4, w4, w4, deltas.reshape(1, d), w1_full)


def _join_lanes(ref, idx):
    return jnp.concatenate([ref[(q,) + idx] for q in range(ref.shape[0])], axis=1)


def _kern_stage3_kernel(kb_ref, g_ref, o_ref):
    gm = g_ref[...]
    for r in range(FFT_G):
        b = _unpack_pair(_join_lanes(kb_ref.at[0], (r,)))
        o_ref[0, r] = jnp.dot(gm, b, preferred_element_type=F32).astype(o_ref.dtype)


def kern_stage3(kb, g_fwd, tc=HY_TC):
    d = kb.shape[1] * LANE
    return pl.pallas_call(
        _kern_stage3_kernel,
        grid=(HY_ORDER, FFT_R // FFT_G, d // tc),
        in_specs=[
            pl.BlockSpec((1, HY_NQ, FFT_G, FFT_R, LANE), lambda o, kg, j: (o, j, kg, 0, 0)),
            pl.BlockSpec((2 * FFT_R, 2 * FFT_R), lambda o, kg, j: (0, 0)),
        ],
        out_specs=pl.BlockSpec((1, FFT_G, 2 * FFT_R, tc), lambda o, kg, j: (o, kg, 0, j)),
        out_shape=jax.ShapeDtypeStruct((HY_ORDER, FFT_R, 2 * FFT_R, d), BF16),
        compiler_params=_params("parallel", "parallel", "parallel"),
        name="hy_kern_stage3",
    )(kb, g_fwd)


def _sig_stage1_kernel(z_ref, w1_ref, o_ref):
    z_rows = _rows_by_group(z_ref)
    _store_by_group(o_ref, [_pack_pair(jnp.dot(w1_ref[r], z_rows[r].astype(BF16), preferred_element_type=F32))
                            for r in range(FFT_G)])


def sig_stage1(z_src, slab0, d, w1_half, tc=HY_TC):
    s = z_src.shape[1]
    rows = s // FFT_R
    zv = z_src.reshape(z_src.shape[0], rows, FFT_R, LANE)
    return pl.pallas_call(
        _sig_stage1_kernel,
        grid=(FFT_R // FFT_G, d // tc),
        in_specs=[
            pl.BlockSpec((HY_NQ, rows, FFT_G, LANE), lambda g, j: (slab0 // HY_NQ + j, 0, g, 0)),
            pl.BlockSpec((FFT_G, 2 * FFT_R, rows), lambda g, j: (g, 0, 0)),
        ],
        out_specs=pl.BlockSpec((HY_NQ, FFT_R, FFT_G, LANE), lambda g, j: (j, 0, g, 0)),
        out_shape=jax.ShapeDtypeStruct((d // LANE, FFT_R, FFT_R, LANE), jnp.uint32),
        compiler_params=_params("parallel", "parallel"),
        name="hy_sig_stage1",
    )(zv, w1_half)


def _sig_stage3_kernel(zb_ref, ks_ref, gf_ref, gi_ref, o_ref):
    gf = gf_ref[...]
    gi = gi_ref[...]
    for r in range(FFT_G):
        x = jnp.dot(gf, _unpack_pair(_join_lanes(zb_ref, (r,))), preferred_element_type=F32)
        ks = ks_ref[0, r].astype(F32)
        xr, xi = x[:FFT_R], x[FFT_R:]
        kr, ki = ks[:FFT_R], ks[FFT_R:]
        y = jnp.concatenate([xr * kr - xi * ki, xr * ki + xi * kr], axis=0).astype(BF16)
        packed = _pack_pair(jnp.dot(gi, y, preferred_element_type=F32))
        for q in range(o_ref.shape[0]):
            o_ref[q, r] = packed[:, q * LANE:(q + 1) * LANE]


def sig_stage3(zb, kspec, order, g_fwd, g_inv, tc=HY_TC):
    d = zb.shape[0] * LANE
    blk = pl.BlockSpec((HY_NQ, FFT_G, FFT_R, LANE), lambda kg, j: (j, kg, 0, 0))
    mat = pl.BlockSpec((2 * FFT_R, 2 * FFT_R), lambda kg, j: (0, 0))
    return pl.pallas_call(
        _sig_stage3_kernel,
        grid=(FFT_R // FFT_G, d // tc),
        in_specs=[blk, pl.BlockSpec((1, FFT_G, 2 * FFT_R, tc), lambda kg, j: (order, kg, 0, j)), mat, mat],
        out_specs=blk,
        out_shape=jax.ShapeDtypeStruct(zb.shape, jnp.uint32),
        compiler_params=_params("parallel", "parallel"),
        name="hy_sig_stage3",
    )(zb, kspec, g_fwd, g_inv)


def _sig_inverse1_kernel(yb_ref, v_ref, gate_ref, z_ref, nrm_ref, bias_ref, *rest, next_stage1):
    inv = 1.0 / nrm_ref[0]
    bias = bias_ref[0]
    yb_rows, gate_rows, z_rows = _rows_by_group(yb_ref), _rows_by_group(gate_ref), _rows_by_group(z_ref)
    out = []
    for r in range(FFT_G):
        conv = jnp.dot(v_ref[r], _unpack_pair(yb_rows[r]), preferred_element_type=F32) * inv
        out.append(gate_rows[r] * (conv + z_rows[r] * bias))
    if next_stage1:
        w1_ref, o_ref, zb_ref = rest
        _store_by_group(zb_ref, [_pack_pair(jnp.dot(w1_ref[r], out[r].astype(BF16), preferred_element_type=F32))
                                 for r in range(FFT_G)])
    else:
        o_ref, = rest
    _store_by_group(o_ref, out)


def sig_inverse1(yb, v1_half, gate_src, gate_slab0, z_src, z_slab0, nrm, bias, order, w1_half=None, tc=HY_TC):
    d = yb.shape[0] * LANE
    s = gate_src.shape[1]
    rows = s // FFT_R
    view = lambda a: a.reshape(a.shape[0], rows, FFT_R, LANE)
    sig = (HY_NQ, rows, FFT_G, LANE)
    freq = (HY_NQ, FFT_R, FFT_G, LANE)
    in_specs = [
        pl.BlockSpec(freq, lambda g, j: (j, 0, g, 0)),
        pl.BlockSpec((FFT_G, rows, 2 * FFT_R), lambda g, j: (g, 0, 0)),
        pl.BlockSpec(sig, lambda g, j: (gate_slab0 // HY_NQ + j, 0, g, 0)),
        pl.BlockSpec(sig, lambda g, j: (z_slab0 // HY_NQ + j, 0, g, 0)),
        pl.BlockSpec((1, 1, tc), lambda g, j: (order, 0, j)),
        pl.BlockSpec((1, 1, tc), lambda g, j: (order, 0, j)),
    ]
    args = [yb, v1_half, view(gate_src), view(z_src), nrm, bias.reshape(HY_ORDER, 1, d)]
    out_specs = [pl.BlockSpec(sig, lambda g, j: (j, 0, g, 0))]
    out_shape = [jax.ShapeDtypeStruct((d // LANE, rows, FFT_R, LANE), F32)]
    if w1_half is not None:
        in_specs.append(pl.BlockSpec((FFT_G, 2 * FFT_R, rows), lambda g, j: (g, 0, 0)))
        args.append(w1_half)
        out_specs.append(pl.BlockSpec(freq, lambda g, j: (j, 0, g, 0)))
        out_shape.append(jax.ShapeDtypeStruct((d // LANE, FFT_R, FFT_R, LANE), jnp.uint32))
    outs = pl.pallas_call(
        functools.partial(_sig_inverse1_kernel, next_stage1=w1_half is not None),
        grid=(FFT_R // FFT_G, d // tc),
        in_specs=in_specs,
        out_specs=out_specs,
        out_shape=out_shape,
        compiler_params=_params("parallel", "parallel"),
        name="hy_sig_inverse1",
    )(*args)
    z_new = outs[0].reshape(d // LANE, s, LANE)
    return (z_new, outs[1]) if w1_half is not None else (z_new, None)


def _hyena_positions(seq):
    t = jnp.linspace(0.0, 1.0, seq, dtype=F32)[:, None]
    bands = (HY_EMB_DIM - 1) // 2
    w = 2.0 * math.pi * jnp.arange(seq, dtype=F32)[:, None] / seq
    f = jnp.linspace(1e-4, bands - 1, bands, dtype=F32)[None, :]
    z = jnp.concatenate([t, jnp.cos(f * w), -jnp.sin(f * w)], axis=-1)
    return jnp.pad(z, ((0, 0), (0, LANE - HY_EMB_DIM)))


def hyena_mixer(u, f_w1, f_b1, f_freq, f_w2, f_b2, f_w3, f_b3, f_w4, bias):
    s = u.shape[1]
    d = u.shape[0] * LANE // 3
    assert 2 * s == FFT_N and d % HY_TC == 0
    c = _dft_constants()
    h3 = filter_mlp(_hyena_positions(s), jnp.pad(f_w1, ((0, LANE - HY_EMB_DIM), (0, 0))), f_b1, f_freq, f_w2, f_b2, f_w3, f_b3)
    width = h3.shape[1]
    h3 = jnp.concatenate([h3, h3[:1], h3[:0:-1]], axis=0)
    h3 = jnp.pad(h3, ((0, 0), (0, LANE - width)))
    w4 = jnp.pad(f_w4.astype(BF16), ((0, LANE - width), (0, 0)))
    deltas = jnp.abs(jnp.linspace(HY_MIN_DECAY, HY_MAX_DECAY, d, dtype=F32))
    kb, nrm = kern_stage1(h3, w4, deltas, c["w1_full"], s)
    kspec = kern_stage3(kb, c["g_fwd"])
    nslab = d // LANE
    z_src, z_slab0 = u, 2 * nslab
    zb = sig_stage1(z_src, z_slab0, d, c["w1_half"])
    for o in range(HY_ORDER):
        yb = sig_stage3(zb, kspec, o, c["g_fwd"], c["g_inv"])
        w1_next = c["w1_half"] if o + 1 < HY_ORDER else None
        z_src, zb = sig_inverse1(yb, c["v1_half"], u, o * nslab, z_src, z_slab0, nrm, bias, o, w1_next)
        z_slab0 = 0
    return z_src


def hyena_layer(x, mix_norm, w_in, conv_w, conv_b, f_w1, f_b1, f_freq, f_w2, f_b2, f_w3, f_b3, f_w4, bias, w_out):
    u = in_proj_conv(x, mix_norm, w_in.astype(BF16), conv_w, conv_b)
    z = hyena_mixer(u, f_w1, f_b1, f_freq, f_w2, f_b2, f_w3, f_b3, f_w4, bias)
    return matmul_residual(z, w_out.astype(BF16), x, a_in_slabs=True)


def _rope_tables(seq, dim):
    inv = 1.0 / (ROPE_THETA ** (jnp.arange(0, dim, 2, dtype=F32) / dim))
    ang = jnp.arange(seq, dtype=F32)[:, None] * inv[None, :]
    return jnp.cos(ang), jnp.sin(ang)


def kernel(x, l0_mix_norm, l0_mla_w_in, l0_mla_q_norm, l0_mla_w_uq, l0_mla_kv_norm, l0_mla_w_ukv, l0_mla_w_o, l0_ffn_norm, l0_ffn_w_gate, l0_ffn_w_up, l0_ffn_w_down, l1_mix_norm, l1_hy_w_in, l1_hy_conv_w, l1_hy_conv_b, l1_hy_f_w1, l1_hy_f_b1, l1_hy_f_freq, l1_hy_f_w2, l1_hy_f_b2, l1_hy_f_w3, l1_hy_f_b3, l1_hy_f_w4, l1_hy_bias, l1_hy_w_out, l1_ffn_norm, l1_ffn_w_gate, l1_ffn_w_up, l1_ffn_w_down, l2_mix_norm, l2_da_w_qkv, l2_da_lq1, l2_da_lk1, l2_da_lq2, l2_da_lk2, l2_da_subln, l2_da_w_o, l2_ffn_norm, l2_ffn_w_gate, l2_ffn_w_up, l2_ffn_w_down, l3_mix_norm, l3_mla_w_in, l3_mla_q_norm, l3_mla_w_uq, l3_mla_kv_norm, l3_mla_w_ukv, l3_mla_w_o, l3_ffn_norm, l3_ffn_w_gate, l3_ffn_w_up, l3_ffn_w_down, final_norm):
    b, s, d = x.shape
    assert b == 1
    h = x.reshape(s, d)

    mc, ms = _rope_tables(s, MLA_ROPE)
    mla_tab = jnp.concatenate([mc, mc, ms, ms], axis=-1)
    mla = (mla_tab, mc.T, ms.T)
    dc, ds = _rope_tables(s, DIFF_HEAD_DIM)
    diff_rope = (jnp.concatenate([dc, dc], axis=-1), jnp.concatenate([-ds, ds], axis=-1))
    diff_rope_t = (dc.T, ds.T)

    def ffn(h, g, wg, wu, wd, out_norm_gain=None):
        return ffn_block(h, g, wg.astype(BF16), wu.astype(BF16), wd.astype(BF16), out_norm_gain)

    h = mla_layer(h, l0_mix_norm, l0_mla_w_in, l0_mla_q_norm, l0_mla_w_uq, l0_mla_kv_norm, l0_mla_w_ukv, l0_mla_w_o, *mla)
    h = ffn(h, l0_ffn_norm, l0_ffn_w_gate, l0_ffn_w_up, l0_ffn_w_down)
    h = hyena_layer(h, l1_mix_norm, l1_hy_w_in, l1_hy_conv_w, l1_hy_conv_b, l1_hy_f_w1, l1_hy_f_b1, l1_hy_f_freq, l1_hy_f_w2, l1_hy_f_b2, l1_hy_f_w3, l1_hy_f_b3, l1_hy_f_w4, l1_hy_bias, l1_hy_w_out)
    h = ffn(h, l1_ffn_norm, l1_ffn_w_gate, l1_ffn_w_up, l1_ffn_w_down)
    h = diff_layer(h, l2_mix_norm, l2_da_w_qkv, l2_da_lq1, l2_da_lk1, l2_da_lq2, l2_da_lk2, l2_da_subln, l2_da_w_o, diff_rope, diff_rope_t, 2)
    h = ffn(h, l2_ffn_norm, l2_ffn_w_gate, l2_ffn_w_up, l2_ffn_w_down)
    h = mla_layer(h, l3_mix_norm, l3_mla_w_in, l3_mla_q_norm, l3_mla_w_uq, l3_mla_kv_norm, l3_mla_w_ukv, l3_mla_w_o, *mla)
    h = ffn(h, l3_ffn_norm, l3_ffn_w_gate, l3_ffn_w_up, l3_ffn_w_down, final_norm)
    return h.reshape(b, s, d)
```

```python
import functools
import math

import jax
import jax.numpy as jnp
import numpy as np
from jax import lax
from jax.experimental import pallas as pl
from jax.experimental.pallas import tpu as pltpu

F32 = jnp.float32
BF16 = jnp.bfloat16

D_MODEL = 2048
RMS_EPS = 1e-6
ROPE_THETA = 10000.0
LOG2E = math.log2(math.e)

MLA_HEADS = 16
MLA_Q_LORA = 768
MLA_KV_LORA = 512
MLA_NOPE = 128
MLA_ROPE = 64
MLA_V = 128
MLA_QK = MLA_NOPE + MLA_ROPE
MLA_SCALE = MLA_QK ** -0.5
MLA_QK_PAD = 256

HY_ORDER = 2
HY_EMB_DIM = 33
HY_TARGET = 1e-2
HY_MAX_DECAY = math.log(HY_TARGET) / 0.3
HY_MIN_DECAY = math.log(HY_TARGET) / 1.5

DIFF_HEAD_DIM = 128
DIFF_HEADS = D_MODEL // (2 * DIFF_HEAD_DIM)
DIFF_QK = DIFF_HEADS * 2 * DIFF_HEAD_DIM
DIFF_SCALE = DIFF_HEAD_DIM ** -0.5
DIFF_EPS = 1e-5

LANE = 128
VMEM_BYTES = 64 * 1024 * 1024
VMEM_LIMIT = VMEM_BYTES * 7 // 8
NEG_BIG = -1e30

PROJ_TM, PROJ_TN = 1024, 1024
MLA_IN_TM = 512
FFN_TM, FFN_TF = 1024, 512
MLA_TQ, DIFF_TQ, ATTN_TK = 4096, 2048, 512
HY_IN_TN = 1024


def _params(*sem):
    return pltpu.CompilerParams(dimension_semantics=sem, vmem_limit_bytes=VMEM_LIMIT)


def _steps(dim, tile):
    assert dim % tile == 0, f"tile {tile} does not divide {dim}"
    return dim // tile


def _rms(x, g, eps):
    return x * lax.rsqrt(jnp.mean(x * x, axis=-1, keepdims=True) + eps) * g


def _dot_nt(a, b):
    return lax.dot_general(a, b, (((1,), (1,)), ((), ())), preferred_element_type=F32)


def _rope_rows(y, c, s):
    h = c.shape[0]
    y1, y2 = y[:h], y[h:]
    return y1 * c - y2 * s, y2 * c + y1 * s


def _norm_matmul_kernel(x_ref, g_ref, w_ref, *rest, rope, scale):
    o_ref, xn_ref = rest[-2:]

    @pl.when(pl.program_id(1) == 0)
    def _():
        xn_ref[...] = _rms(x_ref[...], g_ref[...], RMS_EPS).astype(BF16)

    y = jnp.dot(xn_ref[...], w_ref[...], preferred_element_type=F32)
    if rope:
        c = rest[0][...] * scale
        s = rest[1][...] * scale
        hd = c.shape[1]
        for g in range(y.shape[1] // hd):
            t = y[:, g * hd:(g + 1) * hd]
            o_ref[:, g * hd:(g + 1) * hd] = (t * c + pltpu.roll(t, hd // 2, axis=1) * s).astype(o_ref.dtype)
    else:
        o_ref[...] = (y * scale).astype(o_ref.dtype)


def norm_matmul(x, g, w, out_dtype, rope=None, scale=1.0, tm=PROJ_TM, tn=PROJ_TN):
    m, k = x.shape
    n = w.shape[1]
    in_specs = [
        pl.BlockSpec((tm, k), lambda i, j: (i, 0)),
        pl.BlockSpec((1, k), lambda i, j: (0, 0)),
        pl.BlockSpec((k, tn), lambda i, j: (0, j)),
    ]
    args = [x, g.reshape(1, k), w]
    if rope is not None:
        in_specs += [pl.BlockSpec((tm, rope[0].shape[1]), lambda i, j: (i, 0))] * 2
        args += list(rope)
    return pl.pallas_call(
        functools.partial(_norm_matmul_kernel, rope=rope is not None, scale=scale),
        grid=(_steps(m, tm), _steps(n, tn)),
        in_specs=in_specs,
        out_specs=pl.BlockSpec((tm, tn), lambda i, j: (i, j)),
        out_shape=jax.ShapeDtypeStruct((m, n), out_dtype),
        scratch_shapes=[pltpu.VMEM((tm, k), BF16)],
        compiler_params=_params("parallel", "arbitrary"),
        name="norm_matmul",
    )(*args)


def _norm_matmul_t_kernel(x_ref, g_ref, wt_ref, *rest, rope, scale):
    o_ref, xn_ref = rest[-2:]

    @pl.when(pl.program_id(1) == 0)
    def _():
        xn_ref[...] = _rms(x_ref[...], g_ref[...], RMS_EPS).astype(BF16)

    y = _dot_nt(wt_ref[...], xn_ref[...]) * scale
    if rope:
        c = rest[0][...]
        s = rest[1][...]
        half = c.shape[0]
        for g in range(y.shape[0] // (2 * half)):
            r1, r2 = _rope_rows(y[2 * g * half:2 * (g + 1) * half], c, s)
            o_ref[2 * g * half:(2 * g + 1) * half, :] = r1.astype(o_ref.dtype)
            o_ref[(2 * g + 1) * half:2 * (g + 1) * half, :] = r2.astype(o_ref.dtype)
    else:
        o_ref[...] = y.astype(o_ref.dtype)


def norm_matmul_t(x, g, wt, out_dtype, rope_t=None, scale=1.0, tm=PROJ_TM, tn=PROJ_TN):
    m, k = x.shape
    n = wt.shape[0]
    in_specs = [
        pl.BlockSpec((tm, k), lambda i, j: (i, 0)),
        pl.BlockSpec((1, k), lambda i, j: (0, 0)),
        pl.BlockSpec((tn, k), lambda i, j: (j, 0)),
    ]
    args = [x, g.reshape(1, k), wt]
    if rope_t is not None:
        in_specs += [pl.BlockSpec((rope_t[0].shape[0], tm), lambda i, j: (0, i))] * 2
        args += list(rope_t)
    return pl.pallas_call(
        functools.partial(_norm_matmul_t_kernel, rope=rope_t is not None, scale=scale),
        grid=(_steps(m, tm), _steps(n, tn)),
        in_specs=in_specs,
        out_specs=pl.BlockSpec((tn, tm), lambda i, j: (j, i)),
        out_shape=jax.ShapeDtypeStruct((n, m), out_dtype),
        scratch_shapes=[pltpu.VMEM((tm, k), BF16)],
        compiler_params=_params("parallel", "arbitrary"),
        name="norm_matmul_t",
    )(*args)


def _matmul_res_kernel(a_ref, w_ref, r_ref, o_ref):
    if len(a_ref.shape) == 3:
        a = jnp.concatenate([a_ref[q] for q in range(a_ref.shape[0])], axis=1)
    else:
        a = a_ref[...]
    o_ref[...] = r_ref[...] + jnp.dot(a.astype(BF16), w_ref[...], preferred_element_type=F32)


def matmul_residual(a, w, res, a_in_slabs=False, tm=PROJ_TM, tn=PROJ_TN):
    k, n = w.shape
    m = res.shape[0]
    if a_in_slabs:
        a_spec = pl.BlockSpec((k // LANE, tm, LANE), lambda i, j: (0, i, 0))
    else:
        a_spec = pl.BlockSpec((tm, k), lambda i, j: (i, 0))
    return pl.pallas_call(
        _matmul_res_kernel,
        grid=(_steps(m, tm), _steps(n, tn)),
        in_specs=[
            a_spec,
            pl.BlockSpec((k, tn), lambda i, j: (0, j)),
            pl.BlockSpec((tm, tn), lambda i, j: (i, j)),
        ],
        out_specs=pl.BlockSpec((tm, tn), lambda i, j: (i, j)),
        out_shape=jax.ShapeDtypeStruct((m, n), F32),
        compiler_params=_params("parallel", "parallel"),
        name="matmul_residual",
    )(a, w, res)


def _ffn_kernel(x_ref, g_ref, wg_ref, wu_ref, wd_ref, *rest, out_norm):
    o_ref, xn_ref = rest[-2:]

    @pl.when(pl.program_id(1) == 0)
    def _():
        x = x_ref[...]
        xn_ref[...] = _rms(x, g_ref[...], RMS_EPS).astype(BF16)
        o_ref[...] = x

    xn = xn_ref[...]
    a = jnp.dot(xn, wg_ref[...], preferred_element_type=F32)
    b = jnp.dot(xn, wu_ref[...], preferred_element_type=F32)
    h = (a * jax.nn.sigmoid(a) * b).astype(BF16)
    o_ref[...] += jnp.dot(h, wd_ref[...], preferred_element_type=F32)

    if out_norm:
        @pl.when(pl.program_id(1) == pl.num_programs(1) - 1)
        def _():
            o_ref[...] = _rms(o_ref[...], rest[0][...], RMS_EPS)


def ffn_block(x, g, wg, wu, wd, out_norm_gain=None, tm=FFN_TM, tf=FFN_TF):
    m, d = x.shape
    f = wg.shape[1]
    vec = pl.BlockSpec((1, d), lambda i, j: (0, 0))
    in_specs = [
        pl.BlockSpec((tm, d), lambda i, j: (i, 0)),
        vec,
        pl.BlockSpec((d, tf), lambda i, j: (0, j)),
        pl.BlockSpec((d, tf), lambda i, j: (0, j)),
        pl.BlockSpec((tf, d), lambda i, j: (j, 0)),
    ]
    args = [x, g.reshape(1, d), wg, wu, wd]
    if out_norm_gain is not None:
        in_specs.append(vec)
        args.append(out_norm_gain.reshape(1, d))
    return pl.pallas_call(
        functools.partial(_ffn_kernel, out_norm=out_norm_gain is not None),
        grid=(_steps(m, tm), _steps(f, tf)),
        in_specs=in_specs,
        out_specs=pl.BlockSpec((tm, d), lambda i, j: (i, 0)),
        out_shape=jax.ShapeDtypeStruct((m, d), F32),
        scratch_shapes=[pltpu.VMEM((tm, d), BF16)],
        compiler_params=_params("parallel", "arbitrary"),
        name="ffn_block",
    )(*args)


def _flash_t(qs, k_of_part, k_ref, vt_ref, sa_ref, sb_ref, acc_ref, *, tk):
    parts = len(qs)
    tp = qs[0].shape[1]
    nk = k_ref.shape[0] // tk
    assert nk % 2 == 0 and nk >= 4

    def chunk(c):
        return pl.ds(c * tk if isinstance(c, int) else pl.multiple_of(c * tk, tk), tk)

    def scores(c, s_ref):
        k = k_ref[chunk(c), :]
        mc = ()
        for i in range(parts):
            st = jnp.dot(k_of_part(k, i), qs[i], preferred_element_type=F32)
            s_ref[i] = st
            mc += (jnp.max(st, axis=0, keepdims=True),)
        return mc

    def softmax_pv(c, s_ref, mc, state):
        vt = vt_ref[:, chunk(c)]
        new = ()
        for i in range(parts):
            m, l = state[2 * i:2 * i + 2]
            m_new = jnp.maximum(m, mc[i])
            alpha = jnp.exp2(m - m_new)
            p = jnp.exp2(s_ref[i] - m_new)
            l = alpha * l + jnp.sum(p, axis=0, keepdims=True)
            acc_ref[i] = alpha * acc_ref[i] + jnp.dot(vt, p.astype(BF16), preferred_element_type=F32)
            new += (m_new, l)
        return new

    def body(c2, carry):
        mc_a, state = carry[:parts], carry[parts:]
        c = 2 * c2
        mc_b = scores(c + 1, sb_ref)
        state = softmax_pv(c, sa_ref, mc_a, state)
        mc_a = scores(c + 2, sa_ref)
        state = softmax_pv(c + 1, sb_ref, mc_b, state)
        return mc_a + state

    acc_ref[...] = jnp.zeros(acc_ref.shape, F32)
    state0 = (jnp.full((1, tp), NEG_BIG, F32), jnp.zeros((1, tp), F32)) * parts
    carry = lax.fori_loop(0, nk // 2 - 1, body, scores(0, sa_ref) + state0)
    mc_a, state = carry[:parts], carry[parts:]
    mc_b = scores(nk - 1, sb_ref)
    state = softmax_pv(nk - 2, sa_ref, mc_a, state)
    state = softmax_pv(nk - 1, sb_ref, mc_b, state)
    return state[1::2]


MLA_KR_PAD = MLA_QK_PAD - MLA_NOPE


def _rope_pair(t, tab):
    t = t * tab
    return t + pltpu.roll(t, MLA_ROPE, axis=1)


def _mla_in_kernel(x_ref, g_ref, w_ref, qn_ref, kvn_ref, tab_ref, cq_ref, ckv_ref, kr_ref):
    xn = _rms(x_ref[...], g_ref[...], RMS_EPS).astype(BF16)
    h = jnp.dot(xn, w_ref[...], preferred_element_type=F32)
    cq_ref[...] = _rms(h[:, :MLA_Q_LORA], qn_ref[...], RMS_EPS).astype(BF16)
    ckv_ref[...] = _rms(h[:, MLA_Q_LORA:MLA_Q_LORA + MLA_KV_LORA], kvn_ref[...], RMS_EPS).astype(BF16)
    kr = _rope_pair(h[:, MLA_Q_LORA + MLA_KV_LORA:], tab_ref[...])
    lane = lax.broadcasted_iota(jnp.int32, kr.shape, 1)
    kr_ref[...] = jnp.where(lane < MLA_ROPE, kr, 0.0).astype(BF16)


def mla_in(x, g, w_in_ext, q_norm, kv_norm, tab, tm=MLA_IN_TM):
    m, d = x.shape
    n = w_in_ext.shape[1]
    row = lambda i: (i, 0)
    fixed = lambda i: (0, 0)
    return pl.pallas_call(
        _mla_in_kernel,
        grid=(_steps(m, tm),),
        in_specs=[
            pl.BlockSpec((tm, d), row),
            pl.BlockSpec((1, d), fixed),
            pl.BlockSpec((d, n), fixed),
            pl.BlockSpec((1, MLA_Q_LORA), fixed),
            pl.BlockSpec((1, MLA_KV_LORA), fixed),
            pl.BlockSpec((tm, MLA_KR_PAD), row),
        ],
        out_specs=[
            pl.BlockSpec((tm, MLA_Q_LORA), row),
            pl.BlockSpec((tm, MLA_KV_LORA), row),
            pl.BlockSpec((tm, MLA_KR_PAD), row),
        ],
        out_shape=[
            jax.ShapeDtypeStruct((m, MLA_Q_LORA), BF16),
            jax.ShapeDtypeStruct((m, MLA_KV_LORA), BF16),
            jax.ShapeDtypeStruct((m, MLA_KR_PAD), BF16),
        ],
        compiler_params=_params("parallel"),
        name="mla_in",
    )(x, g.reshape(1, d), w_in_ext, q_norm.reshape(1, -1), kv_norm.reshape(1, -1), tab)


def _mla_qt_kernel(c_ref, wt_ref, cos_ref, sin_ref, q_ref, *, heads):
    y = _dot_nt(wt_ref[...], c_ref[...]) * (MLA_SCALE * LOG2E)
    c = cos_ref[...]
    s = sin_ref[...]
    half = MLA_ROPE // 2
    zeros = jnp.zeros((MLA_QK_PAD - MLA_QK, y.shape[1]), BF16)
    for h in range(heads):
        src = h * MLA_QK
        dst = h * MLA_QK_PAD
        q_ref[dst:dst + MLA_NOPE, :] = y[src:src + MLA_NOPE].astype(BF16)
        r1, r2 = _rope_rows(y[src + MLA_NOPE:src + MLA_QK], c, s)
        q_ref[dst + MLA_NOPE:dst + MLA_NOPE + half, :] = r1.astype(BF16)
        q_ref[dst + MLA_NOPE + half:dst + MLA_QK, :] = r2.astype(BF16)
        q_ref[dst + MLA_QK:dst + MLA_QK_PAD, :] = zeros


def mla_qt(cq, w_uq_t, cos_t, sin_t, tm=PROJ_TM, heads_per_step=4):
    m, k = cq.shape
    tn_in = heads_per_step * MLA_QK
    tn_out = heads_per_step * MLA_QK_PAD
    return pl.pallas_call(
        functools.partial(_mla_qt_kernel, heads=heads_per_step),
        grid=(_steps(m, tm), _steps(MLA_HEADS, heads_per_step)),
        in_specs=[
            pl.BlockSpec((tm, k), lambda i, j: (i, 0)),
            pl.BlockSpec((tn_in, k), lambda i, j: (j, 0)),
            pl.BlockSpec((MLA_ROPE // 2, tm), lambda i, j: (0, i)),
            pl.BlockSpec((MLA_ROPE // 2, tm), lambda i, j: (0, i)),
        ],
        out_specs=pl.BlockSpec((tn_out, tm), lambda i, j: (j, i)),
        out_shape=jax.ShapeDtypeStruct((MLA_HEADS * MLA_QK_PAD, m), BF16),
        compiler_params=_params("parallel", "parallel"),
        name="mla_qt",
    )(cq, w_uq_t, cos_t, sin_t)


def _mla_kv_kernel(c_ref, wk_ref, wvt_ref, kr_ref, k_ref, vt_ref, *, heads):
    c = c_ref[...]
    yk = jnp.dot(c, wk_ref[...], preferred_element_type=F32)
    kr = kr_ref[...]
    for h in range(heads):
        dst = h * MLA_QK_PAD
        k_ref[:, dst:dst + MLA_NOPE] = yk[:, h * MLA_NOPE:(h + 1) * MLA_NOPE].astype(BF16)
        k_ref[:, dst + MLA_NOPE:dst + MLA_QK_PAD] = kr
    vt_ref[...] = _dot_nt(wvt_ref[...], c).astype(BF16)


def mla_kv(ckv, w_k, w_v_t, kr, tm=PROJ_TM, heads_per_step=4):
    m, k = ckv.shape
    hp = heads_per_step
    return pl.pallas_call(
        functools.partial(_mla_kv_kernel, heads=hp),
        grid=(_steps(m, tm), _steps(MLA_HEADS, hp)),
        in_specs=[
            pl.BlockSpec((tm, k), lambda i, j: (i, 0)),
            pl.BlockSpec((k, hp * MLA_NOPE), lambda i, j: (0, j)),
            pl.BlockSpec((hp * MLA_V, k), lambda i, j: (j, 0)),
            pl.BlockSpec((tm, MLA_KR_PAD), lambda i, j: (i, 0)),
        ],
        out_specs=[
            pl.BlockSpec((tm, hp * MLA_QK_PAD), lambda i, j: (i, j)),
            pl.BlockSpec((hp * MLA_V, tm), lambda i, j: (j, i)),
        ],
        out_shape=[
            jax.ShapeDtypeStruct((m, MLA_HEADS * MLA_QK_PAD), BF16),
            jax.ShapeDtypeStruct((MLA_HEADS * MLA_V, m), BF16),
        ],
        compiler_params=_params("parallel", "parallel"),
        name="mla_kv",
    )(ckv, w_k, w_v_t, kr)


def _mla_attn_kernel(qt_ref, k_ref, vt_ref, o_ref, sa_ref, sb_ref, acc_ref, *, tk):
    parts, _, tp = sa_ref.shape
    qs = [qt_ref[:, i * tp:(i + 1) * tp] for i in range(parts)]
    ls = _flash_t(qs, lambda k, i: k, k_ref, vt_ref, sa_ref, sb_ref, acc_ref, tk=tk)
    for i in range(parts):
        o_ref[i * tp:(i + 1) * tp, :] = (acc_ref[i] / ls[i]).T.astype(o_ref.dtype)


def mla_attention(qt, k, vt, tq=MLA_TQ, tk=ATTN_TK, parts=2):
    s = k.shape[0]
    return pl.pallas_call(
        functools.partial(_mla_attn_kernel, tk=tk),
        grid=(MLA_HEADS, _steps(s, tq)),
        in_specs=[
            pl.BlockSpec((MLA_QK_PAD, tq), lambda h, i: (h, i)),
            pl.BlockSpec((s, MLA_QK_PAD), lambda h, i: (0, h)),
            pl.BlockSpec((MLA_V, s), lambda h, i: (h, 0)),
        ],
        out_specs=pl.BlockSpec((tq, MLA_V), lambda h, i: (i, h)),
        out_shape=jax.ShapeDtypeStruct((s, MLA_HEADS * MLA_V), BF16),
        scratch_shapes=[
            pltpu.VMEM((parts, tk, tq // parts), F32),
            pltpu.VMEM((parts, tk, tq // parts), F32),
            pltpu.VMEM((parts, MLA_V, tq // parts), F32),
        ],
        compiler_params=_params("parallel", "parallel"),
        name="mla_attention",
    )(qt, k, vt)


def _rot_cols(w):
    half = w.shape[-1] // 2
    return jnp.concatenate([-w[..., half:], w[..., :half]], axis=-1)


def mla_layer(x, mix_norm, w_in, q_norm, w_uq, kv_norm, w_ukv, w_o, tab, cos_t, sin_t):
    w_kr = w_in[:, MLA_Q_LORA + MLA_KV_LORA:]
    w_in_ext = jnp.concatenate([w_in, _rot_cols(w_kr)], axis=1).astype(BF16)
    w_uq_t = w_uq.astype(BF16).T
    wkv = w_ukv.astype(BF16).reshape(MLA_KV_LORA, MLA_HEADS, MLA_NOPE + MLA_V)
    w_k = wkv[..., :MLA_NOPE].reshape(MLA_KV_LORA, -1)
    w_v_t = wkv[..., MLA_NOPE:].reshape(MLA_KV_LORA, -1).T
    cq, ckv, kr = mla_in(x, mix_norm, w_in_ext, q_norm, kv_norm, tab)
    qt = mla_qt(cq, w_uq_t, cos_t, sin_t)
    k, vt = mla_kv(ckv, w_k, w_v_t, kr)
    o = mla_attention(qt, k, vt)
    return matmul_residual(o, w_o.astype(BF16), x)


def _diff_attn_kernel(lam_ref, qt_ref, k_ref, vt_ref, g_ref, o_ref, sa_ref, sb_ref, acc_ref, *, tk, lam_init):
    d = DIFF_HEAD_DIM
    qs = [qt_ref[:d, :], qt_ref[d:, :]]
    l0, l1 = _flash_t(qs, lambda k, i: k[:, i * d:(i + 1) * d], k_ref, vt_ref, sa_ref, sb_ref, acc_ref, tk=tk)
    ot = acc_ref[0] / l0 - lam_ref[0, 0] * (acc_ref[1] / l1)
    o = _rms(ot.T, g_ref[...], DIFF_EPS) * (1.0 - lam_init)
    o_ref[...] = o.astype(o_ref.dtype)


def diff_attention(qt, k, vt, lam, subln, lam_init, tq=DIFF_TQ, tk=ATTN_TK):
    s = k.shape[0]
    hd = 2 * DIFF_HEAD_DIM
    return pl.pallas_call(
        functools.partial(_diff_attn_kernel, tk=tk, lam_init=lam_init),
        grid=(DIFF_HEADS, _steps(s, tq)),
        in_specs=[
            pl.BlockSpec(memory_space=pltpu.SMEM),
            pl.BlockSpec((hd, tq), lambda h, i: (h, i)),
            pl.BlockSpec((s, hd), lambda h, i: (0, h)),
            pl.BlockSpec((hd, s), lambda h, i: (h, 0)),
            pl.BlockSpec((1, hd), lambda h, i: (0, 0)),
        ],
        out_specs=pl.BlockSpec((tq, hd), lambda h, i: (i, h)),
        out_shape=jax.ShapeDtypeStruct((s, DIFF_HEADS * hd), BF16),
        scratch_shapes=[
            pltpu.VMEM((2, tk, tq), F32),
            pltpu.VMEM((2, tk, tq), F32),
            pltpu.VMEM((2, hd, tq), F32),
        ],
        compiler_params=_params("parallel", "parallel"),
        name="diff_attention",
    )(lam, qt, k, vt, subln.reshape(1, hd))


def _lambda_kernel(lq1_ref, lk1_ref, lq2_ref, lk2_ref, o_ref, *, lam_init):
    a = jnp.sum(lq1_ref[...] * lk1_ref[...], axis=-1, keepdims=True)
    b = jnp.sum(lq2_ref[...] * lk2_ref[...], axis=-1, keepdims=True)
    o_ref[...] = jnp.exp(a) - jnp.exp(b) + lam_init


def diff_lambda(lq1, lk1, lq2, lk2, lam_init):
    args = [v.reshape(1, -1) for v in (lq1, lk1, lq2, lk2)]
    return pl.pallas_call(
        functools.partial(_lambda_kernel, lam_init=lam_init),
        out_shape=jax.ShapeDtypeStruct((1, 1), F32),
        name="diff_lambda",
    )(*args)


def diff_layer(x, mix_norm, w_qkv, lq1, lk1, lq2, lk2, subln, w_o, rope, rope_t, layer_idx):
    lam_init = 0.8 - 0.6 * math.exp(-0.3 * layer_idx)
    w = w_qkv.astype(BF16)
    qt = norm_matmul_t(x, mix_norm, w[:, :DIFF_QK].T, BF16, rope_t=rope_t, scale=DIFF_SCALE * LOG2E)
    k = norm_matmul(x, mix_norm, w[:, DIFF_QK:2 * DIFF_QK], BF16, rope=rope)
    vt = norm_matmul_t(x, mix_norm, w[:, 2 * DIFF_QK:].T, BF16)
    lam = diff_lambda(lq1, lk1, lq2, lk2, lam_init)
    o = diff_attention(qt, k, vt, lam, subln, lam_init)
    return matmul_residual(o, w_o.astype(BF16), x)


FFT_R = 128
FFT_N = FFT_R * FFT_R
FFT_G = 8
HY_TC = 512
HY_NQ = HY_TC // LANE


def _dft_constants():
    r = np.arange(FFT_R)
    n2, k1, n1 = r[:, None, None], r[None, :, None], r[None, None, :]
    th = 2.0 * np.pi * ((k1 * (FFT_R * n1 + n2)) % FFT_N) / FFT_N
    w1 = np.concatenate([np.cos(th), -np.sin(th)], axis=1)
    tht = np.transpose(th, (0, 2, 1))
    v1 = np.concatenate([np.cos(tht), -np.sin(tht)], axis=2) / FFT_N
    ph = 2.0 * np.pi * ((r[:, None] * r[None, :]) % FFT_R) / FFT_R
    c, s = np.cos(ph), np.sin(ph)
    g_fwd = np.block([[c, s], [-s, c]])
    g_inv = np.block([[c, -s], [s, c]])
    as_bf16 = lambda a: a.astype(np.float32).astype(BF16)
    half = FFT_R // 2
    return dict(w1_full=as_bf16(w1), w1_half=as_bf16(w1[:, :, :half]), v1_half=as_bf16(v1[:, :half, :]),
                g_fwd=as_bf16(g_fwd), g_inv=as_bf16(g_inv))


def _pack_pair(x):
    h = x.shape[0] // 2
    hi = lax.bitcast_convert_type(x[:h].astype(BF16).astype(F32), jnp.uint32)
    lo = lax.bitcast_convert_type(x[h:].astype(BF16).astype(F32), jnp.uint32)
    return hi | (lo >> 16)


def _unpack_pair(p):
    hi = lax.bitcast_convert_type(p & jnp.uint32(0xFFFF0000), F32)
    lo = lax.bitcast_convert_type(p << 16, F32)
    return jnp.concatenate([hi, lo], axis=0).astype(BF16)


HALO = 16


def _in_proj_conv_kernel(x_ref, prev_ref, next_ref, g_ref, w_ref, cw_ref, cb_ref, o_ref, xn_ref):
    i = pl.program_id(0)
    tm = x_ref.shape[0]

    @pl.when(pl.program_id(1) == 0)
    def _():
        g = g_ref[...]
        before = jnp.where(i > 0, _rms(prev_ref[...], g, RMS_EPS), 0.0)
        after = jnp.where(i < pl.num_programs(0) - 1, _rms(next_ref[...], g, RMS_EPS), 0.0)
        xn_ref[:HALO] = before.astype(BF16)
        xn_ref[HALO:HALO + tm] = _rms(x_ref[...], g, RMS_EPS).astype(BF16)
        xn_ref[HALO + tm:] = after.astype(BF16)

    u = jnp.dot(xn_ref[...], w_ref[...], preferred_element_type=F32)
    y = (u[HALO - 1:HALO - 1 + tm] * cw_ref[0:1, :] + u[HALO:HALO + tm] * cw_ref[1:2, :]
         + u[HALO + 1:HALO + 1 + tm] * cw_ref[2:3, :] + cb_ref[...])
    for q in range(o_ref.shape[0]):
        o_ref[q] = y[:, q * LANE:(q + 1) * LANE]


def in_proj_conv(x, g, w, conv_w, conv_b, tm=PROJ_TM, tn=HY_IN_TN):
    s, k = x.shape
    n = w.shape[1]
    nb = tm // HALO
    return pl.pallas_call(
        _in_proj_conv_kernel,
        grid=(_steps(s, tm), _steps(n, tn)),
        in_specs=[
            pl.BlockSpec((tm, k), lambda i, j: (i, 0)),
            pl.BlockSpec((HALO, k), lambda i, j: (jnp.maximum(i * nb - 1, 0), 0)),
            pl.BlockSpec((HALO, k), lambda i, j: (jnp.minimum((i + 1) * nb, s // HALO - 1), 0)),
            pl.BlockSpec((1, k), lambda i, j: (0, 0)),
            pl.BlockSpec((k, tn), lambda i, j: (0, j)),
            pl.BlockSpec((3, tn), lambda i, j: (0, j)),
            pl.BlockSpec((1, tn), lambda i, j: (0, j)),
        ],
        out_specs=pl.BlockSpec((tn // LANE, tm, LANE), lambda i, j: (j, i, 0)),
        out_shape=jax.ShapeDtypeStruct((n // LANE, s, LANE), F32),
        scratch_shapes=[pltpu.VMEM((tm + 2 * HALO, k), BF16)],
        compiler_params=_params("parallel", "arbitrary"),
        name="hy_in_proj_conv",
    )(x, x, x, g.reshape(1, k), w, conv_w, conv_b.reshape(1, n))


def _filter_mlp_kernel(z_ref, w1_ref, b1_ref, f_ref, w2_ref, b2_ref, w3_ref, b3_ref, o_ref):
    hp = lax.Precision.HIGHEST
    f = f_ref[...]
    h = jnp.sin(f * (jnp.dot(z_ref[...], w1_ref[...], precision=hp, preferred_element_type=F32) + b1_ref[...]))
    h = jnp.sin(f * (jnp.dot(h, w2_ref[...], precision=hp, preferred_element_type=F32) + b2_ref[...]))
    o_ref[...] = jnp.sin(f * (jnp.dot(h, w3_ref[...], precision=hp, preferred_element_type=F32) + b3_ref[...]))


def filter_mlp(zfeat, f_w1, f_b1, f_freq, f_w2, f_b2, f_w3, f_b3, tm=2048):
    n, e = zfeat.shape
    wd = f_w2.shape[0]
    row = lambda i: (i, 0)
    fixed = lambda i: (0, 0)
    vec = lambda v: v.reshape(1, wd)
    return pl.pallas_call(
        _filter_mlp_kernel,
        grid=(_steps(n, tm),),
        in_specs=[pl.BlockSpec((tm, e), row), pl.BlockSpec((e, wd), fixed), pl.BlockSpec((1, wd), fixed),
                  pl.BlockSpec((1, wd), fixed), pl.BlockSpec((wd, wd), fixed), pl.BlockSpec((1, wd), fixed),
                  pl.BlockSpec((wd, wd), fixed), pl.BlockSpec((1, wd), fixed)],
        out_specs=pl.BlockSpec((tm, wd), row),
        out_shape=jax.ShapeDtypeStruct((n, wd), F32),
        compiler_params=_params("parallel"),
        name="hy_filter_mlp",
    )(zfeat, f_w1, vec(f_b1), vec(f_freq), f_w2, vec(f_b2), f_w3, vec(f_b3))


def _rows_by_group(ref):
    t = [jnp.swapaxes(ref[i], 0, 1) for i in range(ref.shape[0])]
    return [jnp.concatenate([ti[r] for ti in t], axis=1) for r in range(FFT_G)]


def _store_by_group(ref, vals):
    for i in range(ref.shape[0]):
        ref[i] = jnp.swapaxes(jnp.stack([v[:, i * LANE:(i + 1) * LANE] for v in vals], axis=0), 0, 1)


def _kern_stage1_kernel(h_ref, w00_ref, w01_ref, w10_ref, w11_ref, dl_ref, w1_ref, kb_ref, nrm_ref, *, seq):
    h_rows = _rows_by_group(h_ref)
    g = pl.program_id(1)
    half = FFT_R // 2
    w_fwd = (w00_ref[...], w01_ref[...])
    w_bwd = (w10_ref[...], w11_ref[...])
    rate = dl_ref[...] * (-1.0 / (seq - 1))
    n1 = lax.broadcasted_iota(jnp.int32, (half, 1), 0)
    packed = [[] for _ in range(HY_ORDER)]
    acc = [jnp.zeros(nrm_ref.shape[1:], F32) for _ in range(HY_ORDER)]
    for r in range(FFT_G):
        n2 = g * FFT_G + r
        hs = h_rows[r].astype(BF16)
        lag_f = FFT_R * n1 + n2
        lag_b = seq - FFT_R * n1 - n2
        decay_f = jnp.exp(lag_f.astype(F32) * rate)
        decay_b = jnp.where(lag_b == seq, 0.0, jnp.exp(lag_b.astype(F32) * rate))
        for o in range(HY_ORDER):
            kf = jnp.dot(hs[:half], w_fwd[o], preferred_element_type=F32) * decay_f
            kb = jnp.dot(hs[half:], w_bwd[o], preferred_element_type=F32) * decay_b
            slab = jnp.concatenate([kf, kb], axis=0)
            acc[o] += jnp.sum(jnp.abs(slab), axis=0, keepdims=True)
            packed[o].append(_pack_pair(jnp.dot(w1_ref[r], slab.astype(BF16), preferred_element_type=F32)))
    for o in range(HY_ORDER):
        _store_by_group(kb_ref.at[o], packed[o])

    @pl.when(g == 0)
    def _():
        for o in range(HY_ORDER):
            nrm_ref[o] = acc[o]

    @pl.when(g > 0)
    def _():
        for o in range(HY_ORDER):
            nrm_ref[o] += acc[o]


def kern_stage1(h3, w4, deltas, w1_full, seq, tc=HY_TC):
    d = deltas.shape[0]
    jn = _steps(d, tc)
    h3v = h3.reshape(1, FFT_R, FFT_R, LANE)
    assert HY_ORDER == 2
    kern = functools.partial(_kern_stage1_kernel, seq=seq)
    w4_spec = lambda direction, order: pl.BlockSpec((LANE, tc), lambda j, g: (0, (direction * HY_ORDER + order) * jn + j))
    return pl.pallas_call(
        kern,
        grid=(jn, FFT_R // FFT_G),
        in_specs=[
            pl.BlockSpec((1, FFT_R, FFT_G, LANE), lambda j, g: (0, 0, g, 0)),
            w4_spec(0, 0), w4_spec(0, 1), w4_spec(1, 0), w4_spec(1, 1),
            pl.BlockSpec((1, tc), lambda j, g: (0, j)),
            pl.BlockSpec((FFT_G, 2 * FFT_R, FFT_R), lambda j, g: (g, 0, 0)),
        ],
        out_specs=[
            pl.BlockSpec((HY_ORDER, HY_NQ, FFT_R, FFT_G, LANE), lambda j, g: (0, j, 0, g, 0)),
            pl.BlockSpec((HY_ORDER, 1, tc), lambda j, g: (0, 0, j)),
        ],
        out_shape=[
            jax.ShapeDtypeStruct((HY_ORDER, d // LANE, FFT_R, FFT_R, LANE), jnp.uint32),
            jax.ShapeDtypeStruct((HY_ORDER, 1, d), F32),
        ],
        compiler_params=_params("parallel", "arbitrary"),
        name="hy_kern_stage1",
    )(h3v, w4, w4, w4, w4, deltas.reshape(1, d), w1_full)


def _join_lanes(ref, idx):
    return jnp.concatenate([ref[(q,) + idx] for q in range(ref.shape[0])], axis=1)


def _kern_stage3_kernel(kb_ref, g_ref, o_ref):
    gm = g_ref[...]
    for r in range(FFT_G):
        b = _unpack_pair(_join_lanes(kb_ref.at[0], (r,)))
        o_ref[0, r] = jnp.dot(gm, b, preferred_element_type=F32).astype(o_ref.dtype)


def kern_stage3(kb, g_fwd, tc=HY_TC):
    d = kb.shape[1] * LANE
    return pl.pallas_call(
        _kern_stage3_kernel,
        grid=(HY_ORDER, FFT_R // FFT_G, _steps(d, tc)),
        in_specs=[
            pl.BlockSpec((1, HY_NQ, FFT_G, FFT_R, LANE), lambda o, kg, j: (o, j, kg, 0, 0)),
            pl.BlockSpec((2 * FFT_R, 2 * FFT_R), lambda o, kg, j: (0, 0)),
        ],
        out_specs=pl.BlockSpec((1, FFT_G, 2 * FFT_R, tc), lambda o, kg, j: (o, kg, 0, j)),
        out_shape=jax.ShapeDtypeStruct((HY_ORDER, FFT_R, 2 * FFT_R, d), BF16),
        compiler_params=_params("parallel", "parallel", "parallel"),
        name="hy_kern_stage3",
    )(kb, g_fwd)


def _sig_stage1_kernel(z_ref, w1_ref, o_ref):
    z_rows = _rows_by_group(z_ref)
    _store_by_group(o_ref, [_pack_pair(jnp.dot(w1_ref[r], z_rows[r].astype(BF16), preferred_element_type=F32))
                            for r in range(FFT_G)])


def sig_stage1(z_src, slab0, d, w1_half, tc=HY_TC):
    s = z_src.shape[1]
    rows = s // FFT_R
    zv = z_src.reshape(z_src.shape[0], rows, FFT_R, LANE)
    return pl.pallas_call(
        _sig_stage1_kernel,
        grid=(FFT_R // FFT_G, _steps(d, tc)),
        in_specs=[
            pl.BlockSpec((HY_NQ, rows, FFT_G, LANE), lambda g, j: (slab0 // HY_NQ + j, 0, g, 0)),
            pl.BlockSpec((FFT_G, 2 * FFT_R, rows), lambda g, j: (g, 0, 0)),
        ],
        out_specs=pl.BlockSpec((HY_NQ, FFT_R, FFT_G, LANE), lambda g, j: (j, 0, g, 0)),
        out_shape=jax.ShapeDtypeStruct((d // LANE, FFT_R, FFT_R, LANE), jnp.uint32),
        compiler_params=_params("parallel", "parallel"),
        name="hy_sig_stage1",
    )(zv, w1_half)


def _sig_stage3_kernel(zb_ref, ks_ref, gf_ref, gi_ref, o_ref):
    gf = gf_ref[...]
    gi = gi_ref[...]
    for r in range(FFT_G):
        x = jnp.dot(gf, _unpack_pair(_join_lanes(zb_ref, (r,))), preferred_element_type=F32)
        ks = ks_ref[0, r].astype(F32)
        xr, xi = x[:FFT_R], x[FFT_R:]
        kr, ki = ks[:FFT_R], ks[FFT_R:]
        y = jnp.concatenate([xr * kr - xi * ki, xr * ki + xi * kr], axis=0).astype(BF16)
        packed = _pack_pair(jnp.dot(gi, y, preferred_element_type=F32))
        for q in range(o_ref.shape[0]):
            o_ref[q, r] = packed[:, q * LANE:(q + 1) * LANE]


def sig_stage3(zb, kspec, order, g_fwd, g_inv, tc=HY_TC):
    d = zb.shape[0] * LANE
    blk = pl.BlockSpec((HY_NQ, FFT_G, FFT_R, LANE), lambda kg, j: (j, kg, 0, 0))
    mat = pl.BlockSpec((2 * FFT_R, 2 * FFT_R), lambda kg, j: (0, 0))
    return pl.pallas_call(
        _sig_stage3_kernel,
        grid=(FFT_R // FFT_G, _steps(d, tc)),
        in_specs=[blk, pl.BlockSpec((1, FFT_G, 2 * FFT_R, tc), lambda kg, j: (order, kg, 0, j)), mat, mat],
        out_specs=blk,
        out_shape=jax.ShapeDtypeStruct(zb.shape, jnp.uint32),
        compiler_params=_params("parallel", "parallel"),
        name="hy_sig_stage3",
    )(zb, kspec, g_fwd, g_inv)


def _sig_inverse1_kernel(yb_ref, v_ref, gate_ref, z_ref, nrm_ref, bias_ref, *rest, next_stage1):
    inv = 1.0 / nrm_ref[0]
    bias = bias_ref[0]
    yb_rows, gate_rows, z_rows = _rows_by_group(yb_ref), _rows_by_group(gate_ref), _rows_by_group(z_ref)
    out = []
    for r in range(FFT_G):
        conv = jnp.dot(v_ref[r], _unpack_pair(yb_rows[r]), preferred_element_type=F32) * inv
        out.append(gate_rows[r] * (conv + z_rows[r] * bias))
    if next_stage1:
        w1_ref, o_ref, zb_ref = rest
        _store_by_group(zb_ref, [_pack_pair(jnp.dot(w1_ref[r], out[r].astype(BF16), preferred_element_type=F32))
                                 for r in range(FFT_G)])
    else:
        o_ref, = rest
    _store_by_group(o_ref, out)


def sig_inverse1(yb, v1_half, gate_src, gate_slab0, z_src, z_slab0, nrm, bias, order, w1_half=None, tc=HY_TC):
    d = yb.shape[0] * LANE
    s = gate_src.shape[1]
    rows = s // FFT_R
    view = lambda a: a.reshape(a.shape[0], rows, FFT_R, LANE)
    sig = (HY_NQ, rows, FFT_G, LANE)
    freq = (HY_NQ, FFT_R, FFT_G, LANE)
    in_specs = [
        pl.BlockSpec(freq, lambda g, j: (j, 0, g, 0)),
        pl.BlockSpec((FFT_G, rows, 2 * FFT_R), lambda g, j: (g, 0, 0)),
        pl.BlockSpec(sig, lambda g, j: (gate_slab0 // HY_NQ + j, 0, g, 0)),
        pl.BlockSpec(sig, lambda g, j: (z_slab0 // HY_NQ + j, 0, g, 0)),
        pl.BlockSpec((1, 1, tc), lambda g, j: (order, 0, j)),
        pl.BlockSpec((1, 1, tc), lambda g, j: (order, 0, j)),
    ]
    args = [yb, v1_half, view(gate_src), view(z_src), nrm, bias.reshape(HY_ORDER, 1, d)]
    out_specs = [pl.BlockSpec(sig, lambda g, j: (j, 0, g, 0))]
    out_shape = [jax.ShapeDtypeStruct((d // LANE, rows, FFT_R, LANE), F32)]
    if w1_half is not None:
        in_specs.append(pl.BlockSpec((FFT_G, 2 * FFT_R, rows), lambda g, j: (g, 0, 0)))
        args.append(w1_half)
        out_specs.append(pl.BlockSpec(freq, lambda g, j: (j, 0, g, 0)))
        out_shape.append(jax.ShapeDtypeStruct((d // LANE, FFT_R, FFT_R, LANE), jnp.uint32))
    outs = pl.pallas_call(
        functools.partial(_sig_inverse1_kernel, next_stage1=w1_half is not None),
        grid=(FFT_R // FFT_G, _steps(d, tc)),
        in_specs=in_specs,
        out_specs=out_specs,
        out_shape=out_shape,
        compiler_params=_params("parallel", "parallel"),
        name="hy_sig_inverse1",
    )(*args)
    z_new = outs[0].reshape(d // LANE, s, LANE)
    return (z_new, outs[1]) if w1_half is not None else (z_new, None)


def _hyena_positions(seq):
    t = jnp.linspace(0.0, 1.0, seq, dtype=F32)[:, None]
    bands = (HY_EMB_DIM - 1) // 2
    w = 2.0 * math.pi * jnp.arange(seq, dtype=F32)[:, None] / seq
    f = jnp.linspace(1e-4, bands - 1, bands, dtype=F32)[None, :]
    z = jnp.concatenate([t, jnp.cos(f * w), -jnp.sin(f * w)], axis=-1)
    return jnp.pad(z, ((0, 0), (0, LANE - HY_EMB_DIM)))


def hyena_mixer(u, f_w1, f_b1, f_freq, f_w2, f_b2, f_w3, f_b3, f_w4, bias):
    s = u.shape[1]
    d = u.shape[0] * LANE // 3
    assert 2 * s == FFT_N and d % HY_TC == 0
    c = _dft_constants()
    h3 = filter_mlp(_hyena_positions(s), jnp.pad(f_w1, ((0, LANE - HY_EMB_DIM), (0, 0))), f_b1, f_freq, f_w2, f_b2, f_w3, f_b3)
    width = h3.shape[1]
    h3 = jnp.concatenate([h3, h3[:1], h3[:0:-1]], axis=0)
    h3 = jnp.pad(h3, ((0, 0), (0, LANE - width)))
    w4 = jnp.pad(f_w4.astype(BF16), ((0, LANE - width), (0, 0)))
    deltas = jnp.abs(jnp.linspace(HY_MIN_DECAY, HY_MAX_DECAY, d, dtype=F32))
    kb, nrm = kern_stage1(h3, w4, deltas, c["w1_full"], s)
    kspec = kern_stage3(kb, c["g_fwd"])
    nslab = d // LANE
    z_src, z_slab0 = u, 2 * nslab
    zb = sig_stage1(z_src, z_slab0, d, c["w1_half"])
    for o in range(HY_ORDER):
        yb = sig_stage3(zb, kspec, o, c["g_fwd"], c["g_inv"])
        w1_next = c["w1_half"] if o + 1 < HY_ORDER else None
        z_src, zb = sig_inverse1(yb, c["v1_half"], u, o * nslab, z_src, z_slab0, nrm, bias, o, w1_next)
        z_slab0 = 0
    return z_src


def hyena_layer(x, mix_norm, w_in, conv_w, conv_b, f_w1, f_b1, f_freq, f_w2, f_b2, f_w3, f_b3, f_w4, bias, w_out):
    u = in_proj_conv(x, mix_norm, w_in.astype(BF16), conv_w, conv_b)
    z = hyena_mixer(u, f_w1, f_b1, f_freq, f_w2, f_b2, f_w3, f_b3, f_w4, bias)
    return matmul_residual(z, w_out.astype(BF16), x, a_in_slabs=True)


def _rope_tables(seq, dim):
    inv = 1.0 / (ROPE_THETA ** (jnp.arange(0, dim, 2, dtype=F32) / dim))
    ang = jnp.arange(seq, dtype=F32)[:, None] * inv[None, :]
    return jnp.cos(ang), jnp.sin(ang)


def kernel(x, l0_mix_norm, l0_mla_w_in, l0_mla_q_norm, l0_mla_w_uq, l0_mla_kv_norm, l0_mla_w_ukv, l0_mla_w_o, l0_ffn_norm, l0_ffn_w_gate, l0_ffn_w_up, l0_ffn_w_down, l1_mix_norm, l1_hy_w_in, l1_hy_conv_w, l1_hy_conv_b, l1_hy_f_w1, l1_hy_f_b1, l1_hy_f_freq, l1_hy_f_w2, l1_hy_f_b2, l1_hy_f_w3, l1_hy_f_b3, l1_hy_f_w4, l1_hy_bias, l1_hy_w_out, l1_ffn_norm, l1_ffn_w_gate, l1_ffn_w_up, l1_ffn_w_down, l2_mix_norm, l2_da_w_qkv, l2_da_lq1, l2_da_lk1, l2_da_lq2, l2_da_lk2, l2_da_subln, l2_da_w_o, l2_ffn_norm, l2_ffn_w_gate, l2_ffn_w_up, l2_ffn_w_down, l3_mix_norm, l3_mla_w_in, l3_mla_q_norm, l3_mla_w_uq, l3_mla_kv_norm, l3_mla_w_ukv, l3_mla_w_o, l3_ffn_norm, l3_ffn_w_gate, l3_ffn_w_up, l3_ffn_w_down, final_norm):
    b, s, d = x.shape
    assert b == 1
    h = x.reshape(s, d)

    mc, ms = _rope_tables(s, MLA_ROPE)
    mla_tab = jnp.concatenate([mc, mc, ms, ms], axis=-1)
    mla = (mla_tab, mc.T, ms.T)
    dc, ds = _rope_tables(s, DIFF_HEAD_DIM)
    diff_rope = (jnp.concatenate([dc, dc], axis=-1), jnp.concatenate([-ds, ds], axis=-1))
    diff_rope_t = (dc.T, ds.T)

    def ffn(h, g, wg, wu, wd, out_norm_gain=None):
        return ffn_block(h, g, wg.astype(BF16), wu.astype(BF16), wd.astype(BF16), out_norm_gain)

    h = mla_layer(h, l0_mix_norm, l0_mla_w_in, l0_mla_q_norm, l0_mla_w_uq, l0_mla_kv_norm, l0_mla_w_ukv, l0_mla_w_o, *mla)
    h = ffn(h, l0_ffn_norm, l0_ffn_w_gate, l0_ffn_w_up, l0_ffn_w_down)
    h = hyena_layer(h, l1_mix_norm, l1_hy_w_in, l1_hy_conv_w, l1_hy_conv_b, l1_hy_f_w1, l1_hy_f_b1, l1_hy_f_freq, l1_hy_f_w2, l1_hy_f_b2, l1_hy_f_w3, l1_hy_f_b3, l1_hy_f_w4, l1_hy_bias, l1_hy_w_out)
    h = ffn(h, l1_ffn_norm, l1_ffn_w_gate, l1_ffn_w_up, l1_ffn_w_down)
    h = diff_layer(h, l2_mix_norm, l2_da_w_qkv, l2_da_lq1, l2_da_lk1, l2_da_lq2, l2_da_lk2, l2_da_subln, l2_da_w_o, diff_rope, diff_rope_t, 2)
    h = ffn(h, l2_ffn_norm, l2_ffn_w_gate, l2_ffn_w_up, l2_ffn_w_down)
    h = mla_layer(h, l3_mix_norm, l3_mla_w_in, l3_mla_q_norm, l3_mla_w_uq, l3_mla_kv_norm, l3_mla_w_ukv, l3_mla_w_o, *mla)
    h = ffn(h, l3_ffn_norm, l3_ffn_w_gate, l3_ffn_w_up, l3_ffn_w_down, final_norm)
    return h.reshape(b, s, d)
```

```python
import functools
import math

import jax
import jax.numpy as jnp
import numpy as np
from jax import lax
from jax.experimental import pallas as pl
from jax.experimental.pallas import tpu as pltpu

F32 = jnp.float32
BF16 = jnp.bfloat16

D_MODEL = 2048
RMS_EPS = 1e-6
ROPE_THETA = 10000.0
LOG2E = math.log2(math.e)

MLA_HEADS = 16
MLA_Q_LORA = 768
MLA_KV_LORA = 512
MLA_NOPE = 128
MLA_ROPE = 64
MLA_V = 128
MLA_QK = MLA_NOPE + MLA_ROPE
MLA_SCALE = MLA_QK ** -0.5
MLA_QK_PAD = 256

HY_ORDER = 2
HY_EMB_DIM = 33
HY_TARGET = 1e-2
HY_MAX_DECAY = math.log(HY_TARGET) / 0.3
HY_MIN_DECAY = math.log(HY_TARGET) / 1.5

DIFF_HEAD_DIM = 128
DIFF_HEADS = D_MODEL // (2 * DIFF_HEAD_DIM)
DIFF_QK = DIFF_HEADS * 2 * DIFF_HEAD_DIM
DIFF_SCALE = DIFF_HEAD_DIM ** -0.5
DIFF_EPS = 1e-5

LANE = 128
VMEM_BYTES = 64 * 1024 * 1024
VMEM_LIMIT = VMEM_BYTES * 7 // 8
NEG_BIG = -1e30

PROJ_TM, PROJ_TN = 1024, 1024
MLA_IN_TM = 512
FFN_TM, FFN_TF = 1024, 512
MLA_TQ, DIFF_TQ, ATTN_TK = 4096, 2048, 512
HY_IN_TN = 1024


def _params(*sem):
    return pltpu.CompilerParams(dimension_semantics=sem, vmem_limit_bytes=VMEM_LIMIT)


def _steps(dim, tile):
    assert dim % tile == 0, f"tile {tile} does not divide {dim}"
    return dim // tile


def _rms(x, g, eps):
    return x * lax.rsqrt(jnp.mean(x * x, axis=-1, keepdims=True) + eps) * g


def _dot_nt(a, b):
    return lax.dot_general(a, b, (((1,), (1,)), ((), ())), preferred_element_type=F32)


def _rope_rows(y, c, s):
    h = c.shape[0]
    y1, y2 = y[:h], y[h:]
    return y1 * c - y2 * s, y2 * c + y1 * s


def _norm_matmul_kernel(x_ref, g_ref, w_ref, *rest, rope, scale):
    o_ref, xn_ref = rest[-2:]

    @pl.when(pl.program_id(1) == 0)
    def _():
        xn_ref[...] = _rms(x_ref[...], g_ref[...], RMS_EPS).astype(BF16)

    y = jnp.dot(xn_ref[...], w_ref[...], preferred_element_type=F32)
    if rope:
        c = rest[0][...] * scale
        s = rest[1][...] * scale
        hd = c.shape[1]
        for g in range(y.shape[1] // hd):
            t = y[:, g * hd:(g + 1) * hd]
            o_ref[:, g * hd:(g + 1) * hd] = (t * c + pltpu.roll(t, hd // 2, axis=1) * s).astype(o_ref.dtype)
    else:
        o_ref[...] = (y * scale).astype(o_ref.dtype)


def norm_matmul(x, g, w, out_dtype, rope=None, scale=1.0, tm=PROJ_TM, tn=PROJ_TN):
    m, k = x.shape
    n = w.shape[1]
    in_specs = [
        pl.BlockSpec((tm, k), lambda i, j: (i, 0)),
        pl.BlockSpec((1, k), lambda i, j: (0, 0)),
        pl.BlockSpec((k, tn), lambda i, j: (0, j)),
    ]
    args = [x, g.reshape(1, k), w]
    if rope is not None:
        in_specs += [pl.BlockSpec((tm, rope[0].shape[1]), lambda i, j: (i, 0))] * 2
        args += list(rope)
    return pl.pallas_call(
        functools.partial(_norm_matmul_kernel, rope=rope is not None, scale=scale),
        grid=(_steps(m, tm), _steps(n, tn)),
        in_specs=in_specs,
        out_specs=pl.BlockSpec((tm, tn), lambda i, j: (i, j)),
        out_shape=jax.ShapeDtypeStruct((m, n), out_dtype),
        scratch_shapes=[pltpu.VMEM((tm, k), BF16)],
        compiler_params=_params("parallel", "arbitrary"),
        name="norm_matmul",
    )(*args)


def _norm_matmul_t_kernel(x_ref, g_ref, wt_ref, *rest, rope, scale):
    o_ref, xn_ref = rest[-2:]

    @pl.when(pl.program_id(1) == 0)
    def _():
        xn_ref[...] = _rms(x_ref[...], g_ref[...], RMS_EPS).astype(BF16)

    y = _dot_nt(wt_ref[...], xn_ref[...]) * scale
    if rope:
        c = rest[0][...]
        s = rest[1][...]
        half = c.shape[0]
        for g in range(y.shape[0] // (2 * half)):
            r1, r2 = _rope_rows(y[2 * g * half:2 * (g + 1) * half], c, s)
            o_ref[2 * g * half:(2 * g + 1) * half, :] = r1.astype(o_ref.dtype)
            o_ref[(2 * g + 1) * half:2 * (g + 1) * half, :] = r2.astype(o_ref.dtype)
    else:
        o_ref[...] = y.astype(o_ref.dtype)


def norm_matmul_t(x, g, wt, out_dtype, rope_t=None, scale=1.0, tm=PROJ_TM, tn=PROJ_TN):
    m, k = x.shape
    n = wt.shape[0]
    in_specs = [
        pl.BlockSpec((tm, k), lambda i, j: (i, 0)),
        pl.BlockSpec((1, k), lambda i, j: (0, 0)),
        pl.BlockSpec((tn, k), lambda i, j: (j, 0)),
    ]
    args = [x, g.reshape(1, k), wt]
    if rope_t is not None:
        in_specs += [pl.BlockSpec((rope_t[0].shape[0], tm), lambda i, j: (0, i))] * 2
        args += list(rope_t)
    return pl.pallas_call(
        functools.partial(_norm_matmul_t_kernel, rope=rope_t is not None, scale=scale),
        grid=(_steps(m, tm), _steps(n, tn)),
        in_specs=in_specs,
        out_specs=pl.BlockSpec((tn, tm), lambda i, j: (j, i)),
        out_shape=jax.ShapeDtypeStruct((n, m), out_dtype),
        scratch_shapes=[pltpu.VMEM((tm, k), BF16)],
        compiler_params=_params("parallel", "arbitrary"),
        name="norm_matmul_t",
    )(*args)


def _matmul_res_kernel(a_ref, w_ref, r_ref, o_ref):
    if len(a_ref.shape) == 3:
        a = jnp.concatenate([a_ref[q] for q in range(a_ref.shape[0])], axis=1)
    else:
        a = a_ref[...]
    o_ref[...] = r_ref[...] + jnp.dot(a.astype(BF16), w_ref[...], preferred_element_type=F32)


def matmul_residual(a, w, res, a_in_slabs=False, tm=PROJ_TM, tn=PROJ_TN):
    k, n = w.shape
    m = res.shape[0]
    if a_in_slabs:
        a_spec = pl.BlockSpec((k // LANE, tm, LANE), lambda i, j: (0, i, 0))
    else:
        a_spec = pl.BlockSpec((tm, k), lambda i, j: (i, 0))
    return pl.pallas_call(
        _matmul_res_kernel,
        grid=(_steps(m, tm), _steps(n, tn)),
        in_specs=[
            a_spec,
            pl.BlockSpec((k, tn), lambda i, j: (0, j)),
            pl.BlockSpec((tm, tn), lambda i, j: (i, j)),
        ],
        out_specs=pl.BlockSpec((tm, tn), lambda i, j: (i, j)),
        out_shape=jax.ShapeDtypeStruct((m, n), F32),
        compiler_params=_params("parallel", "parallel"),
        name="matmul_residual",
    )(a, w, res)


def _ffn_kernel(x_ref, g_ref, wg_ref, wu_ref, wd_ref, *rest, out_norm):
    o_ref, xn_ref = rest[-2:]

    @pl.when(pl.program_id(1) == 0)
    def _():
        x = x_ref[...]
        xn_ref[...] = _rms(x, g_ref[...], RMS_EPS).astype(BF16)
        o_ref[...] = x

    xn = xn_ref[...]
    a = jnp.dot(xn, wg_ref[...], preferred_element_type=F32)
    b = jnp.dot(xn, wu_ref[...], preferred_element_type=F32)
    h = (a * jax.nn.sigmoid(a) * b).astype(BF16)
    o_ref[...] += jnp.dot(h, wd_ref[...], preferred_element_type=F32)

    if out_norm:
        @pl.when(pl.program_id(1) == pl.num_programs(1) - 1)
        def _():
            o_ref[...] = _rms(o_ref[...], rest[0][...], RMS_EPS)


def ffn_block(x, g, wg, wu, wd, out_norm_gain=None, tm=FFN_TM, tf=FFN_TF):
    m, d = x.shape
    f = wg.shape[1]
    vec = pl.BlockSpec((1, d), lambda i, j: (0, 0))
    in_specs = [
        pl.BlockSpec((tm, d), lambda i, j: (i, 0)),
        vec,
        pl.BlockSpec((d, tf), lambda i, j: (0, j)),
        pl.BlockSpec((d, tf), lambda i, j: (0, j)),
        pl.BlockSpec((tf, d), lambda i, j: (j, 0)),
    ]
    args = [x, g.reshape(1, d), wg, wu, wd]
    if out_norm_gain is not None:
        in_specs.append(vec)
        args.append(out_norm_gain.reshape(1, d))
    return pl.pallas_call(
        functools.partial(_ffn_kernel, out_norm=out_norm_gain is not None),
        grid=(_steps(m, tm), _steps(f, tf)),
        in_specs=in_specs,
        out_specs=pl.BlockSpec((tm, d), lambda i, j: (i, 0)),
        out_shape=jax.ShapeDtypeStruct((m, d), F32),
        scratch_shapes=[pltpu.VMEM((tm, d), BF16)],
        compiler_params=_params("parallel", "arbitrary"),
        name="ffn_block",
    )(*args)


def _flash_t(qs, k_of_part, k_ref, vt_ref, sa_ref, sb_ref, acc_ref, *, tk):
    parts = len(qs)
    tp = qs[0].shape[1]
    nk = k_ref.shape[0] // tk
    assert nk % 2 == 0 and nk >= 4

    def chunk(c):
        return pl.ds(c * tk if isinstance(c, int) else pl.multiple_of(c * tk, tk), tk)

    def scores(c, s_ref):
        k = k_ref[chunk(c), :]
        mc = ()
        for i in range(parts):
            st = jnp.dot(k_of_part(k, i), qs[i], preferred_element_type=F32)
            s_ref[i] = st
            mc += (jnp.max(st, axis=0, keepdims=True),)
        return mc

    def softmax_pv(c, s_ref, mc, state):
        vt = vt_ref[:, chunk(c)]
        new = ()
        for i in range(parts):
            m, l = state[2 * i:2 * i + 2]
            m_new = jnp.maximum(m, mc[i])
            alpha = jnp.exp2(m - m_new)
            p = jnp.exp2(s_ref[i] - m_new)
            l = alpha * l + jnp.sum(p, axis=0, keepdims=True)
            acc_ref[i] = alpha * acc_ref[i] + jnp.dot(vt, p.astype(BF16), preferred_element_type=F32)
            new += (m_new, l)
        return new

    def body(c2, carry):
        mc_a, state = carry[:parts], carry[parts:]
        c = 2 * c2
        mc_b = scores(c + 1, sb_ref)
        state = softmax_pv(c, sa_ref, mc_a, state)
        mc_a = scores(c + 2, sa_ref)
        state = softmax_pv(c + 1, sb_ref, mc_b, state)
        return mc_a + state

    acc_ref[...] = jnp.zeros(acc_ref.shape, F32)
    state0 = (jnp.full((1, tp), NEG_BIG, F32), jnp.zeros((1, tp), F32)) * parts
    carry = lax.fori_loop(0, nk // 2 - 1, body, scores(0, sa_ref) + state0)
    mc_a, state = carry[:parts], carry[parts:]
    mc_b = scores(nk - 1, sb_ref)
    state = softmax_pv(nk - 2, sa_ref, mc_a, state)
    state = softmax_pv(nk - 1, sb_ref, mc_b, state)
    return state[1::2]


MLA_KR_PAD = MLA_QK_PAD - MLA_NOPE


def _rope_pair(t, tab):
    t = t * tab
    return t + pltpu.roll(t, MLA_ROPE, axis=1)


def _mla_in_kernel(x_ref, g_ref, w_ref, qn_ref, kvn_ref, tab_ref, cq_ref, ckv_ref, kr_ref):
    xn = _rms(x_ref[...], g_ref[...], RMS_EPS).astype(BF16)
    h = jnp.dot(xn, w_ref[...], preferred_element_type=F32)
    cq_ref[...] = _rms(h[:, :MLA_Q_LORA], qn_ref[...], RMS_EPS).astype(BF16)
    ckv_ref[...] = _rms(h[:, MLA_Q_LORA:MLA_Q_LORA + MLA_KV_LORA], kvn_ref[...], RMS_EPS).astype(BF16)
    kr = _rope_pair(h[:, MLA_Q_LORA + MLA_KV_LORA:], tab_ref[...])
    lane = lax.broadcasted_iota(jnp.int32, kr.shape, 1)
    kr_ref[...] = jnp.where(lane < MLA_ROPE, kr, 0.0).astype(BF16)


def mla_in(x, g, w_in_ext, q_norm, kv_norm, tab, tm=MLA_IN_TM):
    m, d = x.shape
    n = w_in_ext.shape[1]
    row = lambda i: (i, 0)
    fixed = lambda i: (0, 0)
    return pl.pallas_call(
        _mla_in_kernel,
        grid=(_steps(m, tm),),
        in_specs=[
            pl.BlockSpec((tm, d), row),
            pl.BlockSpec((1, d), fixed),
            pl.BlockSpec((d, n), fixed),
            pl.BlockSpec((1, MLA_Q_LORA), fixed),
            pl.BlockSpec((1, MLA_KV_LORA), fixed),
            pl.BlockSpec((tm, MLA_KR_PAD), row),
        ],
        out_specs=[
            pl.BlockSpec((tm, MLA_Q_LORA), row),
            pl.BlockSpec((tm, MLA_KV_LORA), row),
            pl.BlockSpec((tm, MLA_KR_PAD), row),
        ],
        out_shape=[
            jax.ShapeDtypeStruct((m, MLA_Q_LORA), BF16),
            jax.ShapeDtypeStruct((m, MLA_KV_LORA), BF16),
            jax.ShapeDtypeStruct((m, MLA_KR_PAD), BF16),
        ],
        compiler_params=_params("parallel"),
        name="mla_in",
    )(x, g.reshape(1, d), w_in_ext, q_norm.reshape(1, -1), kv_norm.reshape(1, -1), tab)


def _mla_qt_kernel(c_ref, wt_ref, cos_ref, sin_ref, q_ref, *, heads):
    y = _dot_nt(wt_ref[...], c_ref[...]) * (MLA_SCALE * LOG2E)
    c = cos_ref[...]
    s = sin_ref[...]
    half = MLA_ROPE // 2
    zeros = jnp.zeros((MLA_QK_PAD - MLA_QK, y.shape[1]), BF16)
    for h in range(heads):
        src = h * MLA_QK
        dst = h * MLA_QK_PAD
        q_ref[dst:dst + MLA_NOPE, :] = y[src:src + MLA_NOPE].astype(BF16)
        r1, r2 = _rope_rows(y[src + MLA_NOPE:src + MLA_QK], c, s)
        q_ref[dst + MLA_NOPE:dst + MLA_NOPE + half, :] = r1.astype(BF16)
        q_ref[dst + MLA_NOPE + half:dst + MLA_QK, :] = r2.astype(BF16)
        q_ref[dst + MLA_QK:dst + MLA_QK_PAD, :] = zeros


def mla_qt(cq, w_uq_t, cos_t, sin_t, tm=PROJ_TM, heads_per_step=4):
    m, k = cq.shape
    tn_in = heads_per_step * MLA_QK
    tn_out = heads_per_step * MLA_QK_PAD
    return pl.pallas_call(
        functools.partial(_mla_qt_kernel, heads=heads_per_step),
        grid=(_steps(m, tm), _steps(MLA_HEADS, heads_per_step)),
        in_specs=[
            pl.BlockSpec((tm, k), lambda i, j: (i, 0)),
            pl.BlockSpec((tn_in, k), lambda i, j: (j, 0)),
            pl.BlockSpec((MLA_ROPE // 2, tm), lambda i, j: (0, i)),
            pl.BlockSpec((MLA_ROPE // 2, tm), lambda i, j: (0, i)),
        ],
        out_specs=pl.BlockSpec((tn_out, tm), lambda i, j: (j, i)),
        out_shape=jax.ShapeDtypeStruct((MLA_HEADS * MLA_QK_PAD, m), BF16),
        compiler_params=_params("parallel", "parallel"),
        name="mla_qt",
    )(cq, w_uq_t, cos_t, sin_t)


def _mla_kv_kernel(c_ref, wk_ref, wvt_ref, kr_ref, k_ref, vt_ref, *, heads):
    c = c_ref[...]
    yk = jnp.dot(c, wk_ref[...], preferred_element_type=F32)
    kr = kr_ref[...]
    for h in range(heads):
        dst = h * MLA_QK_PAD
        k_ref[:, dst:dst + MLA_NOPE] = yk[:, h * MLA_NOPE:(h + 1) * MLA_NOPE].astype(BF16)
        k_ref[:, dst + MLA_NOPE:dst + MLA_QK_PAD] = kr
    vt_ref[...] = _dot_nt(wvt_ref[...], c).astype(BF16)


def mla_kv(ckv, w_k, w_v_t, kr, tm=PROJ_TM, heads_per_step=4):
    m, k = ckv.shape
    hp = heads_per_step
    return pl.pallas_call(
        functools.partial(_mla_kv_kernel, heads=hp),
        grid=(_steps(m, tm), _steps(MLA_HEADS, hp)),
        in_specs=[
            pl.BlockSpec((tm, k), lambda i, j: (i, 0)),
            pl.BlockSpec((k, hp * MLA_NOPE), lambda i, j: (0, j)),
            pl.BlockSpec((hp * MLA_V, k), lambda i, j: (j, 0)),
            pl.BlockSpec((tm, MLA_KR_PAD), lambda i, j: (i, 0)),
        ],
        out_specs=[
            pl.BlockSpec((tm, hp * MLA_QK_PAD), lambda i, j: (i, j)),
            pl.BlockSpec((hp * MLA_V, tm), lambda i, j: (j, i)),
        ],
        out_shape=[
            jax.ShapeDtypeStruct((m, MLA_HEADS * MLA_QK_PAD), BF16),
            jax.ShapeDtypeStruct((MLA_HEADS * MLA_V, m), BF16),
        ],
        compiler_params=_params("parallel", "parallel"),
        name="mla_kv",
    )(ckv, w_k, w_v_t, kr)


def _mla_attn_kernel(qt_ref, k_ref, vt_ref, o_ref, sa_ref, sb_ref, acc_ref, *, tk):
    parts, _, tp = sa_ref.shape
    qs = [qt_ref[:, i * tp:(i + 1) * tp] for i in range(parts)]
    ls = _flash_t(qs, lambda k, i: k, k_ref, vt_ref, sa_ref, sb_ref, acc_ref, tk=tk)
    for i in range(parts):
        o_ref[i * tp:(i + 1) * tp, :] = (acc_ref[i] / ls[i]).T.astype(o_ref.dtype)


def mla_attention(qt, k, vt, tq=MLA_TQ, tk=ATTN_TK, parts=2):
    s = k.shape[0]
    return pl.pallas_call(
        functools.partial(_mla_attn_kernel, tk=tk),
        grid=(MLA_HEADS, _steps(s, tq)),
        in_specs=[
            pl.BlockSpec((MLA_QK_PAD, tq), lambda h, i: (h, i)),
            pl.BlockSpec((s, MLA_QK_PAD), lambda h, i: (0, h)),
            pl.BlockSpec((MLA_V, s), lambda h, i: (h, 0)),
        ],
        out_specs=pl.BlockSpec((tq, MLA_V), lambda h, i: (i, h)),
        out_shape=jax.ShapeDtypeStruct((s, MLA_HEADS * MLA_V), BF16),
        scratch_shapes=[
            pltpu.VMEM((parts, tk, tq // parts), F32),
            pltpu.VMEM((parts, tk, tq // parts), F32),
            pltpu.VMEM((parts, MLA_V, tq // parts), F32),
        ],
        compiler_params=_params("parallel", "parallel"),
        name="mla_attention",
    )(qt, k, vt)


def _rot_cols(w):
    half = w.shape[-1] // 2
    return jnp.concatenate([-w[..., half:], w[..., :half]], axis=-1)


def mla_layer(x, mix_norm, w_in, q_norm, w_uq, kv_norm, w_ukv, w_o, tab, cos_t, sin_t):
    w_kr = w_in[:, MLA_Q_LORA + MLA_KV_LORA:]
    w_in_ext = jnp.concatenate([w_in, _rot_cols(w_kr)], axis=1).astype(BF16)
    w_uq_t = w_uq.astype(BF16).T
    wkv = w_ukv.astype(BF16).reshape(MLA_KV_LORA, MLA_HEADS, MLA_NOPE + MLA_V)
    w_k = wkv[..., :MLA_NOPE].reshape(MLA_KV_LORA, -1)
    w_v_t = wkv[..., MLA_NOPE:].reshape(MLA_KV_LORA, -1).T
    cq, ckv, kr = mla_in(x, mix_norm, w_in_ext, q_norm, kv_norm, tab)
    qt = mla_qt(cq, w_uq_t, cos_t, sin_t)
    k, vt = mla_kv(ckv, w_k, w_v_t, kr)
    o = mla_attention(qt, k, vt)
    return matmul_residual(o, w_o.astype(BF16), x)


def _diff_attn_kernel(lam_ref, qt_ref, k_ref, vt_ref, g_ref, o_ref, sa_ref, sb_ref, acc_ref, *, tk, lam_init):
    d = DIFF_HEAD_DIM
    qs = [qt_ref[:d, :], qt_ref[d:, :]]
    l0, l1 = _flash_t(qs, lambda k, i: k[:, i * d:(i + 1) * d], k_ref, vt_ref, sa_ref, sb_ref, acc_ref, tk=tk)
    ot = acc_ref[0] / l0 - lam_ref[0, 0] * (acc_ref[1] / l1)
    o = _rms(ot.T, g_ref[...], DIFF_EPS) * (1.0 - lam_init)
    o_ref[...] = o.astype(o_ref.dtype)


def diff_attention(qt, k, vt, lam, subln, lam_init, tq=DIFF_TQ, tk=ATTN_TK):
    s = k.shape[0]
    hd = 2 * DIFF_HEAD_DIM
    return pl.pallas_call(
        functools.partial(_diff_attn_kernel, tk=tk, lam_init=lam_init),
        grid=(DIFF_HEADS, _steps(s, tq)),
        in_specs=[
            pl.BlockSpec(memory_space=pltpu.SMEM),
            pl.BlockSpec((hd, tq), lambda h, i: (h, i)),
            pl.BlockSpec((s, hd), lambda h, i: (0, h)),
            pl.BlockSpec((hd, s), lambda h, i: (h, 0)),
            pl.BlockSpec((1, hd), lambda h, i: (0, 0)),
        ],
        out_specs=pl.BlockSpec((tq, hd), lambda h, i: (i, h)),
        out_shape=jax.ShapeDtypeStruct((s, DIFF_HEADS * hd), BF16),
        scratch_shapes=[
            pltpu.VMEM((2, tk, tq), F32),
            pltpu.VMEM((2, tk, tq), F32),
            pltpu.VMEM((2, hd, tq), F32),
        ],
        compiler_params=_params("parallel", "parallel"),
        name="diff_attention",
    )(lam, qt, k, vt, subln.reshape(1, hd))


def _lambda_kernel(lq1_ref, lk1_ref, lq2_ref, lk2_ref, o_ref, *, lam_init):
    a = jnp.sum(lq1_ref[...] * lk1_ref[...], axis=-1, keepdims=True)
    b = jnp.sum(lq2_ref[...] * lk2_ref[...], axis=-1, keepdims=True)
    o_ref[...] = jnp.exp(a) - jnp.exp(b) + lam_init


def diff_lambda(lq1, lk1, lq2, lk2, lam_init):
    args = [v.reshape(1, -1) for v in (lq1, lk1, lq2, lk2)]
    return pl.pallas_call(
        functools.partial(_lambda_kernel, lam_init=lam_init),
        out_shape=jax.ShapeDtypeStruct((1, 1), F32),
        name="diff_lambda",
    )(*args)


def diff_layer(x, mix_norm, w_qkv, lq1, lk1, lq2, lk2, subln, w_o, rope, rope_t, layer_idx):
    lam_init = 0.8 - 0.6 * math.exp(-0.3 * layer_idx)
    w = w_qkv.astype(BF16)
    qt = norm_matmul_t(x, mix_norm, w[:, :DIFF_QK].T, BF16, rope_t=rope_t, scale=DIFF_SCALE * LOG2E)
    k = norm_matmul(x, mix_norm, w[:, DIFF_QK:2 * DIFF_QK], BF16, rope=rope)
    vt = norm_matmul_t(x, mix_norm, w[:, 2 * DIFF_QK:].T, BF16)
    lam = diff_lambda(lq1, lk1, lq2, lk2, lam_init)
    o = diff_attention(qt, k, vt, lam, subln, lam_init)
    return matmul_residual(o, w_o.astype(BF16), x)


FFT_R = 128
FFT_N = FFT_R * FFT_R
FFT_G = 8
HY_TC = 1024
HY_NQ = HY_TC // LANE


def _dft_constants():
    r = np.arange(FFT_R)
    n2, k1, n1 = r[:, None, None], r[None, :, None], r[None, None, :]
    th = 2.0 * np.pi * ((k1 * (FFT_R * n1 + n2)) % FFT_N) / FFT_N
    w1 = np.concatenate([np.cos(th), -np.sin(th)], axis=1)
    tht = np.transpose(th, (0, 2, 1))
    v1 = np.concatenate([np.cos(tht), -np.sin(tht)], axis=2) / FFT_N
    ph = 2.0 * np.pi * ((r[:, None] * r[None, :]) % FFT_R) / FFT_R
    c, s = np.cos(ph), np.sin(ph)
    g_fwd = np.block([[c, s], [-s, c]])
    g_inv = np.block([[c, -s], [s, c]])
    as_bf16 = lambda a: a.astype(np.float32).astype(BF16)
    half = FFT_R // 2
    return dict(w1_full=as_bf16(w1), w1_half=as_bf16(w1[:, :, :half]), v1_half=as_bf16(v1[:, :half, :]),
                g_fwd=as_bf16(g_fwd), g_inv=as_bf16(g_inv))


def _pack_pair(x):
    h = x.shape[0] // 2
    hi = lax.bitcast_convert_type(x[:h].astype(BF16).astype(F32), jnp.uint32)
    lo = lax.bitcast_convert_type(x[h:].astype(BF16).astype(F32), jnp.uint32)
    return hi | (lo >> 16)


def _unpack_pair(p):
    hi = lax.bitcast_convert_type(p & jnp.uint32(0xFFFF0000), F32)
    lo = lax.bitcast_convert_type(p << 16, F32)
    return jnp.concatenate([hi, lo], axis=0).astype(BF16)


HALO = 16


def _in_proj_conv_kernel(x_ref, prev_ref, next_ref, g_ref, w_ref, cw_ref, cb_ref, o_ref, xn_ref):
    i = pl.program_id(0)
    tm = x_ref.shape[0]

    @pl.when(pl.program_id(1) == 0)
    def _():
        g = g_ref[...]
        before = jnp.where(i > 0, _rms(prev_ref[...], g, RMS_EPS), 0.0)
        after = jnp.where(i < pl.num_programs(0) - 1, _rms(next_ref[...], g, RMS_EPS), 0.0)
        xn_ref[:HALO] = before.astype(BF16)
        xn_ref[HALO:HALO + tm] = _rms(x_ref[...], g, RMS_EPS).astype(BF16)
        xn_ref[HALO + tm:] = after.astype(BF16)

    u = jnp.dot(xn_ref[...], w_ref[...], preferred_element_type=F32)
    y = (u[HALO - 1:HALO - 1 + tm] * cw_ref[0:1, :] + u[HALO:HALO + tm] * cw_ref[1:2, :]
         + u[HALO + 1:HALO + 1 + tm] * cw_ref[2:3, :] + cb_ref[...])
    for q in range(o_ref.shape[0]):
        o_ref[q] = y[:, q * LANE:(q + 1) * LANE]


def in_proj_conv(x, g, w, conv_w, conv_b, tm=PROJ_TM, tn=HY_IN_TN):
    s, k = x.shape
    n = w.shape[1]
    nb = tm // HALO
    return pl.pallas_call(
        _in_proj_conv_kernel,
        grid=(_steps(s, tm), _steps(n, tn)),
        in_specs=[
            pl.BlockSpec((tm, k), lambda i, j: (i, 0)),
            pl.BlockSpec((HALO, k), lambda i, j: (jnp.maximum(i * nb - 1, 0), 0)),
            pl.BlockSpec((HALO, k), lambda i, j: (jnp.minimum((i + 1) * nb, s // HALO - 1), 0)),
            pl.BlockSpec((1, k), lambda i, j: (0, 0)),
            pl.BlockSpec((k, tn), lambda i, j: (0, j)),
            pl.BlockSpec((3, tn), lambda i, j: (0, j)),
            pl.BlockSpec((1, tn), lambda i, j: (0, j)),
        ],
        out_specs=pl.BlockSpec((tn // LANE, tm, LANE), lambda i, j: (j, i, 0)),
        out_shape=jax.ShapeDtypeStruct((n // LANE, s, LANE), F32),
        scratch_shapes=[pltpu.VMEM((tm + 2 * HALO, k), BF16)],
        compiler_params=_params("parallel", "arbitrary"),
        name="hy_in_proj_conv",
    )(x, x, x, g.reshape(1, k), w, conv_w, conv_b.reshape(1, n))


def _filter_mlp_kernel(z_ref, w1_ref, b1_ref, f_ref, w2_ref, b2_ref, w3_ref, b3_ref, f3_ref, o_ref):
    hp = lax.Precision.HIGHEST
    f = f_ref[...]
    h = jnp.sin(f * (jnp.dot(z_ref[...], w1_ref[...], precision=hp, preferred_element_type=F32) + b1_ref[...]))
    h = jnp.sin(f * (jnp.dot(h, w2_ref[...], precision=hp, preferred_element_type=F32) + b2_ref[...]))
    o_ref[...] = jnp.sin(f3_ref[...] * (jnp.dot(h, w3_ref[...], precision=hp, preferred_element_type=F32) + b3_ref[...]))


def filter_mlp(zfeat, f_w1, f_b1, f_freq, f_w2, f_b2, f_w3, f_b3, tm=2048):
    n, e = zfeat.shape
    wd = f_w2.shape[0]
    row = lambda i: (i, 0)
    fixed = lambda i: (0, 0)
    vec = lambda v: v.reshape(1, wd)
    lane_pad = lambda a: jnp.pad(a, ((0, 0), (0, LANE - wd)))
    wide = pl.BlockSpec((1, LANE), fixed)
    return pl.pallas_call(
        _filter_mlp_kernel,
        grid=(_steps(n, tm),),
        in_specs=[pl.BlockSpec((tm, e), row), pl.BlockSpec((e, wd), fixed), pl.BlockSpec((1, wd), fixed),
                  pl.BlockSpec((1, wd), fixed), pl.BlockSpec((wd, wd), fixed), pl.BlockSpec((1, wd), fixed),
                  pl.BlockSpec((wd, LANE), fixed), wide, wide],
        out_specs=pl.BlockSpec((tm, LANE), row),
        out_shape=jax.ShapeDtypeStruct((n, LANE), F32),
        compiler_params=_params("parallel"),
        name="hy_filter_mlp",
    )(zfeat, f_w1, vec(f_b1), vec(f_freq), f_w2, vec(f_b2), lane_pad(f_w3), lane_pad(vec(f_b3)), lane_pad(vec(f_freq)))


def _rows_by_group(ref):
    t = [jnp.swapaxes(ref[i], 0, 1) for i in range(ref.shape[0])]
    return [jnp.concatenate([ti[r] for ti in t], axis=1) for r in range(FFT_G)]


def _store_by_group(ref, vals):
    for i in range(ref.shape[0]):
        ref[i] = jnp.swapaxes(jnp.stack([v[:, i * LANE:(i + 1) * LANE] for v in vals], axis=0), 0, 1)


def _kern_stage1_kernel(h_ref, w00_ref, w01_ref, w10_ref, w11_ref, dl_ref, w1_ref, kb_ref, nrm_ref, *, seq):
    h_rows = _rows_by_group(h_ref)
    g = pl.program_id(1)
    half = FFT_R // 2
    w_fwd = (w00_ref[...], w01_ref[...])
    w_bwd = (w10_ref[...], w11_ref[...])
    rate = dl_ref[...] * (-1.0 / (seq - 1))
    n1 = lax.broadcasted_iota(jnp.int32, (half, 1), 0)
    packed = [[] for _ in range(HY_ORDER)]
    acc = [jnp.zeros(nrm_ref.shape[1:], F32) for _ in range(HY_ORDER)]
    for r in range(FFT_G):
        n2 = g * FFT_G + r
        hs = h_rows[r].astype(BF16)
        lag_f = FFT_R * n1 + n2
        lag_b = seq - FFT_R * n1 - n2
        decay_f = jnp.exp(lag_f.astype(F32) * rate)
        decay_b = jnp.where(lag_b == seq, 0.0, jnp.exp(lag_b.astype(F32) * rate))
        for o in range(HY_ORDER):
            kf = jnp.dot(hs[:half], w_fwd[o], preferred_element_type=F32) * decay_f
            kb = jnp.dot(hs[half:], w_bwd[o], preferred_element_type=F32) * decay_b
            slab = jnp.concatenate([kf, kb], axis=0)
            acc[o] += jnp.sum(jnp.abs(slab), axis=0, keepdims=True)
            packed[o].append(_pack_pair(jnp.dot(w1_ref[r], slab.astype(BF16), preferred_element_type=F32)))
    for o in range(HY_ORDER):
        _store_by_group(kb_ref.at[o], packed[o])

    @pl.when(g == 0)
    def _():
        for o in range(HY_ORDER):
            nrm_ref[o] = acc[o]

    @pl.when(g > 0)
    def _():
        for o in range(HY_ORDER):
            nrm_ref[o] += acc[o]


def kern_stage1(h3, w4, deltas, w1_full, seq, tc=HY_TC):
    d = deltas.shape[0]
    jn = _steps(d, tc)
    h3v = h3.reshape(1, FFT_R, FFT_R, LANE)
    assert HY_ORDER == 2
    kern = functools.partial(_kern_stage1_kernel, seq=seq)
    w4_spec = lambda direction, order: pl.BlockSpec((LANE, tc), lambda j, g: (0, (direction * HY_ORDER + order) * jn + j))
    return pl.pallas_call(
        kern,
        grid=(jn, FFT_R // FFT_G),
        in_specs=[
            pl.BlockSpec((1, FFT_R, FFT_G, LANE), lambda j, g: (0, 0, g, 0)),
            w4_spec(0, 0), w4_spec(0, 1), w4_spec(1, 0), w4_spec(1, 1),
            pl.BlockSpec((1, tc), lambda j, g: (0, j)),
            pl.BlockSpec((FFT_G, 2 * FFT_R, FFT_R), lambda j, g: (g, 0, 0)),
        ],
        out_specs=[
            pl.BlockSpec((HY_ORDER, HY_NQ, FFT_R, FFT_G, LANE), lambda j, g: (0, j, 0, g, 0)),
            pl.BlockSpec((HY_ORDER, 1, tc), lambda j, g: (0, 0, j)),
        ],
        out_shape=[
            jax.ShapeDtypeStruct((HY_ORDER, d // LANE, FFT_R, FFT_R, LANE), jnp.uint32),
            jax.ShapeDtypeStruct((HY_ORDER, 1, d), F32),
        ],
        compiler_params=_params("parallel", "arbitrary"),
        name="hy_kern_stage1",
    )(h3v, w4, w4, w4, w4, deltas.reshape(1, d), w1_full)


def _join_lanes(ref, idx):
    return jnp.concatenate([ref[(q,) + idx] for q in range(ref.shape[0])], axis=1)


def _kern_stage3_kernel(kb_ref, g_ref, o_ref):
    gm = g_ref[...]
    for r in range(FFT_G):
        b = _unpack_pair(_join_lanes(kb_ref.at[0], (r,)))
        o_ref[0, r] = jnp.dot(gm, b, preferred_element_type=F32).astype(o_ref.dtype)


def kern_stage3(kb, g_fwd, tc=HY_TC):
    d = kb.shape[1] * LANE
    return pl.pallas_call(
        _kern_stage3_kernel,
        grid=(HY_ORDER, FFT_R // FFT_G, _steps(d, tc)),
        in_specs=[
            pl.BlockSpec((1, HY_NQ, FFT_G, FFT_R, LANE), lambda o, kg, j: (o, j, kg, 0, 0)),
            pl.BlockSpec((2 * FFT_R, 2 * FFT_R), lambda o, kg, j: (0, 0)),
        ],
        out_specs=pl.BlockSpec((1, FFT_G, 2 * FFT_R, tc), lambda o, kg, j: (o, kg, 0, j)),
        out_shape=jax.ShapeDtypeStruct((HY_ORDER, FFT_R, 2 * FFT_R, d), BF16),
        compiler_params=_params("parallel", "parallel", "parallel"),
        name="hy_kern_stage3",
    )(kb, g_fwd)


def _sig_stage1_kernel(z_ref, w1_ref, o_ref):
    z_rows = _rows_by_group(z_ref)
    _store_by_group(o_ref, [_pack_pair(jnp.dot(w1_ref[r], z_rows[r].astype(BF16), preferred_element_type=F32))
                            for r in range(FFT_G)])


def sig_stage1(z_src, slab0, d, w1_half, tc=HY_TC):
    s = z_src.shape[1]
    rows = s // FFT_R
    zv = z_src.reshape(z_src.shape[0], rows, FFT_R, LANE)
    return pl.pallas_call(
        _sig_stage1_kernel,
        grid=(FFT_R // FFT_G, _steps(d, tc)),
        in_specs=[
            pl.BlockSpec((HY_NQ, rows, FFT_G, LANE), lambda g, j: (slab0 // HY_NQ + j, 0, g, 0)),
            pl.BlockSpec((FFT_G, 2 * FFT_R, rows), lambda g, j: (g, 0, 0)),
        ],
        out_specs=pl.BlockSpec((HY_NQ, FFT_R, FFT_G, LANE), lambda g, j: (j, 0, g, 0)),
        out_shape=jax.ShapeDtypeStruct((d // LANE, FFT_R, FFT_R, LANE), jnp.uint32),
        compiler_params=_params("parallel", "parallel"),
        name="hy_sig_stage1",
    )(zv, w1_half)


def _sig_stage3_kernel(zb_ref, ks_ref, gf_ref, gi_ref, o_ref):
    gf = gf_ref[...]
    gi = gi_ref[...]
    for r in range(FFT_G):
        x = jnp.dot(gf, _unpack_pair(_join_lanes(zb_ref, (r,))), preferred_element_type=F32)
        ks = ks_ref[0, r].astype(F32)
        xr, xi = x[:FFT_R], x[FFT_R:]
        kr, ki = ks[:FFT_R], ks[FFT_R:]
        y = jnp.concatenate([xr * kr - xi * ki, xr * ki + xi * kr], axis=0).astype(BF16)
        packed = _pack_pair(jnp.dot(gi, y, preferred_element_type=F32))
        for q in range(o_ref.shape[0]):
            o_ref[q, r] = packed[:, q * LANE:(q + 1) * LANE]


def sig_stage3(zb, kspec, order, g_fwd, g_inv, tc=HY_TC):
    d = zb.shape[0] * LANE
    blk = pl.BlockSpec((HY_NQ, FFT_G, FFT_R, LANE), lambda kg, j: (j, kg, 0, 0))
    mat = pl.BlockSpec((2 * FFT_R, 2 * FFT_R), lambda kg, j: (0, 0))
    return pl.pallas_call(
        _sig_stage3_kernel,
        grid=(FFT_R // FFT_G, _steps(d, tc)),
        in_specs=[blk, pl.BlockSpec((1, FFT_G, 2 * FFT_R, tc), lambda kg, j: (order, kg, 0, j)), mat, mat],
        out_specs=blk,
        out_shape=jax.ShapeDtypeStruct(zb.shape, jnp.uint32),
        compiler_params=_params("parallel", "parallel"),
        name="hy_sig_stage3",
    )(zb, kspec, g_fwd, g_inv)


def _sig_inverse1_kernel(yb_ref, v_ref, gate_ref, z_ref, nrm_ref, bias_ref, *rest, next_stage1):
    inv = 1.0 / nrm_ref[0]
    bias = bias_ref[0]
    yb_rows, gate_rows, z_rows = _rows_by_group(yb_ref), _rows_by_group(gate_ref), _rows_by_group(z_ref)
    out = []
    for r in range(FFT_G):
        conv = jnp.dot(v_ref[r], _unpack_pair(yb_rows[r]), preferred_element_type=F32) * inv
        out.append(gate_rows[r] * (conv + z_rows[r] * bias))
    if next_stage1:
        w1_ref, o_ref, zb_ref = rest
        _store_by_group(zb_ref, [_pack_pair(jnp.dot(w1_ref[r], out[r].astype(BF16), preferred_element_type=F32))
                                 for r in range(FFT_G)])
    else:
        o_ref, = rest
    _store_by_group(o_ref, out)


def sig_inverse1(yb, v1_half, gate_src, gate_slab0, z_src, z_slab0, nrm, bias, order, w1_half=None, tc=HY_TC):
    d = yb.shape[0] * LANE
    s = gate_src.shape[1]
    rows = s // FFT_R
    view = lambda a: a.reshape(a.shape[0], rows, FFT_R, LANE)
    sig = (HY_NQ, rows, FFT_G, LANE)
    freq = (HY_NQ, FFT_R, FFT_G, LANE)
    in_specs = [
        pl.BlockSpec(freq, lambda g, j: (j, 0, g, 0)),
        pl.BlockSpec((FFT_G, rows, 2 * FFT_R), lambda g, j: (g, 0, 0)),
        pl.BlockSpec(sig, lambda g, j: (gate_slab0 // HY_NQ + j, 0, g, 0)),
        pl.BlockSpec(sig, lambda g, j: (z_slab0 // HY_NQ + j, 0, g, 0)),
        pl.BlockSpec((1, 1, tc), lambda g, j: (order, 0, j)),
        pl.BlockSpec((1, 1, tc), lambda g, j: (order, 0, j)),
    ]
    args = [yb, v1_half, view(gate_src), view(z_src), nrm, bias.reshape(HY_ORDER, 1, d)]
    out_specs = [pl.BlockSpec(sig, lambda g, j: (j, 0, g, 0))]
    out_shape = [jax.ShapeDtypeStruct((d // LANE, rows, FFT_R, LANE), F32)]
    if w1_half is not None:
        in_specs.append(pl.BlockSpec((FFT_G, 2 * FFT_R, rows), lambda g, j: (g, 0, 0)))
        args.append(w1_half)
        out_specs.append(pl.BlockSpec(freq, lambda g, j: (j, 0, g, 0)))
        out_shape.append(jax.ShapeDtypeStruct((d // LANE, FFT_R, FFT_R, LANE), jnp.uint32))
    outs = pl.pallas_call(
        functools.partial(_sig_inverse1_kernel, next_stage1=w1_half is not None),
        grid=(FFT_R // FFT_G, _steps(d, tc)),
        in_specs=in_specs,
        out_specs=out_specs,
        out_shape=out_shape,
        compiler_params=_params("parallel", "parallel"),
        name="hy_sig_inverse1",
    )(*args)
    z_new = outs[0].reshape(d // LANE, s, LANE)
    return (z_new, outs[1]) if w1_half is not None else (z_new, None)


def _hyena_positions(seq):
    t = jnp.linspace(0.0, 1.0, seq, dtype=F32)[:, None]
    bands = (HY_EMB_DIM - 1) // 2
    w = 2.0 * math.pi * jnp.arange(seq, dtype=F32)[:, None] / seq
    f = jnp.linspace(1e-4, bands - 1, bands, dtype=F32)[None, :]
    z = jnp.concatenate([t, jnp.cos(f * w), -jnp.sin(f * w)], axis=-1)
    return jnp.pad(z, ((0, 0), (0, LANE - HY_EMB_DIM)))


def hyena_mixer(u, f_w1, f_b1, f_freq, f_w2, f_b2, f_w3, f_b3, f_w4, bias):
    s = u.shape[1]
    d = u.shape[0] * LANE // 3
    assert 2 * s == FFT_N and d % HY_TC == 0
    c = _dft_constants()
    h3 = filter_mlp(_hyena_positions(s), jnp.pad(f_w1, ((0, LANE - HY_EMB_DIM), (0, 0))), f_b1, f_freq, f_w2, f_b2, f_w3, f_b3)
    h3 = jnp.concatenate([h3, h3[:1], h3[:0:-1]], axis=0)
    w4 = jnp.pad(f_w4.astype(BF16), ((0, LANE - f_w4.shape[0]), (0, 0)))
    deltas = jnp.abs(jnp.linspace(HY_MIN_DECAY, HY_MAX_DECAY, d, dtype=F32))
    kb, nrm = kern_stage1(h3, w4, deltas, c["w1_full"], s)
    kspec = kern_stage3(kb, c["g_fwd"])
    nslab = d // LANE
    z_src, z_slab0 = u, 2 * nslab
    zb = sig_stage1(z_src, z_slab0, d, c["w1_half"])
    for o in range(HY_ORDER):
        yb = sig_stage3(zb, kspec, o, c["g_fwd"], c["g_inv"])
        w1_next = c["w1_half"] if o + 1 < HY_ORDER else None
        z_src, zb = sig_inverse1(yb, c["v1_half"], u, o * nslab, z_src, z_slab0, nrm, bias, o, w1_next)
        z_slab0 = 0
    return z_src


def hyena_layer(x, mix_norm, w_in, conv_w, conv_b, f_w1, f_b1, f_freq, f_w2, f_b2, f_w3, f_b3, f_w4, bias, w_out):
    u = in_proj_conv(x, mix_norm, w_in.astype(BF16), conv_w, conv_b)
    z = hyena_mixer(u, f_w1, f_b1, f_freq, f_w2, f_b2, f_w3, f_b3, f_w4, bias)
    return matmul_residual(z, w_out.astype(BF16), x, a_in_slabs=True)


def _rope_tables(seq, dim):
    inv = 1.0 / (ROPE_THETA ** (jnp.arange(0, dim, 2, dtype=F32) / dim))
    ang = jnp.arange(seq, dtype=F32)[:, None] * inv[None, :]
    return jnp.cos(ang), jnp.sin(ang)


def kernel(x, l0_mix_norm, l0_mla_w_in, l0_mla_q_norm, l0_mla_w_uq, l0_mla_kv_norm, l0_mla_w_ukv, l0_mla_w_o, l0_ffn_norm, l0_ffn_w_gate, l0_ffn_w_up, l0_ffn_w_down, l1_mix_norm, l1_hy_w_in, l1_hy_conv_w, l1_hy_conv_b, l1_hy_f_w1, l1_hy_f_b1, l1_hy_f_freq, l1_hy_f_w2, l1_hy_f_b2, l1_hy_f_w3, l1_hy_f_b3, l1_hy_f_w4, l1_hy_bias, l1_hy_w_out, l1_ffn_norm, l1_ffn_w_gate, l1_ffn_w_up, l1_ffn_w_down, l2_mix_norm, l2_da_w_qkv, l2_da_lq1, l2_da_lk1, l2_da_lq2, l2_da_lk2, l2_da_subln, l2_da_w_o, l2_ffn_norm, l2_ffn_w_gate, l2_ffn_w_up, l2_ffn_w_down, l3_mix_norm, l3_mla_w_in, l3_mla_q_norm, l3_mla_w_uq, l3_mla_kv_norm, l3_mla_w_ukv, l3_mla_w_o, l3_ffn_norm, l3_ffn_w_gate, l3_ffn_w_up, l3_ffn_w_down, final_norm):
    b, s, d = x.shape
    assert b == 1
    h = x.reshape(s, d)

    mc, ms = _rope_tables(s, MLA_ROPE)
    mla_tab = jnp.concatenate([mc, mc, ms, ms], axis=-1)
    mla = (mla_tab, mc.T, ms.T)
    dc, ds = _rope_tables(s, DIFF_HEAD_DIM)
    diff_rope = (jnp.concatenate([dc, dc], axis=-1), jnp.concatenate([-ds, ds], axis=-1))
    diff_rope_t = (dc.T, ds.T)

    def ffn(h, g, wg, wu, wd, out_norm_gain=None):
        return ffn_block(h, g, wg.astype(BF16), wu.astype(BF16), wd.astype(BF16), out_norm_gain)

    h = mla_layer(h, l0_mix_norm, l0_mla_w_in, l0_mla_q_norm, l0_mla_w_uq, l0_mla_kv_norm, l0_mla_w_ukv, l0_mla_w_o, *mla)
    h = ffn(h, l0_ffn_norm, l0_ffn_w_gate, l0_ffn_w_up, l0_ffn_w_down)
    h = hyena_layer(h, l1_mix_norm, l1_hy_w_in, l1_hy_conv_w, l1_hy_conv_b, l1_hy_f_w1, l1_hy_f_b1, l1_hy_f_freq, l1_hy_f_w2, l1_hy_f_b2, l1_hy_f_w3, l1_hy_f_b3, l1_hy_f_w4, l1_hy_bias, l1_hy_w_out)
    h = ffn(h, l1_ffn_norm, l1_ffn_w_gate, l1_ffn_w_up, l1_ffn_w_down)
    h = diff_layer(h, l2_mix_norm, l2_da_w_qkv, l2_da_lq1, l2_da_lk1, l2_da_lq2, l2_da_lk2, l2_da_subln, l2_da_w_o, diff_rope, diff_rope_t, 2)
    h = ffn(h, l2_ffn_norm, l2_ffn_w_gate, l2_ffn_w_up, l2_ffn_w_down)
    h = mla_layer(h, l3_mix_norm, l3_mla_w_in, l3_mla_q_norm, l3_mla_w_uq, l3_mla_kv_norm, l3_mla_w_ukv, l3_mla_w_o, *mla)
    h = ffn(h, l3_ffn_norm, l3_ffn_w_gate, l3_ffn_w_up, l3_ffn_w_down, final_norm)
    return h.reshape(b, s, d)
```

```python
import functools
import math

import jax
import jax.numpy as jnp
import numpy as np
from jax import lax
from jax.experimental import pallas as pl
from jax.experimental.pallas import tpu as pltpu

F32 = jnp.float32
BF16 = jnp.bfloat16

D_MODEL = 2048
RMS_EPS = 1e-6
ROPE_THETA = 10000.0
LOG2E = math.log2(math.e)

MLA_HEADS = 16
MLA_Q_LORA = 768
MLA_KV_LORA = 512
MLA_NOPE = 128
MLA_ROPE = 64
MLA_V = 128
MLA_QK = MLA_NOPE + MLA_ROPE
MLA_SCALE = MLA_QK ** -0.5
MLA_QK_PAD = 256

HY_ORDER = 2
HY_EMB_DIM = 33
HY_TARGET = 1e-2
HY_MAX_DECAY = math.log(HY_TARGET) / 0.3
HY_MIN_DECAY = math.log(HY_TARGET) / 1.5

DIFF_HEAD_DIM = 128
DIFF_HEADS = D_MODEL // (2 * DIFF_HEAD_DIM)
DIFF_QK = DIFF_HEADS * 2 * DIFF_HEAD_DIM
DIFF_SCALE = DIFF_HEAD_DIM ** -0.5
DIFF_EPS = 1e-5

LANE = 128
VMEM_BYTES = 64 * 1024 * 1024
VMEM_LIMIT = VMEM_BYTES * 7 // 8
NEG_BIG = -1e30

PROJ_TM, PROJ_TN = 1024, 1024
MLA_IN_TM = 512
FFN_TM, FFN_TF = 1024, 512
MLA_TQ, DIFF_TQ, ATTN_TK = 4096, 2048, 512
HY_IN_TN = 1024


def _params(*sem):
    return pltpu.CompilerParams(dimension_semantics=sem, vmem_limit_bytes=VMEM_LIMIT)


def _steps(dim, tile):
    assert dim % tile == 0, f"tile {tile} does not divide {dim}"
    return dim // tile


def _rms(x, g, eps):
    return x * lax.rsqrt(jnp.mean(x * x, axis=-1, keepdims=True) + eps) * g


def _dot_nt(a, b):
    return lax.dot_general(a, b, (((1,), (1,)), ((), ())), preferred_element_type=F32)


def _rope_rows(y, c, s):
    h = c.shape[0]
    y1, y2 = y[:h], y[h:]
    return y1 * c - y2 * s, y2 * c + y1 * s


def _norm_matmul_kernel(x_ref, g_ref, w_ref, *rest, rope, scale):
    o_ref, xn_ref = rest[-2:]

    @pl.when(pl.program_id(1) == 0)
    def _():
        xn_ref[...] = _rms(x_ref[...], g_ref[...], RMS_EPS).astype(BF16)

    y = jnp.dot(xn_ref[...], w_ref[...], preferred_element_type=F32)
    if rope:
        c = rest[0][...] * scale
        s = rest[1][...] * scale
        hd = c.shape[1]
        for g in range(y.shape[1] // hd):
            t = y[:, g * hd:(g + 1) * hd]
            o_ref[:, g * hd:(g + 1) * hd] = (t * c + pltpu.roll(t, hd // 2, axis=1) * s).astype(o_ref.dtype)
    else:
        o_ref[...] = (y * scale).astype(o_ref.dtype)


def norm_matmul(x, g, w, out_dtype, rope=None, scale=1.0, tm=PROJ_TM, tn=PROJ_TN):
    m, k = x.shape
    n = w.shape[1]
    in_specs = [
        pl.BlockSpec((tm, k), lambda i, j: (i, 0)),
        pl.BlockSpec((1, k), lambda i, j: (0, 0)),
        pl.BlockSpec((k, tn), lambda i, j: (0, j)),
    ]
    args = [x, g.reshape(1, k), w]
    if rope is not None:
        in_specs += [pl.BlockSpec((tm, rope[0].shape[1]), lambda i, j: (i, 0))] * 2
        args += list(rope)
    return pl.pallas_call(
        functools.partial(_norm_matmul_kernel, rope=rope is not None, scale=scale),
        grid=(_steps(m, tm), _steps(n, tn)),
        in_specs=in_specs,
        out_specs=pl.BlockSpec((tm, tn), lambda i, j: (i, j)),
        out_shape=jax.ShapeDtypeStruct((m, n), out_dtype),
        scratch_shapes=[pltpu.VMEM((tm, k), BF16)],
        compiler_params=_params("parallel", "arbitrary"),
        name="norm_matmul",
    )(*args)


def _norm_matmul_t_kernel(x_ref, g_ref, wt_ref, *rest, rope, scale):
    o_ref, xn_ref = rest[-2:]

    @pl.when(pl.program_id(1) == 0)
    def _():
        xn_ref[...] = _rms(x_ref[...], g_ref[...], RMS_EPS).astype(BF16)

    y = _dot_nt(wt_ref[...], xn_ref[...]) * scale
    if rope:
        c = rest[0][...]
        s = rest[1][...]
        half = c.shape[0]
        for g in range(y.shape[0] // (2 * half)):
            r1, r2 = _rope_rows(y[2 * g * half:2 * (g + 1) * half], c, s)
            o_ref[2 * g * half:(2 * g + 1) * half, :] = r1.astype(o_ref.dtype)
            o_ref[(2 * g + 1) * half:2 * (g + 1) * half, :] = r2.astype(o_ref.dtype)
    else:
        o_ref[...] = y.astype(o_ref.dtype)


def norm_matmul_t(x, g, wt, out_dtype, rope_t=None, scale=1.0, tm=PROJ_TM, tn=PROJ_TN):
    m, k = x.shape
    n = wt.shape[0]
    in_specs = [
        pl.BlockSpec((tm, k), lambda i, j: (i, 0)),
        pl.BlockSpec((1, k), lambda i, j: (0, 0)),
        pl.BlockSpec((tn, k), lambda i, j: (j, 0)),
    ]
    args = [x, g.reshape(1, k), wt]
    if rope_t is not None:
        in_specs += [pl.BlockSpec((rope_t[0].shape[0], tm), lambda i, j: (0, i))] * 2
        args += list(rope_t)
    return pl.pallas_call(
        functools.partial(_norm_matmul_t_kernel, rope=rope_t is not None, scale=scale),
        grid=(_steps(m, tm), _steps(n, tn)),
        in_specs=in_specs,
        out_specs=pl.BlockSpec((tn, tm), lambda i, j: (j, i)),
        out_shape=jax.ShapeDtypeStruct((n, m), out_dtype),
        scratch_shapes=[pltpu.VMEM((tm, k), BF16)],
        compiler_params=_params("parallel", "arbitrary"),
        name="norm_matmul_t",
    )(*args)


def _matmul_res_kernel(a_ref, w_ref, r_ref, o_ref):
    if len(a_ref.shape) == 3:
        a = jnp.concatenate([a_ref[q] for q in range(a_ref.shape[0])], axis=1)
    else:
        a = a_ref[...]
    o_ref[...] = r_ref[...] + jnp.dot(a.astype(BF16), w_ref[...], preferred_element_type=F32)


def matmul_residual(a, w, res, a_in_slabs=False, tm=PROJ_TM, tn=PROJ_TN):
    k, n = w.shape
    m = res.shape[0]
    if a_in_slabs:
        a_spec = pl.BlockSpec((k // LANE, tm, LANE), lambda i, j: (0, i, 0))
    else:
        a_spec = pl.BlockSpec((tm, k), lambda i, j: (i, 0))
    return pl.pallas_call(
        _matmul_res_kernel,
        grid=(_steps(m, tm), _steps(n, tn)),
        in_specs=[
            a_spec,
            pl.BlockSpec((k, tn), lambda i, j: (0, j)),
            pl.BlockSpec((tm, tn), lambda i, j: (i, j)),
        ],
        out_specs=pl.BlockSpec((tm, tn), lambda i, j: (i, j)),
        out_shape=jax.ShapeDtypeStruct((m, n), F32),
        compiler_params=_params("parallel", "parallel"),
        name="matmul_residual",
    )(a, w, res)


def _ffn_kernel(x_ref, g_ref, wg_ref, wu_ref, wd_ref, *rest, out_norm):
    o_ref, xn_ref = rest[-2:]

    @pl.when(pl.program_id(1) == 0)
    def _():
        x = x_ref[...]
        xn_ref[...] = _rms(x, g_ref[...], RMS_EPS).astype(BF16)
        o_ref[...] = x

    xn = xn_ref[...]
    a = jnp.dot(xn, wg_ref[...], preferred_element_type=F32)
    b = jnp.dot(xn, wu_ref[...], preferred_element_type=F32)
    h = (a * jax.nn.sigmoid(a) * b).astype(BF16)
    o_ref[...] += jnp.dot(h, wd_ref[...], preferred_element_type=F32)

    if out_norm:
        @pl.when(pl.program_id(1) == pl.num_programs(1) - 1)
        def _():
            o_ref[...] = _rms(o_ref[...], rest[0][...], RMS_EPS)


def ffn_block(x, g, wg, wu, wd, out_norm_gain=None, tm=FFN_TM, tf=FFN_TF):
    m, d = x.shape
    f = wg.shape[1]
    vec = pl.BlockSpec((1, d), lambda i, j: (0, 0))
    in_specs = [
        pl.BlockSpec((tm, d), lambda i, j: (i, 0)),
        vec,
        pl.BlockSpec((d, tf), lambda i, j: (0, j)),
        pl.BlockSpec((d, tf), lambda i, j: (0, j)),
        pl.BlockSpec((tf, d), lambda i, j: (j, 0)),
    ]
    args = [x, g.reshape(1, d), wg, wu, wd]
    if out_norm_gain is not None:
        in_specs.append(vec)
        args.append(out_norm_gain.reshape(1, d))
    return pl.pallas_call(
        functools.partial(_ffn_kernel, out_norm=out_norm_gain is not None),
        grid=(_steps(m, tm), _steps(f, tf)),
        in_specs=in_specs,
        out_specs=pl.BlockSpec((tm, d), lambda i, j: (i, 0)),
        out_shape=jax.ShapeDtypeStruct((m, d), F32),
        scratch_shapes=[pltpu.VMEM((tm, d), BF16)],
        compiler_params=_params("parallel", "arbitrary"),
        name="ffn_block",
    )(*args)


def _flash_t(qs, k_of_part, k_ref, vt_ref, sa_ref, sb_ref, acc_ref, *, tk):
    parts = len(qs)
    tp = qs[0].shape[1]
    nk = k_ref.shape[0] // tk
    assert nk % 2 == 0 and nk >= 4

    def chunk(c):
        return pl.ds(c * tk if isinstance(c, int) else pl.multiple_of(c * tk, tk), tk)

    def scores(c, s_ref):
        k = k_ref[chunk(c), :]
        mc = ()
        for i in range(parts):
            st = jnp.dot(k_of_part(k, i), qs[i], preferred_element_type=F32)
            s_ref[i] = st
            mc += (jnp.max(st, axis=0, keepdims=True),)
        return mc

    def softmax_pv(c, s_ref, mc, state):
        vt = vt_ref[:, chunk(c)]
        new = ()
        for i in range(parts):
            m, l = state[2 * i:2 * i + 2]
            m_new = jnp.maximum(m, mc[i])
            alpha = jnp.exp2(m - m_new)
            p = jnp.exp2(s_ref[i] - m_new)
            l = alpha * l + jnp.sum(p, axis=0, keepdims=True)
            acc_ref[i] = alpha * acc_ref[i] + jnp.dot(vt, p.astype(BF16), preferred_element_type=F32)
            new += (m_new, l)
        return new

    def body(c2, carry):
        mc_a, state = carry[:parts], carry[parts:]
        c = 2 * c2
        mc_b = scores(c + 1, sb_ref)
        state = softmax_pv(c, sa_ref, mc_a, state)
        mc_a = scores(c + 2, sa_ref)
        state = softmax_pv(c + 1, sb_ref, mc_b, state)
        return mc_a + state

    acc_ref[...] = jnp.zeros(acc_ref.shape, F32)
    state0 = (jnp.full((1, tp), NEG_BIG, F32), jnp.zeros((1, tp), F32)) * parts
    carry = lax.fori_loop(0, nk // 2 - 1, body, scores(0, sa_ref) + state0)
    mc_a, state = carry[:parts], carry[parts:]
    mc_b = scores(nk - 1, sb_ref)
    state = softmax_pv(nk - 2, sa_ref, mc_a, state)
    state = softmax_pv(nk - 1, sb_ref, mc_b, state)
    return state[1::2]


MLA_KR_PAD = MLA_QK_PAD - MLA_NOPE


def _rope_pair(t, tab):
    t = t * tab
    return t + pltpu.roll(t, MLA_ROPE, axis=1)


def _mla_in_kernel(x_ref, g_ref, w_ref, qn_ref, kvn_ref, tab_ref, cq_ref, ckv_ref, kr_ref):
    xn = _rms(x_ref[...], g_ref[...], RMS_EPS).astype(BF16)
    h = jnp.dot(xn, w_ref[...], preferred_element_type=F32)
    cq_ref[...] = _rms(h[:, :MLA_Q_LORA], qn_ref[...], RMS_EPS).astype(BF16)
    ckv_ref[...] = _rms(h[:, MLA_Q_LORA:MLA_Q_LORA + MLA_KV_LORA], kvn_ref[...], RMS_EPS).astype(BF16)
    kr = _rope_pair(h[:, MLA_Q_LORA + MLA_KV_LORA:], tab_ref[...])
    lane = lax.broadcasted_iota(jnp.int32, kr.shape, 1)
    kr_ref[...] = jnp.where(lane < MLA_ROPE, kr, 0.0).astype(BF16)


def mla_in(x, g, w_in_ext, q_norm, kv_norm, tab, tm=MLA_IN_TM):
    m, d = x.shape
    n = w_in_ext.shape[1]
    row = lambda i: (i, 0)
    fixed = lambda i: (0, 0)
    return pl.pallas_call(
        _mla_in_kernel,
        grid=(_steps(m, tm),),
        in_specs=[
            pl.BlockSpec((tm, d), row),
            pl.BlockSpec((1, d), fixed),
            pl.BlockSpec((d, n), fixed),
            pl.BlockSpec((1, MLA_Q_LORA), fixed),
            pl.BlockSpec((1, MLA_KV_LORA), fixed),
            pl.BlockSpec((tm, MLA_KR_PAD), row),
        ],
        out_specs=[
            pl.BlockSpec((tm, MLA_Q_LORA), row),
            pl.BlockSpec((tm, MLA_KV_LORA), row),
            pl.BlockSpec((tm, MLA_KR_PAD), row),
        ],
        out_shape=[
            jax.ShapeDtypeStruct((m, MLA_Q_LORA), BF16),
            jax.ShapeDtypeStruct((m, MLA_KV_LORA), BF16),
            jax.ShapeDtypeStruct((m, MLA_KR_PAD), BF16),
        ],
        compiler_params=_params("parallel"),
        name="mla_in",
    )(x, g.reshape(1, d), w_in_ext, q_norm.reshape(1, -1), kv_norm.reshape(1, -1), tab)


def _mla_qt_kernel(c_ref, wt_ref, cos_ref, sin_ref, q_ref, *, heads):
    y = _dot_nt(wt_ref[...], c_ref[...]) * (MLA_SCALE * LOG2E)
    c = cos_ref[...]
    s = sin_ref[...]
    half = MLA_ROPE // 2
    zeros = jnp.zeros((MLA_QK_PAD - MLA_QK, y.shape[1]), BF16)
    for h in range(heads):
        src = h * MLA_QK
        dst = h * MLA_QK_PAD
        q_ref[dst:dst + MLA_NOPE, :] = y[src:src + MLA_NOPE].astype(BF16)
        r1, r2 = _rope_rows(y[src + MLA_NOPE:src + MLA_QK], c, s)
        q_ref[dst + MLA_NOPE:dst + MLA_NOPE + half, :] = r1.astype(BF16)
        q_ref[dst + MLA_NOPE + half:dst + MLA_QK, :] = r2.astype(BF16)
        q_ref[dst + MLA_QK:dst + MLA_QK_PAD, :] = zeros


def mla_qt(cq, w_uq_t, cos_t, sin_t, tm=PROJ_TM, heads_per_step=4):
    m, k = cq.shape
    tn_in = heads_per_step * MLA_QK
    tn_out = heads_per_step * MLA_QK_PAD
    return pl.pallas_call(
        functools.partial(_mla_qt_kernel, heads=heads_per_step),
        grid=(_steps(m, tm), _steps(MLA_HEADS, heads_per_step)),
        in_specs=[
            pl.BlockSpec((tm, k), lambda i, j: (i, 0)),
            pl.BlockSpec((tn_in, k), lambda i, j: (j, 0)),
            pl.BlockSpec((MLA_ROPE // 2, tm), lambda i, j: (0, i)),
            pl.BlockSpec((MLA_ROPE // 2, tm), lambda i, j: (0, i)),
        ],
        out_specs=pl.BlockSpec((tn_out, tm), lambda i, j: (j, i)),
        out_shape=jax.ShapeDtypeStruct((MLA_HEADS * MLA_QK_PAD, m), BF16),
        compiler_params=_params("parallel", "parallel"),
        name="mla_qt",
    )(cq, w_uq_t, cos_t, sin_t)


def _mla_kv_kernel(c_ref, wk_ref, wvt_ref, kr_ref, k_ref, vt_ref, *, heads):
    c = c_ref[...]
    yk = jnp.dot(c, wk_ref[...], preferred_element_type=F32)
    kr = kr_ref[...]
    for h in range(heads):
        dst = h * MLA_QK_PAD
        k_ref[:, dst:dst + MLA_NOPE] = yk[:, h * MLA_NOPE:(h + 1) * MLA_NOPE].astype(BF16)
        k_ref[:, dst + MLA_NOPE:dst + MLA_QK_PAD] = kr
    vt_ref[...] = _dot_nt(wvt_ref[...], c).astype(BF16)


def mla_kv(ckv, w_k, w_v_t, kr, tm=PROJ_TM, heads_per_step=4):
    m, k = ckv.shape
    hp = heads_per_step
    return pl.pallas_call(
        functools.partial(_mla_kv_kernel, heads=hp),
        grid=(_steps(m, tm), _steps(MLA_HEADS, hp)),
        in_specs=[
            pl.BlockSpec((tm, k), lambda i, j: (i, 0)),
            pl.BlockSpec((k, hp * MLA_NOPE), lambda i, j: (0, j)),
            pl.BlockSpec((hp * MLA_V, k), lambda i, j: (j, 0)),
            pl.BlockSpec((tm, MLA_KR_PAD), lambda i, j: (i, 0)),
        ],
        out_specs=[
            pl.BlockSpec((tm, hp * MLA_QK_PAD), lambda i, j: (i, j)),
            pl.BlockSpec((hp * MLA_V, tm), lambda i, j: (j, i)),
        ],
        out_shape=[
            jax.ShapeDtypeStruct((m, MLA_HEADS * MLA_QK_PAD), BF16),
            jax.ShapeDtypeStruct((MLA_HEADS * MLA_V, m), BF16),
        ],
        compiler_params=_params("parallel", "parallel"),
        name="mla_kv",
    )(ckv, w_k, w_v_t, kr)


def _mla_attn_kernel(qt_ref, k_ref, vt_ref, o_ref, sa_ref, sb_ref, acc_ref, *, tk):
    parts, _, tp = sa_ref.shape
    qs = [qt_ref[:, i * tp:(i + 1) * tp] for i in range(parts)]
    ls = _flash_t(qs, lambda k, i: k, k_ref, vt_ref, sa_ref, sb_ref, acc_ref, tk=tk)
    for i in range(parts):
        o_ref[i * tp:(i + 1) * tp, :] = (acc_ref[i] / ls[i]).T.astype(o_ref.dtype)


def mla_attention(qt, k, vt, tq=MLA_TQ, tk=ATTN_TK, parts=2):
    s = k.shape[0]
    return pl.pallas_call(
        functools.partial(_mla_attn_kernel, tk=tk),
        grid=(MLA_HEADS, _steps(s, tq)),
        in_specs=[
            pl.BlockSpec((MLA_QK_PAD, tq), lambda h, i: (h, i)),
            pl.BlockSpec((s, MLA_QK_PAD), lambda h, i: (0, h)),
            pl.BlockSpec((MLA_V, s), lambda h, i: (h, 0)),
        ],
        out_specs=pl.BlockSpec((tq, MLA_V), lambda h, i: (i, h)),
        out_shape=jax.ShapeDtypeStruct((s, MLA_HEADS * MLA_V), BF16),
        scratch_shapes=[
            pltpu.VMEM((parts, tk, tq // parts), F32),
            pltpu.VMEM((parts, tk, tq // parts), F32),
            pltpu.VMEM((parts, MLA_V, tq // parts), F32),
        ],
        compiler_params=_params("parallel", "parallel"),
        name="mla_attention",
    )(qt, k, vt)


def _rot_cols(w):
    half = w.shape[-1] // 2
    return jnp.concatenate([-w[..., half:], w[..., :half]], axis=-1)


def mla_layer(x, mix_norm, w_in, q_norm, w_uq, kv_norm, w_ukv, w_o, tab, cos_t, sin_t):
    w_kr = w_in[:, MLA_Q_LORA + MLA_KV_LORA:]
    w_in_ext = jnp.concatenate([w_in, _rot_cols(w_kr)], axis=1).astype(BF16)
    w_uq_t = w_uq.astype(BF16).T
    wkv = w_ukv.astype(BF16).reshape(MLA_KV_LORA, MLA_HEADS, MLA_NOPE + MLA_V)
    w_k = wkv[..., :MLA_NOPE].reshape(MLA_KV_LORA, -1)
    w_v_t = wkv[..., MLA_NOPE:].reshape(MLA_KV_LORA, -1).T
    cq, ckv, kr = mla_in(x, mix_norm, w_in_ext, q_norm, kv_norm, tab)
    qt = mla_qt(cq, w_uq_t, cos_t, sin_t)
    k, vt = mla_kv(ckv, w_k, w_v_t, kr)
    o = mla_attention(qt, k, vt)
    return matmul_residual(o, w_o.astype(BF16), x)


def _diff_attn_kernel(lam_ref, qt_ref, k_ref, vt_ref, g_ref, o_ref, sa_ref, sb_ref, acc_ref, *, tk, lam_init):
    d = DIFF_HEAD_DIM
    qs = [qt_ref[:d, :], qt_ref[d:, :]]
    l0, l1 = _flash_t(qs, lambda k, i: k[:, i * d:(i + 1) * d], k_ref, vt_ref, sa_ref, sb_ref, acc_ref, tk=tk)
    ot = acc_ref[0] / l0 - lam_ref[0, 0] * (acc_ref[1] / l1)
    o = _rms(ot.T, g_ref[...], DIFF_EPS) * (1.0 - lam_init)
    o_ref[...] = o.astype(o_ref.dtype)


def diff_attention(qt, k, vt, lam, subln, lam_init, tq=DIFF_TQ, tk=ATTN_TK):
    s = k.shape[0]
    hd = 2 * DIFF_HEAD_DIM
    return pl.pallas_call(
        functools.partial(_diff_attn_kernel, tk=tk, lam_init=lam_init),
        grid=(DIFF_HEADS, _steps(s, tq)),
        in_specs=[
            pl.BlockSpec(memory_space=pltpu.SMEM),
            pl.BlockSpec((hd, tq), lambda h, i: (h, i)),
            pl.BlockSpec((s, hd), lambda h, i: (0, h)),
            pl.BlockSpec((hd, s), lambda h, i: (h, 0)),
            pl.BlockSpec((1, hd), lambda h, i: (0, 0)),
        ],
        out_specs=pl.BlockSpec((tq, hd), lambda h, i: (i, h)),
        out_shape=jax.ShapeDtypeStruct((s, DIFF_HEADS * hd), BF16),
        scratch_shapes=[
            pltpu.VMEM((2, tk, tq), F32),
            pltpu.VMEM((2, tk, tq), F32),
            pltpu.VMEM((2, hd, tq), F32),
        ],
        compiler_params=_params("parallel", "parallel"),
        name="diff_attention",
    )(lam, qt, k, vt, subln.reshape(1, hd))


def _lambda_kernel(lq1_ref, lk1_ref, lq2_ref, lk2_ref, o_ref, *, lam_init):
    a = jnp.sum(lq1_ref[...] * lk1_ref[...], axis=-1, keepdims=True)
    b = jnp.sum(lq2_ref[...] * lk2_ref[...], axis=-1, keepdims=True)
    o_ref[...] = jnp.exp(a) - jnp.exp(b) + lam_init


def diff_lambda(lq1, lk1, lq2, lk2, lam_init):
    args = [v.reshape(1, -1) for v in (lq1, lk1, lq2, lk2)]
    return pl.pallas_call(
        functools.partial(_lambda_kernel, lam_init=lam_init),
        out_shape=jax.ShapeDtypeStruct((1, 1), F32),
        name="diff_lambda",
    )(*args)


def diff_layer(x, mix_norm, w_qkv, lq1, lk1, lq2, lk2, subln, w_o, rope, rope_t, layer_idx):
    lam_init = 0.8 - 0.6 * math.exp(-0.3 * layer_idx)
    w = w_qkv.astype(BF16)
    qt = norm_matmul_t(x, mix_norm, w[:, :DIFF_QK].T, BF16, rope_t=rope_t, scale=DIFF_SCALE * LOG2E)
    k = norm_matmul(x, mix_norm, w[:, DIFF_QK:2 * DIFF_QK], BF16, rope=rope)
    vt = norm_matmul_t(x, mix_norm, w[:, 2 * DIFF_QK:].T, BF16)
    lam = diff_lambda(lq1, lk1, lq2, lk2, lam_init)
    o = diff_attention(qt, k, vt, lam, subln, lam_init)
    return matmul_residual(o, w_o.astype(BF16), x)


FFT_R = 128
FFT_N = FFT_R * FFT_R
FFT_G = 8
FFT_K1 = 72
HY_TC = 1024
HY_NQ = HY_TC // LANE


def _dft_constants():
    r = np.arange(FFT_R)
    n2, k1, n1 = r[:, None, None], np.arange(FFT_K1)[None, :, None], r[None, None, :]
    th = 2.0 * np.pi * ((k1 * (FFT_R * n1 + n2)) % FFT_N) / FFT_N
    w1 = np.concatenate([np.cos(th), -np.sin(th)], axis=1)
    tht = np.transpose(th, (0, 2, 1))
    weight = np.where(np.arange(FFT_K1) > FFT_R // 2, 0.0, np.where(np.arange(FFT_K1) % (FFT_R // 2) == 0, 1.0, 2.0))
    v1 = np.concatenate([np.cos(tht) * weight, -np.sin(tht) * weight], axis=2) / FFT_N
    ph = 2.0 * np.pi * ((r[:, None] * r[None, :]) % FFT_R) / FFT_R
    c, s = np.cos(ph), np.sin(ph)
    g_fwd = np.block([[c, s], [-s, c]])
    g_inv = np.block([[c, -s], [s, c]])
    as_bf16 = lambda a: a.astype(np.float32).astype(BF16)
    half = FFT_R // 2
    return dict(w1_full=as_bf16(w1), w1_half=as_bf16(w1[:, :, :half]), v1_half=as_bf16(v1[:, :half, :]),
                g_fwd=as_bf16(g_fwd), g_inv=as_bf16(g_inv))


def _pack_pair(x):
    h = x.shape[0] // 2
    hi = lax.bitcast_convert_type(x[:h].astype(BF16).astype(F32), jnp.uint32)
    lo = lax.bitcast_convert_type(x[h:].astype(BF16).astype(F32), jnp.uint32)
    return hi | (lo >> 16)


def _unpack_pair(p):
    hi = lax.bitcast_convert_type(p & jnp.uint32(0xFFFF0000), F32)
    lo = lax.bitcast_convert_type(p << 16, F32)
    return jnp.concatenate([hi, lo], axis=0).astype(BF16)


HALO = 16


def _in_proj_conv_kernel(x_ref, prev_ref, next_ref, g_ref, w_ref, cw_ref, cb_ref, o_ref, xn_ref):
    i = pl.program_id(0)
    tm = x_ref.shape[0]

    @pl.when(pl.program_id(1) == 0)
    def _():
        g = g_ref[...]
        before = jnp.where(i > 0, _rms(prev_ref[...], g, RMS_EPS), 0.0)
        after = jnp.where(i < pl.num_programs(0) - 1, _rms(next_ref[...], g, RMS_EPS), 0.0)
        xn_ref[:HALO] = before.astype(BF16)
        xn_ref[HALO:HALO + tm] = _rms(x_ref[...], g, RMS_EPS).astype(BF16)
        xn_ref[HALO + tm:] = after.astype(BF16)

    u = jnp.dot(xn_ref[...], w_ref[...], preferred_element_type=F32)
    y = (u[HALO - 1:HALO - 1 + tm] * cw_ref[0:1, :] + u[HALO:HALO + tm] * cw_ref[1:2, :]
         + u[HALO + 1:HALO + 1 + tm] * cw_ref[2:3, :] + cb_ref[...])
    for q in range(o_ref.shape[0]):
        o_ref[q] = y[:, q * LANE:(q + 1) * LANE]


def in_proj_conv(x, g, w, conv_w, conv_b, tm=PROJ_TM, tn=HY_IN_TN):
    s, k = x.shape
    n = w.shape[1]
    nb = tm // HALO
    return pl.pallas_call(
        _in_proj_conv_kernel,
        grid=(_steps(s, tm), _steps(n, tn)),
        in_specs=[
            pl.BlockSpec((tm, k), lambda i, j: (i, 0)),
            pl.BlockSpec((HALO, k), lambda i, j: (jnp.maximum(i * nb - 1, 0), 0)),
            pl.BlockSpec((HALO, k), lambda i, j: (jnp.minimum((i + 1) * nb, s // HALO - 1), 0)),
            pl.BlockSpec((1, k), lambda i, j: (0, 0)),
            pl.BlockSpec((k, tn), lambda i, j: (0, j)),
            pl.BlockSpec((3, tn), lambda i, j: (0, j)),
            pl.BlockSpec((1, tn), lambda i, j: (0, j)),
        ],
        out_specs=pl.BlockSpec((tn // LANE, tm, LANE), lambda i, j: (j, i, 0)),
        out_shape=jax.ShapeDtypeStruct((n // LANE, s, LANE), F32),
        scratch_shapes=[pltpu.VMEM((tm + 2 * HALO, k), BF16)],
        compiler_params=_params("parallel", "arbitrary"),
        name="hy_in_proj_conv",
    )(x, x, x, g.reshape(1, k), w, conv_w, conv_b.reshape(1, n))


def _filter_mlp_kernel(z_ref, w1_ref, b1_ref, f_ref, w2_ref, b2_ref, w3_ref, b3_ref, f3_ref, o_ref):
    hp = lax.Precision.HIGHEST
    f = f_ref[...]
    h = jnp.sin(f * (jnp.dot(z_ref[...], w1_ref[...], precision=hp, preferred_element_type=F32) + b1_ref[...]))
    h = jnp.sin(f * (jnp.dot(h, w2_ref[...], precision=hp, preferred_element_type=F32) + b2_ref[...]))
    o_ref[...] = jnp.sin(f3_ref[...] * (jnp.dot(h, w3_ref[...], precision=hp, preferred_element_type=F32) + b3_ref[...]))


def filter_mlp(zfeat, f_w1, f_b1, f_freq, f_w2, f_b2, f_w3, f_b3, tm=2048):
    n, e = zfeat.shape
    wd = f_w2.shape[0]
    row = lambda i: (i, 0)
    fixed = lambda i: (0, 0)
    vec = lambda v: v.reshape(1, wd)
    lane_pad = lambda a: jnp.pad(a, ((0, 0), (0, LANE - wd)))
    wide = pl.BlockSpec((1, LANE), fixed)
    return pl.pallas_call(
        _filter_mlp_kernel,
        grid=(_steps(n, tm),),
        in_specs=[pl.BlockSpec((tm, e), row), pl.BlockSpec((e, wd), fixed), pl.BlockSpec((1, wd), fixed),
                  pl.BlockSpec((1, wd), fixed), pl.BlockSpec((wd, wd), fixed), pl.BlockSpec((1, wd), fixed),
                  pl.BlockSpec((wd, LANE), fixed), wide, wide],
        out_specs=pl.BlockSpec((tm, LANE), row),
        out_shape=jax.ShapeDtypeStruct((n, LANE), F32),
        compiler_params=_params("parallel"),
        name="hy_filter_mlp",
    )(zfeat, f_w1, vec(f_b1), vec(f_freq), f_w2, vec(f_b2), lane_pad(f_w3), lane_pad(vec(f_b3)), lane_pad(vec(f_freq)))


def _rows_by_group(ref):
    t = [jnp.swapaxes(ref[i], 0, 1) for i in range(ref.shape[0])]
    return [jnp.concatenate([ti[r] for ti in t], axis=1) for r in range(FFT_G)]


def _store_by_group(ref, vals):
    for i in range(ref.shape[0]):
        ref[i] = jnp.swapaxes(jnp.stack([v[:, i * LANE:(i + 1) * LANE] for v in vals], axis=0), 0, 1)


def _kern_stage1_kernel(h_ref, w00_ref, w01_ref, w10_ref, w11_ref, dl_ref, w1_ref, kb_ref, nrm_ref, *, seq):
    h_rows = _rows_by_group(h_ref)
    g = pl.program_id(1)
    half = FFT_R // 2
    w_fwd = (w00_ref[...], w01_ref[...])
    w_bwd = (w10_ref[...], w11_ref[...])
    rate = dl_ref[...] * (-1.0 / (seq - 1))
    n1 = lax.broadcasted_iota(jnp.int32, (half, 1), 0)
    packed = [[] for _ in range(HY_ORDER)]
    acc = [jnp.zeros(nrm_ref.shape[1:], F32) for _ in range(HY_ORDER)]
    for r in range(FFT_G):
        n2 = g * FFT_G + r
        hs = h_rows[r].astype(BF16)
        lag_f = FFT_R * n1 + n2
        lag_b = seq - FFT_R * n1 - n2
        decay_f = jnp.exp(lag_f.astype(F32) * rate)
        decay_b = jnp.where(lag_b == seq, 0.0, jnp.exp(lag_b.astype(F32) * rate))
        for o in range(HY_ORDER):
            kf = jnp.dot(hs[:half], w_fwd[o], preferred_element_type=F32) * decay_f
            kb = jnp.dot(hs[half:], w_bwd[o], preferred_element_type=F32) * decay_b
            slab = jnp.concatenate([kf, kb], axis=0)
            acc[o] += jnp.sum(jnp.abs(slab), axis=0, keepdims=True)
            packed[o].append(_pack_pair(jnp.dot(w1_ref[r], slab.astype(BF16), preferred_element_type=F32)))
    for o in range(HY_ORDER):
        _store_by_group(kb_ref.at[o], packed[o])

    @pl.when(g == 0)
    def _():
        for o in range(HY_ORDER):
            nrm_ref[o] = acc[o]

    @pl.when(g > 0)
    def _():
        for o in range(HY_ORDER):
            nrm_ref[o] += acc[o]


def kern_stage1(h3, w4, deltas, w1_full, seq, tc=HY_TC):
    d = deltas.shape[0]
    jn = _steps(d, tc)
    h3v = h3.reshape(1, FFT_R, FFT_R, LANE)
    assert HY_ORDER == 2
    kern = functools.partial(_kern_stage1_kernel, seq=seq)
    w4_spec = lambda direction, order: pl.BlockSpec((LANE, tc), lambda j, g: (0, (direction * HY_ORDER + order) * jn + j))
    return pl.pallas_call(
        kern,
        grid=(jn, FFT_R // FFT_G),
        in_specs=[
            pl.BlockSpec((1, FFT_R, FFT_G, LANE), lambda j, g: (0, 0, g, 0)),
            w4_spec(0, 0), w4_spec(0, 1), w4_spec(1, 0), w4_spec(1, 1),
            pl.BlockSpec((1, tc), lambda j, g: (0, j)),
            pl.BlockSpec((FFT_G, 2 * FFT_K1, FFT_R), lambda j, g: (g, 0, 0)),
        ],
        out_specs=[
            pl.BlockSpec((HY_ORDER, HY_NQ, FFT_K1, FFT_G, LANE), lambda j, g: (0, j, 0, g, 0)),
            pl.BlockSpec((HY_ORDER, 1, tc), lambda j, g: (0, 0, j)),
        ],
        out_shape=[
            jax.ShapeDtypeStruct((HY_ORDER, d // LANE, FFT_K1, FFT_R, LANE), jnp.uint32),
            jax.ShapeDtypeStruct((HY_ORDER, 1, d), F32),
        ],
        compiler_params=_params("parallel", "arbitrary"),
        name="hy_kern_stage1",
    )(h3v, w4, w4, w4, w4, deltas.reshape(1, d), w1_full)


def _join_lanes(ref, idx):
    return jnp.concatenate([ref[(q,) + idx] for q in range(ref.shape[0])], axis=1)


def _kern_stage3_kernel(kb_ref, g_ref, o_ref):
    gm = g_ref[...]
    for r in range(FFT_G):
        b = _unpack_pair(_join_lanes(kb_ref.at[0], (r,)))
        o_ref[0, r] = jnp.dot(gm, b, preferred_element_type=F32).astype(o_ref.dtype)


def kern_stage3(kb, g_fwd, tc=HY_TC):
    d = kb.shape[1] * LANE
    return pl.pallas_call(
        _kern_stage3_kernel,
        grid=(HY_ORDER, FFT_K1 // FFT_G, _steps(d, tc)),
        in_specs=[
            pl.BlockSpec((1, HY_NQ, FFT_G, FFT_R, LANE), lambda o, kg, j: (o, j, kg, 0, 0)),
            pl.BlockSpec((2 * FFT_R, 2 * FFT_R), lambda o, kg, j: (0, 0)),
        ],
        out_specs=pl.BlockSpec((1, FFT_G, 2 * FFT_R, tc), lambda o, kg, j: (o, kg, 0, j)),
        out_shape=jax.ShapeDtypeStruct((HY_ORDER, FFT_K1, 2 * FFT_R, d), BF16),
        compiler_params=_params("parallel", "parallel", "parallel"),
        name="hy_kern_stage3",
    )(kb, g_fwd)


def _sig_stage1_kernel(z_ref, w1_ref, o_ref):
    z_rows = _rows_by_group(z_ref)
    _store_by_group(o_ref, [_pack_pair(jnp.dot(w1_ref[r], z_rows[r].astype(BF16), preferred_element_type=F32))
                            for r in range(FFT_G)])


def sig_stage1(z_src, slab0, d, w1_half, tc=HY_TC):
    s = z_src.shape[1]
    rows = s // FFT_R
    zv = z_src.reshape(z_src.shape[0], rows, FFT_R, LANE)
    return pl.pallas_call(
        _sig_stage1_kernel,
        grid=(FFT_R // FFT_G, _steps(d, tc)),
        in_specs=[
            pl.BlockSpec((HY_NQ, rows, FFT_G, LANE), lambda g, j: (slab0 // HY_NQ + j, 0, g, 0)),
            pl.BlockSpec((FFT_G, 2 * FFT_K1, rows), lambda g, j: (g, 0, 0)),
        ],
        out_specs=pl.BlockSpec((HY_NQ, FFT_K1, FFT_G, LANE), lambda g, j: (j, 0, g, 0)),
        out_shape=jax.ShapeDtypeStruct((d // LANE, FFT_K1, FFT_R, LANE), jnp.uint32),
        compiler_params=_params("parallel", "parallel"),
        name="hy_sig_stage1",
    )(zv, w1_half)


def _sig_stage3_kernel(zb_ref, ks_ref, gf_ref, gi_ref, o_ref):
    gf = gf_ref[...]
    gi = gi_ref[...]
    for r in range(FFT_G):
        x = jnp.dot(gf, _unpack_pair(_join_lanes(zb_ref, (r,))), preferred_element_type=F32)
        ks = ks_ref[0, r].astype(F32)
        xr, xi = x[:FFT_R], x[FFT_R:]
        kr, ki = ks[:FFT_R], ks[FFT_R:]
        y = jnp.concatenate([xr * kr - xi * ki, xr * ki + xi * kr], axis=0).astype(BF16)
        packed = _pack_pair(jnp.dot(gi, y, preferred_element_type=F32))
        for q in range(o_ref.shape[0]):
            o_ref[q, r] = packed[:, q * LANE:(q + 1) * LANE]


def sig_stage3(zb, kspec, order, g_fwd, g_inv, tc=HY_TC):
    d = zb.shape[0] * LANE
    blk = pl.BlockSpec((HY_NQ, FFT_G, FFT_R, LANE), lambda kg, j: (j, kg, 0, 0))
    mat = pl.BlockSpec((2 * FFT_R, 2 * FFT_R), lambda kg, j: (0, 0))
    return pl.pallas_call(
        _sig_stage3_kernel,
        grid=(FFT_K1 // FFT_G, _steps(d, tc)),
        in_specs=[blk, pl.BlockSpec((1, FFT_G, 2 * FFT_R, tc), lambda kg, j: (order, kg, 0, j)), mat, mat],
        out_specs=blk,
        out_shape=jax.ShapeDtypeStruct(zb.shape, jnp.uint32),
        compiler_params=_params("parallel", "parallel"),
        name="hy_sig_stage3",
    )(zb, kspec, g_fwd, g_inv)


def _sig_inverse1_kernel(yb_ref, v_ref, gate_ref, z_ref, nrm_ref, bias_ref, *rest, next_stage1):
    inv = 1.0 / nrm_ref[0]
    bias = bias_ref[0]
    yb_rows, gate_rows, z_rows = _rows_by_group(yb_ref), _rows_by_group(gate_ref), _rows_by_group(z_ref)
    out = []
    for r in range(FFT_G):
        conv = jnp.dot(v_ref[r], _unpack_pair(yb_rows[r]), preferred_element_type=F32) * inv
        out.append(gate_rows[r] * (conv + z_rows[r] * bias))
    if next_stage1:
        w1_ref, o_ref, zb_ref = rest
        _store_by_group(zb_ref, [_pack_pair(jnp.dot(w1_ref[r], out[r].astype(BF16), preferred_element_type=F32))
                                 for r in range(FFT_G)])
    else:
        o_ref, = rest
    _store_by_group(o_ref, out)


def sig_inverse1(yb, v1_half, gate_src, gate_slab0, z_src, z_slab0, nrm, bias, order, w1_half=None, tc=HY_TC):
    d = yb.shape[0] * LANE
    s = gate_src.shape[1]
    rows = s // FFT_R
    view = lambda a: a.reshape(a.shape[0], rows, FFT_R, LANE)
    sig = (HY_NQ, rows, FFT_G, LANE)
    freq = (HY_NQ, FFT_K1, FFT_G, LANE)
    in_specs = [
        pl.BlockSpec(freq, lambda g, j: (j, 0, g, 0)),
        pl.BlockSpec((FFT_G, rows, 2 * FFT_K1), lambda g, j: (g, 0, 0)),
        pl.BlockSpec(sig, lambda g, j: (gate_slab0 // HY_NQ + j, 0, g, 0)),
        pl.BlockSpec(sig, lambda g, j: (z_slab0 // HY_NQ + j, 0, g, 0)),
        pl.BlockSpec((1, 1, tc), lambda g, j: (order, 0, j)),
        pl.BlockSpec((1, 1, tc), lambda g, j: (order, 0, j)),
    ]
    args = [yb, v1_half, view(gate_src), view(z_src), nrm, bias.reshape(HY_ORDER, 1, d)]
    out_specs = [pl.BlockSpec(sig, lambda g, j: (j, 0, g, 0))]
    out_shape = [jax.ShapeDtypeStruct((d // LANE, rows, FFT_R, LANE), F32)]
    if w1_half is not None:
        in_specs.append(pl.BlockSpec((FFT_G, 2 * FFT_K1, rows), lambda g, j: (g, 0, 0)))
        args.append(w1_half)
        out_specs.append(pl.BlockSpec(freq, lambda g, j: (j, 0, g, 0)))
        out_shape.append(jax.ShapeDtypeStruct((d // LANE, FFT_K1, FFT_R, LANE), jnp.uint32))
    outs = pl.pallas_call(
        functools.partial(_sig_inverse1_kernel, next_stage1=w1_half is not None),
        grid=(FFT_R // FFT_G, _steps(d, tc)),
        in_specs=in_specs,
        out_specs=out_specs,
        out_shape=out_shape,
        compiler_params=_params("parallel", "parallel"),
        name="hy_sig_inverse1",
    )(*args)
    z_new = outs[0].reshape(d // LANE, s, LANE)
    return (z_new, outs[1]) if w1_half is not None else (z_new, None)


def _hyena_positions(seq):
    t = jnp.linspace(0.0, 1.0, seq, dtype=F32)[:, None]
    bands = (HY_EMB_DIM - 1) // 2
    w = 2.0 * math.pi * jnp.arange(seq, dtype=F32)[:, None] / seq
    f = jnp.linspace(1e-4, bands - 1, bands, dtype=F32)[None, :]
    z = jnp.concatenate([t, jnp.cos(f * w), -jnp.sin(f * w)], axis=-1)
    return jnp.pad(z, ((0, 0), (0, LANE - HY_EMB_DIM)))


def hyena_mixer(u, f_w1, f_b1, f_freq, f_w2, f_b2, f_w3, f_b3, f_w4, bias):
    s = u.shape[1]
    d = u.shape[0] * LANE // 3
    assert 2 * s == FFT_N and d % HY_TC == 0
    c = _dft_constants()
    h3 = filter_mlp(_hyena_positions(s), jnp.pad(f_w1, ((0, LANE - HY_EMB_DIM), (0, 0))), f_b1, f_freq, f_w2, f_b2, f_w3, f_b3)
    h3 = jnp.concatenate([h3, h3[:1], h3[:0:-1]], axis=0)
    w4 = jnp.pad(f_w4.astype(BF16), ((0, LANE - f_w4.shape[0]), (0, 0)))
    deltas = jnp.abs(jnp.linspace(HY_MIN_DECAY, HY_MAX_DECAY, d, dtype=F32))
    kb, nrm = kern_stage1(h3, w4, deltas, c["w1_full"], s)
    kspec = kern_stage3(kb, c["g_fwd"])
    nslab = d // LANE
    z_src, z_slab0 = u, 2 * nslab
    zb = sig_stage1(z_src, z_slab0, d, c["w1_half"])
    for o in range(HY_ORDER):
        yb = sig_stage3(zb, kspec, o, c["g_fwd"], c["g_inv"])
        w1_next = c["w1_half"] if o + 1 < HY_ORDER else None
        z_src, zb = sig_inverse1(yb, c["v1_half"], u, o * nslab, z_src, z_slab0, nrm, bias, o, w1_next)
        z_slab0 = 0
    return z_src


def hyena_layer(x, mix_norm, w_in, conv_w, conv_b, f_w1, f_b1, f_freq, f_w2, f_b2, f_w3, f_b3, f_w4, bias, w_out):
    u = in_proj_conv(x, mix_norm, w_in.astype(BF16), conv_w, conv_b)
    z = hyena_mixer(u, f_w1, f_b1, f_freq, f_w2, f_b2, f_w3, f_b3, f_w4, bias)
    return matmul_residual(z, w_out.astype(BF16), x, a_in_slabs=True)


def _rope_tables(seq, dim):
    inv = 1.0 / (ROPE_THETA ** (jnp.arange(0, dim, 2, dtype=F32) / dim))
    ang = jnp.arange(seq, dtype=F32)[:, None] * inv[None, :]
    return jnp.cos(ang), jnp.sin(ang)


def kernel(x, l0_mix_norm, l0_mla_w_in, l0_mla_q_norm, l0_mla_w_uq, l0_mla_kv_norm, l0_mla_w_ukv, l0_mla_w_o, l0_ffn_norm, l0_ffn_w_gate, l0_ffn_w_up, l0_ffn_w_down, l1_mix_norm, l1_hy_w_in, l1_hy_conv_w, l1_hy_conv_b, l1_hy_f_w1, l1_hy_f_b1, l1_hy_f_freq, l1_hy_f_w2, l1_hy_f_b2, l1_hy_f_w3, l1_hy_f_b3, l1_hy_f_w4, l1_hy_bias, l1_hy_w_out, l1_ffn_norm, l1_ffn_w_gate, l1_ffn_w_up, l1_ffn_w_down, l2_mix_norm, l2_da_w_qkv, l2_da_lq1, l2_da_lk1, l2_da_lq2, l2_da_lk2, l2_da_subln, l2_da_w_o, l2_ffn_norm, l2_ffn_w_gate, l2_ffn_w_up, l2_ffn_w_down, l3_mix_norm, l3_mla_w_in, l3_mla_q_norm, l3_mla_w_uq, l3_mla_kv_norm, l3_mla_w_ukv, l3_mla_w_o, l3_ffn_norm, l3_ffn_w_gate, l3_ffn_w_up, l3_ffn_w_down, final_norm):
    b, s, d = x.shape
    assert b == 1
    h = x.reshape(s, d)

    mc, ms = _rope_tables(s, MLA_ROPE)
    mla_tab = jnp.concatenate([mc, mc, ms, ms], axis=-1)
    mla = (mla_tab, mc.T, ms.T)
    dc, ds = _rope_tables(s, DIFF_HEAD_DIM)
    diff_rope = (jnp.concatenate([dc, dc], axis=-1), jnp.concatenate([-ds, ds], axis=-1))
    diff_rope_t = (dc.T, ds.T)

    def ffn(h, g, wg, wu, wd, out_norm_gain=None):
        return ffn_block(h, g, wg.astype(BF16), wu.astype(BF16), wd.astype(BF16), out_norm_gain)

    h = mla_layer(h, l0_mix_norm, l0_mla_w_in, l0_mla_q_norm, l0_mla_w_uq, l0_mla_kv_norm, l0_mla_w_ukv, l0_mla_w_o, *mla)
    h = ffn(h, l0_ffn_norm, l0_ffn_w_gate, l0_ffn_w_up, l0_ffn_w_down)
    h = hyena_layer(h, l1_mix_norm, l1_hy_w_in, l1_hy_conv_w, l1_hy_conv_b, l1_hy_f_w1, l1_hy_f_b1, l1_hy_f_freq, l1_hy_f_w2, l1_hy_f_b2, l1_hy_f_w3, l1_hy_f_b3, l1_hy_f_w4, l1_hy_bias, l1_hy_w_out)
    h = ffn(h, l1_ffn_norm, l1_ffn_w_gate, l1_ffn_w_up, l1_ffn_w_down)
    h = diff_layer(h, l2_mix_norm, l2_da_w_qkv, l2_da_lq1, l2_da_lk1, l2_da_lq2, l2_da_lk2, l2_da_subln, l2_da_w_o, diff_rope, diff_rope_t, 2)
    h = ffn(h, l2_ffn_norm, l2_ffn_w_gate, l2_ffn_w_up, l2_ffn_w_down)
    h = mla_layer(h, l3_mix_norm, l3_mla_w_in, l3_mla_q_norm, l3_mla_w_uq, l3_mla_kv_norm, l3_mla_w_ukv, l3_mla_w_o, *mla)
    h = ffn(h, l3_ffn_norm, l3_ffn_w_gate, l3_ffn_w_up, l3_ffn_w_down, final_norm)
    return h.reshape(b, s, d)
```

```python
import functools
import math

import jax
import jax.numpy as jnp
import numpy as np
from jax import lax
from jax.experimental import pallas as pl
from jax.experimental.pallas import tpu as pltpu

F32 = jnp.float32
BF16 = jnp.bfloat16

D_MODEL = 2048
RMS_EPS = 1e-6
ROPE_THETA = 10000.0
LOG2E = math.log2(math.e)

MLA_HEADS = 16
MLA_Q_LORA = 768
MLA_KV_LORA = 512
MLA_NOPE = 128
MLA_ROPE = 64
MLA_V = 128
MLA_QK = MLA_NOPE + MLA_ROPE
MLA_SCALE = MLA_QK ** -0.5
MLA_QK_PAD = 256

HY_ORDER = 2
HY_EMB_DIM = 33
HY_TARGET = 1e-2
HY_MAX_DECAY = math.log(HY_TARGET) / 0.3
HY_MIN_DECAY = math.log(HY_TARGET) / 1.5

DIFF_HEAD_DIM = 128
DIFF_HEADS = D_MODEL // (2 * DIFF_HEAD_DIM)
DIFF_QK = DIFF_HEADS * 2 * DIFF_HEAD_DIM
DIFF_SCALE = DIFF_HEAD_DIM ** -0.5
DIFF_EPS = 1e-5

LANE = 128
VMEM_BYTES = 64 * 1024 * 1024
VMEM_LIMIT = VMEM_BYTES * 7 // 8
NEG_BIG = -1e30

PROJ_TM, PROJ_TN = 1024, 1024
MLA_IN_TM = 1024
MLA_HEADS_PER_STEP = 8
FFN_TM, FFN_TF = 1024, 512
MLA_TQ, DIFF_TQ, ATTN_TK = 4096, 2048, 512
HY_IN_TN = 1024


def _params(*sem):
    return pltpu.CompilerParams(dimension_semantics=sem, vmem_limit_bytes=VMEM_LIMIT)


def _steps(dim, tile):
    assert dim % tile == 0, f"tile {tile} does not divide {dim}"
    return dim // tile


def _rms(x, g, eps):
    return x * lax.rsqrt(jnp.mean(x * x, axis=-1, keepdims=True) + eps) * g


def _dot_nt(a, b):
    return lax.dot_general(a, b, (((1,), (1,)), ((), ())), preferred_element_type=F32)


def _rope_rows(y, c, s):
    h = c.shape[0]
    y1, y2 = y[:h], y[h:]
    return y1 * c - y2 * s, y2 * c + y1 * s


def _norm_matmul_kernel(x_ref, g_ref, w_ref, *rest, rope, scale):
    o_ref, xn_ref = rest[-2:]

    @pl.when(pl.program_id(1) == 0)
    def _():
        xn_ref[...] = _rms(x_ref[...], g_ref[...], RMS_EPS).astype(BF16)

    y = jnp.dot(xn_ref[...], w_ref[...], preferred_element_type=F32)
    if rope:
        c = rest[0][...] * scale
        s = rest[1][...] * scale
        hd = c.shape[1]
        for g in range(y.shape[1] // hd):
            t = y[:, g * hd:(g + 1) * hd]
            o_ref[:, g * hd:(g + 1) * hd] = (t * c + pltpu.roll(t, hd // 2, axis=1) * s).astype(o_ref.dtype)
    else:
        o_ref[...] = (y * scale).astype(o_ref.dtype)


def norm_matmul(x, g, w, out_dtype, rope=None, scale=1.0, tm=PROJ_TM, tn=PROJ_TN):
    m, k = x.shape
    n = w.shape[1]
    in_specs = [
        pl.BlockSpec((tm, k), lambda i, j: (i, 0)),
        pl.BlockSpec((1, k), lambda i, j: (0, 0)),
        pl.BlockSpec((k, tn), lambda i, j: (0, j)),
    ]
    args = [x, g.reshape(1, k), w]
    if rope is not None:
        in_specs += [pl.BlockSpec((tm, rope[0].shape[1]), lambda i, j: (i, 0))] * 2
        args += list(rope)
    return pl.pallas_call(
        functools.partial(_norm_matmul_kernel, rope=rope is not None, scale=scale),
        grid=(_steps(m, tm), _steps(n, tn)),
        in_specs=in_specs,
        out_specs=pl.BlockSpec((tm, tn), lambda i, j: (i, j)),
        out_shape=jax.ShapeDtypeStruct((m, n), out_dtype),
        scratch_shapes=[pltpu.VMEM((tm, k), BF16)],
        compiler_params=_params("parallel", "arbitrary"),
        name="norm_matmul",
    )(*args)


def _norm_matmul_t_kernel(x_ref, g_ref, wt_ref, *rest, rope, scale):
    o_ref, xn_ref = rest[-2:]

    @pl.when(pl.program_id(1) == 0)
    def _():
        xn_ref[...] = _rms(x_ref[...], g_ref[...], RMS_EPS).astype(BF16)

    y = _dot_nt(wt_ref[...], xn_ref[...]) * scale
    if rope:
        c = rest[0][...]
        s = rest[1][...]
        half = c.shape[0]
        for g in range(y.shape[0] // (2 * half)):
            r1, r2 = _rope_rows(y[2 * g * half:2 * (g + 1) * half], c, s)
            o_ref[2 * g * half:(2 * g + 1) * half, :] = r1.astype(o_ref.dtype)
            o_ref[(2 * g + 1) * half:2 * (g + 1) * half, :] = r2.astype(o_ref.dtype)
    else:
        o_ref[...] = y.astype(o_ref.dtype)


def norm_matmul_t(x, g, wt, out_dtype, rope_t=None, scale=1.0, tm=PROJ_TM, tn=PROJ_TN):
    m, k = x.shape
    n = wt.shape[0]
    in_specs = [
        pl.BlockSpec((tm, k), lambda i, j: (i, 0)),
        pl.BlockSpec((1, k), lambda i, j: (0, 0)),
        pl.BlockSpec((tn, k), lambda i, j: (j, 0)),
    ]
    args = [x, g.reshape(1, k), wt]
    if rope_t is not None:
        in_specs += [pl.BlockSpec((rope_t[0].shape[0], tm), lambda i, j: (0, i))] * 2
        args += list(rope_t)
    return pl.pallas_call(
        functools.partial(_norm_matmul_t_kernel, rope=rope_t is not None, scale=scale),
        grid=(_steps(m, tm), _steps(n, tn)),
        in_specs=in_specs,
        out_specs=pl.BlockSpec((tn, tm), lambda i, j: (j, i)),
        out_shape=jax.ShapeDtypeStruct((n, m), out_dtype),
        scratch_shapes=[pltpu.VMEM((tm, k), BF16)],
        compiler_params=_params("parallel", "arbitrary"),
        name="norm_matmul_t",
    )(*args)


def _matmul_res_kernel(a_ref, w_ref, r_ref, o_ref):
    if len(a_ref.shape) == 3:
        a = jnp.concatenate([a_ref[q] for q in range(a_ref.shape[0])], axis=1)
    else:
        a = a_ref[...]
    o_ref[...] = r_ref[...] + jnp.dot(a.astype(BF16), w_ref[...], preferred_element_type=F32)


def matmul_residual(a, w, res, a_in_slabs=False, tm=PROJ_TM, tn=PROJ_TN):
    k, n = w.shape
    m = res.shape[0]
    if a_in_slabs:
        a_spec = pl.BlockSpec((k // LANE, tm, LANE), lambda i, j: (0, i, 0))
    else:
        a_spec = pl.BlockSpec((tm, k), lambda i, j: (i, 0))
    return pl.pallas_call(
        _matmul_res_kernel,
        grid=(_steps(m, tm), _steps(n, tn)),
        in_specs=[
            a_spec,
            pl.BlockSpec((k, tn), lambda i, j: (0, j)),
            pl.BlockSpec((tm, tn), lambda i, j: (i, j)),
        ],
        out_specs=pl.BlockSpec((tm, tn), lambda i, j: (i, j)),
        out_shape=jax.ShapeDtypeStruct((m, n), F32),
        compiler_params=_params("parallel", "parallel"),
        name="matmul_residual",
    )(a, w, res)


def _ffn_kernel(x_ref, g_ref, wg_ref, wu_ref, wd_ref, *rest, out_norm):
    o_ref, xn_ref = rest[-2:]

    @pl.when(pl.program_id(1) == 0)
    def _():
        x = x_ref[...]
        xn_ref[...] = _rms(x, g_ref[...], RMS_EPS).astype(BF16)
        o_ref[...] = x

    xn = xn_ref[...]
    a = jnp.dot(xn, wg_ref[...], preferred_element_type=F32)
    b = jnp.dot(xn, wu_ref[...], preferred_element_type=F32)
    h = (a * jax.nn.sigmoid(a) * b).astype(BF16)
    o_ref[...] += jnp.dot(h, wd_ref[...], preferred_element_type=F32)

    if out_norm:
        @pl.when(pl.program_id(1) == pl.num_programs(1) - 1)
        def _():
            o_ref[...] = _rms(o_ref[...], rest[0][...], RMS_EPS)


def ffn_block(x, g, wg, wu, wd, out_norm_gain=None, tm=FFN_TM, tf=FFN_TF):
    m, d = x.shape
    f = wg.shape[1]
    vec = pl.BlockSpec((1, d), lambda i, j: (0, 0))
    in_specs = [
        pl.BlockSpec((tm, d), lambda i, j: (i, 0)),
        vec,
        pl.BlockSpec((d, tf), lambda i, j: (0, j)),
        pl.BlockSpec((d, tf), lambda i, j: (0, j)),
        pl.BlockSpec((tf, d), lambda i, j: (j, 0)),
    ]
    args = [x, g.reshape(1, d), wg, wu, wd]
    if out_norm_gain is not None:
        in_specs.append(vec)
        args.append(out_norm_gain.reshape(1, d))
    return pl.pallas_call(
        functools.partial(_ffn_kernel, out_norm=out_norm_gain is not None),
        grid=(_steps(m, tm), _steps(f, tf)),
        in_specs=in_specs,
        out_specs=pl.BlockSpec((tm, d), lambda i, j: (i, 0)),
        out_shape=jax.ShapeDtypeStruct((m, d), F32),
        scratch_shapes=[pltpu.VMEM((tm, d), BF16)],
        compiler_params=_params("parallel", "arbitrary"),
        name="ffn_block",
    )(*args)


def _flash_t(qs, k_of_part, k_ref, vt_ref, sa_ref, sb_ref, acc_ref, *, tk):
    parts = len(qs)
    tp = qs[0].shape[1]
    nk = k_ref.shape[0] // tk
    assert nk % 2 == 0 and nk >= 4

    def chunk(c):
        return pl.ds(c * tk if isinstance(c, int) else pl.multiple_of(c * tk, tk), tk)

    def scores(c, s_ref):
        k = k_ref[chunk(c), :]
        mc = ()
        for i in range(parts):
            st = jnp.dot(k_of_part(k, i), qs[i], preferred_element_type=F32)
            s_ref[i] = st
            mc += (jnp.max(st, axis=0, keepdims=True),)
        return mc

    def softmax_pv(c, s_ref, mc, state):
        vt = vt_ref[:, chunk(c)]
        new = ()
        for i in range(parts):
            m, l = state[2 * i:2 * i + 2]
            m_new = jnp.maximum(m, mc[i])
            alpha = jnp.exp2(m - m_new)
            p = jnp.exp2(s_ref[i] - m_new)
            l = alpha * l + jnp.sum(p, axis=0, keepdims=True)
            acc_ref[i] = alpha * acc_ref[i] + jnp.dot(vt, p.astype(BF16), preferred_element_type=F32)
            new += (m_new, l)
        return new

    def body(c2, carry):
        mc_a, state = carry[:parts], carry[parts:]
        c = 2 * c2
        mc_b = scores(c + 1, sb_ref)
        state = softmax_pv(c, sa_ref, mc_a, state)
        mc_a = scores(c + 2, sa_ref)
        state = softmax_pv(c + 1, sb_ref, mc_b, state)
        return mc_a + state

    acc_ref[...] = jnp.zeros(acc_ref.shape, F32)
    state0 = (jnp.full((1, tp), NEG_BIG, F32), jnp.zeros((1, tp), F32)) * parts
    carry = lax.fori_loop(0, nk // 2 - 1, body, scores(0, sa_ref) + state0)
    mc_a, state = carry[:parts], carry[parts:]
    mc_b = scores(nk - 1, sb_ref)
    state = softmax_pv(nk - 2, sa_ref, mc_a, state)
    state = softmax_pv(nk - 1, sb_ref, mc_b, state)
    return state[1::2]


MLA_KR_PAD = MLA_QK_PAD - MLA_NOPE


def _rope_pair(t, tab):
    t = t * tab
    return t + pltpu.roll(t, MLA_ROPE, axis=1)


def _mla_in_kernel(x_ref, g_ref, w_ref, qn_ref, kvn_ref, tab_ref, cq_ref, ckv_ref, kr_ref):
    xn = _rms(x_ref[...], g_ref[...], RMS_EPS).astype(BF16)
    h = jnp.dot(xn, w_ref[...], preferred_element_type=F32)
    cq_ref[...] = _rms(h[:, :MLA_Q_LORA], qn_ref[...], RMS_EPS).astype(BF16)
    ckv_ref[...] = _rms(h[:, MLA_Q_LORA:MLA_Q_LORA + MLA_KV_LORA], kvn_ref[...], RMS_EPS).astype(BF16)
    kr = _rope_pair(h[:, MLA_Q_LORA + MLA_KV_LORA:], tab_ref[...])
    lane = lax.broadcasted_iota(jnp.int32, kr.shape, 1)
    kr_ref[...] = jnp.where(lane < MLA_ROPE, kr, 0.0).astype(BF16)


def mla_in(x, g, w_in_ext, q_norm, kv_norm, tab, tm=MLA_IN_TM):
    m, d = x.shape
    n = w_in_ext.shape[1]
    row = lambda i: (i, 0)
    fixed = lambda i: (0, 0)
    return pl.pallas_call(
        _mla_in_kernel,
        grid=(_steps(m, tm),),
        in_specs=[
            pl.BlockSpec((tm, d), row),
            pl.BlockSpec((1, d), fixed),
            pl.BlockSpec((d, n), fixed),
            pl.BlockSpec((1, MLA_Q_LORA), fixed),
            pl.BlockSpec((1, MLA_KV_LORA), fixed),
            pl.BlockSpec((tm, MLA_KR_PAD), row),
        ],
        out_specs=[
            pl.BlockSpec((tm, MLA_Q_LORA), row),
            pl.BlockSpec((tm, MLA_KV_LORA), row),
            pl.BlockSpec((tm, MLA_KR_PAD), row),
        ],
        out_shape=[
            jax.ShapeDtypeStruct((m, MLA_Q_LORA), BF16),
            jax.ShapeDtypeStruct((m, MLA_KV_LORA), BF16),
            jax.ShapeDtypeStruct((m, MLA_KR_PAD), BF16),
        ],
        compiler_params=_params("parallel"),
        name="mla_in",
    )(x, g.reshape(1, d), w_in_ext, q_norm.reshape(1, -1), kv_norm.reshape(1, -1), tab)


def _mla_qt_kernel(c_ref, wt_ref, cos_ref, sin_ref, q_ref, *, heads):
    y = _dot_nt(wt_ref[...], c_ref[...]) * (MLA_SCALE * LOG2E)
    c = cos_ref[...]
    s = sin_ref[...]
    half = MLA_ROPE // 2
    zeros = jnp.zeros((MLA_QK_PAD - MLA_QK, y.shape[1]), BF16)
    for h in range(heads):
        src = h * MLA_QK
        dst = h * MLA_QK_PAD
        q_ref[dst:dst + MLA_NOPE, :] = y[src:src + MLA_NOPE].astype(BF16)
        r1, r2 = _rope_rows(y[src + MLA_NOPE:src + MLA_QK], c, s)
        q_ref[dst + MLA_NOPE:dst + MLA_NOPE + half, :] = r1.astype(BF16)
        q_ref[dst + MLA_NOPE + half:dst + MLA_QK, :] = r2.astype(BF16)
        q_ref[dst + MLA_QK:dst + MLA_QK_PAD, :] = zeros


def mla_qt(cq, w_uq_t, cos_t, sin_t, tm=PROJ_TM, heads_per_step=MLA_HEADS_PER_STEP):
    m, k = cq.shape
    tn_in = heads_per_step * MLA_QK
    tn_out = heads_per_step * MLA_QK_PAD
    return pl.pallas_call(
        functools.partial(_mla_qt_kernel, heads=heads_per_step),
        grid=(_steps(m, tm), _steps(MLA_HEADS, heads_per_step)),
        in_specs=[
            pl.BlockSpec((tm, k), lambda i, j: (i, 0)),
            pl.BlockSpec((tn_in, k), lambda i, j: (j, 0)),
            pl.BlockSpec((MLA_ROPE // 2, tm), lambda i, j: (0, i)),
            pl.BlockSpec((MLA_ROPE // 2, tm), lambda i, j: (0, i)),
        ],
        out_specs=pl.BlockSpec((tn_out, tm), lambda i, j: (j, i)),
        out_shape=jax.ShapeDtypeStruct((MLA_HEADS * MLA_QK_PAD, m), BF16),
        compiler_params=_params("parallel", "parallel"),
        name="mla_qt",
    )(cq, w_uq_t, cos_t, sin_t)


def _mla_kv_kernel(c_ref, wk_ref, wvt_ref, kr_ref, k_ref, vt_ref, *, heads):
    c = c_ref[...]
    yk = jnp.dot(c, wk_ref[...], preferred_element_type=F32)
    kr = kr_ref[...]
    for h in range(heads):
        dst = h * MLA_QK_PAD
        k_ref[:, dst:dst + MLA_NOPE] = yk[:, h * MLA_NOPE:(h + 1) * MLA_NOPE].astype(BF16)
        k_ref[:, dst + MLA_NOPE:dst + MLA_QK_PAD] = kr
    vt_ref[...] = _dot_nt(wvt_ref[...], c).astype(BF16)


def mla_kv(ckv, w_k, w_v_t, kr, tm=PROJ_TM, heads_per_step=MLA_HEADS_PER_STEP):
    m, k = ckv.shape
    hp = heads_per_step
    return pl.pallas_call(
        functools.partial(_mla_kv_kernel, heads=hp),
        grid=(_steps(m, tm), _steps(MLA_HEADS, hp)),
        in_specs=[
            pl.BlockSpec((tm, k), lambda i, j: (i, 0)),
            pl.BlockSpec((k, hp * MLA_NOPE), lambda i, j: (0, j)),
            pl.BlockSpec((hp * MLA_V, k), lambda i, j: (j, 0)),
            pl.BlockSpec((tm, MLA_KR_PAD), lambda i, j: (i, 0)),
        ],
        out_specs=[
            pl.BlockSpec((tm, hp * MLA_QK_PAD), lambda i, j: (i, j)),
            pl.BlockSpec((hp * MLA_V, tm), lambda i, j: (j, i)),
        ],
        out_shape=[
            jax.ShapeDtypeStruct((m, MLA_HEADS * MLA_QK_PAD), BF16),
            jax.ShapeDtypeStruct((MLA_HEADS * MLA_V, m), BF16),
        ],
        compiler_params=_params("parallel", "parallel"),
        name="mla_kv",
    )(ckv, w_k, w_v_t, kr)


def _mla_attn_kernel(qt_ref, k_ref, vt_ref, o_ref, sa_ref, sb_ref, acc_ref, *, tk):
    parts, _, tp = sa_ref.shape
    qs = [qt_ref[:, i * tp:(i + 1) * tp] for i in range(parts)]
    ls = _flash_t(qs, lambda k, i: k, k_ref, vt_ref, sa_ref, sb_ref, acc_ref, tk=tk)
    for i in range(parts):
        o_ref[i * tp:(i + 1) * tp, :] = (acc_ref[i] / ls[i]).T.astype(o_ref.dtype)


def mla_attention(qt, k, vt, tq=MLA_TQ, tk=ATTN_TK, parts=2):
    s = k.shape[0]
    return pl.pallas_call(
        functools.partial(_mla_attn_kernel, tk=tk),
        grid=(MLA_HEADS, _steps(s, tq)),
        in_specs=[
            pl.BlockSpec((MLA_QK_PAD, tq), lambda h, i: (h, i)),
            pl.BlockSpec((s, MLA_QK_PAD), lambda h, i: (0, h)),
            pl.BlockSpec((MLA_V, s), lambda h, i: (h, 0)),
        ],
        out_specs=pl.BlockSpec((tq, MLA_V), lambda h, i: (i, h)),
        out_shape=jax.ShapeDtypeStruct((s, MLA_HEADS * MLA_V), BF16),
        scratch_shapes=[
            pltpu.VMEM((parts, tk, tq // parts), F32),
            pltpu.VMEM((parts, tk, tq // parts), F32),
            pltpu.VMEM((parts, MLA_V, tq // parts), F32),
        ],
        compiler_params=_params("parallel", "parallel"),
        name="mla_attention",
    )(qt, k, vt)


def _rot_cols(w):
    half = w.shape[-1] // 2
    return jnp.concatenate([-w[..., half:], w[..., :half]], axis=-1)


def mla_layer(x, mix_norm, w_in, q_norm, w_uq, kv_norm, w_ukv, w_o, tab, cos_t, sin_t):
    w_kr = w_in[:, MLA_Q_LORA + MLA_KV_LORA:]
    w_in_ext = jnp.concatenate([w_in, _rot_cols(w_kr)], axis=1).astype(BF16)
    w_uq_t = w_uq.astype(BF16).T
    wkv = w_ukv.astype(BF16).reshape(MLA_KV_LORA, MLA_HEADS, MLA_NOPE + MLA_V)
    w_k = wkv[..., :MLA_NOPE].reshape(MLA_KV_LORA, -1)
    w_v_t = wkv[..., MLA_NOPE:].reshape(MLA_KV_LORA, -1).T
    cq, ckv, kr = mla_in(x, mix_norm, w_in_ext, q_norm, kv_norm, tab)
    qt = mla_qt(cq, w_uq_t, cos_t, sin_t)
    k, vt = mla_kv(ckv, w_k, w_v_t, kr)
    o = mla_attention(qt, k, vt)
    return matmul_residual(o, w_o.astype(BF16), x)


def _diff_attn_kernel(lam_ref, qt_ref, k_ref, vt_ref, g_ref, o_ref, sa_ref, sb_ref, acc_ref, *, tk, lam_init):
    d = DIFF_HEAD_DIM
    qs = [qt_ref[:d, :], qt_ref[d:, :]]
    l0, l1 = _flash_t(qs, lambda k, i: k[:, i * d:(i + 1) * d], k_ref, vt_ref, sa_ref, sb_ref, acc_ref, tk=tk)
    ot = acc_ref[0] / l0 - lam_ref[0, 0] * (acc_ref[1] / l1)
    o = _rms(ot.T, g_ref[...], DIFF_EPS) * (1.0 - lam_init)
    o_ref[...] = o.astype(o_ref.dtype)


def diff_attention(qt, k, vt, lam, subln, lam_init, tq=DIFF_TQ, tk=ATTN_TK):
    s = k.shape[0]
    hd = 2 * DIFF_HEAD_DIM
    return pl.pallas_call(
        functools.partial(_diff_attn_kernel, tk=tk, lam_init=lam_init),
        grid=(DIFF_HEADS, _steps(s, tq)),
        in_specs=[
            pl.BlockSpec(memory_space=pltpu.SMEM),
            pl.BlockSpec((hd, tq), lambda h, i: (h, i)),
            pl.BlockSpec((s, hd), lambda h, i: (0, h)),
            pl.BlockSpec((hd, s), lambda h, i: (h, 0)),
            pl.BlockSpec((1, hd), lambda h, i: (0, 0)),
        ],
        out_specs=pl.BlockSpec((tq, hd), lambda h, i: (i, h)),
        out_shape=jax.ShapeDtypeStruct((s, DIFF_HEADS * hd), BF16),
        scratch_shapes=[
            pltpu.VMEM((2, tk, tq), F32),
            pltpu.VMEM((2, tk, tq), F32),
            pltpu.VMEM((2, hd, tq), F32),
        ],
        compiler_params=_params("parallel", "parallel"),
        name="diff_attention",
    )(lam, qt, k, vt, subln.reshape(1, hd))


def _lambda_kernel(lq1_ref, lk1_ref, lq2_ref, lk2_ref, o_ref, *, lam_init):
    a = jnp.sum(lq1_ref[...] * lk1_ref[...], axis=-1, keepdims=True)
    b = jnp.sum(lq2_ref[...] * lk2_ref[...], axis=-1, keepdims=True)
    o_ref[...] = jnp.exp(a) - jnp.exp(b) + lam_init


def diff_lambda(lq1, lk1, lq2, lk2, lam_init):
    args = [v.reshape(1, -1) for v in (lq1, lk1, lq2, lk2)]
    return pl.pallas_call(
        functools.partial(_lambda_kernel, lam_init=lam_init),
        out_shape=jax.ShapeDtypeStruct((1, 1), F32),
        name="diff_lambda",
    )(*args)


def diff_layer(x, mix_norm, w_qkv, lq1, lk1, lq2, lk2, subln, w_o, rope, rope_t, layer_idx):
    lam_init = 0.8 - 0.6 * math.exp(-0.3 * layer_idx)
    w = w_qkv.astype(BF16)
    qt = norm_matmul_t(x, mix_norm, w[:, :DIFF_QK].T, BF16, rope_t=rope_t, scale=DIFF_SCALE * LOG2E)
    k = norm_matmul(x, mix_norm, w[:, DIFF_QK:2 * DIFF_QK], BF16, rope=rope)
    vt = norm_matmul_t(x, mix_norm, w[:, 2 * DIFF_QK:].T, BF16)
    lam = diff_lambda(lq1, lk1, lq2, lk2, lam_init)
    o = diff_attention(qt, k, vt, lam, subln, lam_init)
    return matmul_residual(o, w_o.astype(BF16), x)


FFT_R = 128
FFT_N = FFT_R * FFT_R
FFT_G = 8
FFT_K1 = 72
HY_TC = 1024
HY_NQ = HY_TC // LANE


def _dft_constants():
    r = np.arange(FFT_R)
    n2, k1, n1 = r[:, None, None], np.arange(FFT_K1)[None, :, None], r[None, None, :]
    th = 2.0 * np.pi * ((k1 * (FFT_R * n1 + n2)) % FFT_N) / FFT_N
    w1 = np.concatenate([np.cos(th), -np.sin(th)], axis=1)
    tht = np.transpose(th, (0, 2, 1))
    weight = np.where(np.arange(FFT_K1) > FFT_R // 2, 0.0, np.where(np.arange(FFT_K1) % (FFT_R // 2) == 0, 1.0, 2.0))
    v1 = np.concatenate([np.cos(tht) * weight, -np.sin(tht) * weight], axis=2) / FFT_N
    ph = 2.0 * np.pi * ((r[:, None] * r[None, :]) % FFT_R) / FFT_R
    c, s = np.cos(ph), np.sin(ph)
    g_fwd = np.block([[c, s], [-s, c]])
    g_inv = np.block([[c, -s], [s, c]])
    as_bf16 = lambda a: a.astype(np.float32).astype(BF16)
    half = FFT_R // 2
    return dict(w1_full=as_bf16(w1), w1_half=as_bf16(w1[:, :, :half]), v1_half=as_bf16(v1[:, :half, :]),
                g_fwd=as_bf16(g_fwd), g_inv=as_bf16(g_inv))


def _pack_pair(x):
    h = x.shape[0] // 2
    hi = lax.bitcast_convert_type(x[:h].astype(BF16).astype(F32), jnp.uint32)
    lo = lax.bitcast_convert_type(x[h:].astype(BF16).astype(F32), jnp.uint32)
    return hi | (lo >> 16)


def _unpack_pair(p):
    hi = lax.bitcast_convert_type(p & jnp.uint32(0xFFFF0000), F32)
    lo = lax.bitcast_convert_type(p << 16, F32)
    return jnp.concatenate([hi, lo], axis=0).astype(BF16)


HALO = 16


def _in_proj_conv_kernel(x_ref, prev_ref, next_ref, g_ref, w_ref, cw_ref, cb_ref, o_ref, xn_ref):
    i = pl.program_id(0)
    tm = x_ref.shape[0]

    @pl.when(pl.program_id(1) == 0)
    def _():
        g = g_ref[...]
        before = jnp.where(i > 0, _rms(prev_ref[...], g, RMS_EPS), 0.0)
        after = jnp.where(i < pl.num_programs(0) - 1, _rms(next_ref[...], g, RMS_EPS), 0.0)
        xn_ref[:HALO] = before.astype(BF16)
        xn_ref[HALO:HALO + tm] = _rms(x_ref[...], g, RMS_EPS).astype(BF16)
        xn_ref[HALO + tm:] = after.astype(BF16)

    u = jnp.dot(xn_ref[...], w_ref[...], preferred_element_type=F32)
    y = (u[HALO - 1:HALO - 1 + tm] * cw_ref[0:1, :] + u[HALO:HALO + tm] * cw_ref[1:2, :]
         + u[HALO + 1:HALO + 1 + tm] * cw_ref[2:3, :] + cb_ref[...])
    for q in range(o_ref.shape[0]):
        o_ref[q] = y[:, q * LANE:(q + 1) * LANE]


def in_proj_conv(x, g, w, conv_w, conv_b, tm=PROJ_TM, tn=HY_IN_TN):
    s, k = x.shape
    n = w.shape[1]
    nb = tm // HALO
    return pl.pallas_call(
        _in_proj_conv_kernel,
        grid=(_steps(s, tm), _steps(n, tn)),
        in_specs=[
            pl.BlockSpec((tm, k), lambda i, j: (i, 0)),
            pl.BlockSpec((HALO, k), lambda i, j: (jnp.maximum(i * nb - 1, 0), 0)),
            pl.BlockSpec((HALO, k), lambda i, j: (jnp.minimum((i + 1) * nb, s // HALO - 1), 0)),
            pl.BlockSpec((1, k), lambda i, j: (0, 0)),
            pl.BlockSpec((k, tn), lambda i, j: (0, j)),
            pl.BlockSpec((3, tn), lambda i, j: (0, j)),
            pl.BlockSpec((1, tn), lambda i, j: (0, j)),
        ],
        out_specs=pl.BlockSpec((tn // LANE, tm, LANE), lambda i, j: (j, i, 0)),
        out_shape=jax.ShapeDtypeStruct((n // LANE, s, LANE), F32),
        scratch_shapes=[pltpu.VMEM((tm + 2 * HALO, k), BF16)],
        compiler_params=_params("parallel", "arbitrary"),
        name="hy_in_proj_conv",
    )(x, x, x, g.reshape(1, k), w, conv_w, conv_b.reshape(1, n))


def _filter_mlp_kernel(z_ref, w1_ref, b1_ref, f_ref, w2_ref, b2_ref, w3_ref, b3_ref, f3_ref, o_ref):
    hp = lax.Precision.HIGHEST
    f = f_ref[...]
    h = jnp.sin(f * (jnp.dot(z_ref[...], w1_ref[...], precision=hp, preferred_element_type=F32) + b1_ref[...]))
    h = jnp.sin(f * (jnp.dot(h, w2_ref[...], precision=hp, preferred_element_type=F32) + b2_ref[...]))
    o_ref[...] = jnp.sin(f3_ref[...] * (jnp.dot(h, w3_ref[...], precision=hp, preferred_element_type=F32) + b3_ref[...]))


def filter_mlp(zfeat, f_w1, f_b1, f_freq, f_w2, f_b2, f_w3, f_b3, tm=2048):
    n, e = zfeat.shape
    wd = f_w2.shape[0]
    row = lambda i: (i, 0)
    fixed = lambda i: (0, 0)
    vec = lambda v: v.reshape(1, wd)
    lane_pad = lambda a: jnp.pad(a, ((0, 0), (0, LANE - wd)))
    wide = pl.BlockSpec((1, LANE), fixed)
    return pl.pallas_call(
        _filter_mlp_kernel,
        grid=(_steps(n, tm),),
        in_specs=[pl.BlockSpec((tm, e), row), pl.BlockSpec((e, wd), fixed), pl.BlockSpec((1, wd), fixed),
                  pl.BlockSpec((1, wd), fixed), pl.BlockSpec((wd, wd), fixed), pl.BlockSpec((1, wd), fixed),
                  pl.BlockSpec((wd, LANE), fixed), wide, wide],
        out_specs=pl.BlockSpec((tm, LANE), row),
        out_shape=jax.ShapeDtypeStruct((n, LANE), F32),
        compiler_params=_params("parallel"),
        name="hy_filter_mlp",
    )(zfeat, f_w1, vec(f_b1), vec(f_freq), f_w2, vec(f_b2), lane_pad(f_w3), lane_pad(vec(f_b3)), lane_pad(vec(f_freq)))


def _rows_by_group(ref):
    t = [jnp.swapaxes(ref[i], 0, 1) for i in range(ref.shape[0])]
    return [jnp.concatenate([ti[r] for ti in t], axis=1) for r in range(FFT_G)]


def _store_by_group(ref, vals):
    for i in range(ref.shape[0]):
        ref[i] = jnp.swapaxes(jnp.stack([v[:, i * LANE:(i + 1) * LANE] for v in vals], axis=0), 0, 1)


def _kern_stage1_kernel(h_ref, w00_ref, w01_ref, w10_ref, w11_ref, dl_ref, w1_ref, kb_ref, nrm_ref, *, seq):
    h_rows = _rows_by_group(h_ref)
    g = pl.program_id(1)
    half = FFT_R // 2
    w_fwd = (w00_ref[...], w01_ref[...])
    w_bwd = (w10_ref[...], w11_ref[...])
    rate = dl_ref[...] * (-1.0 / (seq - 1))
    n1 = lax.broadcasted_iota(jnp.int32, (half, 1), 0)
    packed = [[] for _ in range(HY_ORDER)]
    acc = [jnp.zeros(nrm_ref.shape[1:], F32) for _ in range(HY_ORDER)]
    for r in range(FFT_G):
        n2 = g * FFT_G + r
        hs = h_rows[r].astype(BF16)
        lag_f = FFT_R * n1 + n2
        lag_b = seq - FFT_R * n1 - n2
        decay_f = jnp.exp(lag_f.astype(F32) * rate)
        decay_b = jnp.where(lag_b == seq, 0.0, jnp.exp(lag_b.astype(F32) * rate))
        for o in range(HY_ORDER):
            kf = jnp.dot(hs[:half], w_fwd[o], preferred_element_type=F32) * decay_f
            kb = jnp.dot(hs[half:], w_bwd[o], preferred_element_type=F32) * decay_b
            slab = jnp.concatenate([kf, kb], axis=0)
            acc[o] += jnp.sum(jnp.abs(slab), axis=0, keepdims=True)
            packed[o].append(_pack_pair(jnp.dot(w1_ref[r], slab.astype(BF16), preferred_element_type=F32)))
    for o in range(HY_ORDER):
        _store_by_group(kb_ref.at[o], packed[o])

    @pl.when(g == 0)
    def _():
        for o in range(HY_ORDER):
            nrm_ref[o] = acc[o]

    @pl.when(g > 0)
    def _():
        for o in range(HY_ORDER):
            nrm_ref[o] += acc[o]


def kern_stage1(h3, w4, deltas, w1_full, seq, tc=HY_TC):
    d = deltas.shape[0]
    jn = _steps(d, tc)
    h3v = h3.reshape(1, FFT_R, FFT_R, LANE)
    assert HY_ORDER == 2
    kern = functools.partial(_kern_stage1_kernel, seq=seq)
    w4_spec = lambda direction, order: pl.BlockSpec((LANE, tc), lambda j, g: (0, (direction * HY_ORDER + order) * jn + j))
    return pl.pallas_call(
        kern,
        grid=(jn, FFT_R // FFT_G),
        in_specs=[
            pl.BlockSpec((1, FFT_R, FFT_G, LANE), lambda j, g: (0, 0, g, 0)),
            w4_spec(0, 0), w4_spec(0, 1), w4_spec(1, 0), w4_spec(1, 1),
            pl.BlockSpec((1, tc), lambda j, g: (0, j)),
            pl.BlockSpec((FFT_G, 2 * FFT_K1, FFT_R), lambda j, g: (g, 0, 0)),
        ],
        out_specs=[
            pl.BlockSpec((HY_ORDER, HY_NQ, FFT_K1, FFT_G, LANE), lambda j, g: (0, j, 0, g, 0)),
            pl.BlockSpec((HY_ORDER, 1, tc), lambda j, g: (0, 0, j)),
        ],
        out_shape=[
            jax.ShapeDtypeStruct((HY_ORDER, d // LANE, FFT_K1, FFT_R, LANE), jnp.uint32),
            jax.ShapeDtypeStruct((HY_ORDER, 1, d), F32),
        ],
        compiler_params=_params("parallel", "arbitrary"),
        name="hy_kern_stage1",
    )(h3v, w4, w4, w4, w4, deltas.reshape(1, d), w1_full)


def _join_lanes(ref, idx):
    return jnp.concatenate([ref[(q,) + idx] for q in range(ref.shape[0])], axis=1)


def _kern_stage3_kernel(kb_ref, g_ref, o_ref):
    gm = g_ref[...]
    for r in range(FFT_G):
        b = _unpack_pair(_join_lanes(kb_ref.at[0], (r,)))
        o_ref[0, r] = jnp.dot(gm, b, preferred_element_type=F32).astype(o_ref.dtype)


def kern_stage3(kb, g_fwd, tc=HY_TC):
    d = kb.shape[1] * LANE
    return pl.pallas_call(
        _kern_stage3_kernel,
        grid=(HY_ORDER, FFT_K1 // FFT_G, _steps(d, tc)),
        in_specs=[
            pl.BlockSpec((1, HY_NQ, FFT_G, FFT_R, LANE), lambda o, kg, j: (o, j, kg, 0, 0)),
            pl.BlockSpec((2 * FFT_R, 2 * FFT_R), lambda o, kg, j: (0, 0)),
        ],
        out_specs=pl.BlockSpec((1, FFT_G, 2 * FFT_R, tc), lambda o, kg, j: (o, kg, 0, j)),
        out_shape=jax.ShapeDtypeStruct((HY_ORDER, FFT_K1, 2 * FFT_R, d), BF16),
        compiler_params=_params("parallel", "parallel", "parallel"),
        name="hy_kern_stage3",
    )(kb, g_fwd)


def _sig_stage1_kernel(z_ref, w1_ref, o_ref):
    z_rows = _rows_by_group(z_ref)
    _store_by_group(o_ref, [_pack_pair(jnp.dot(w1_ref[r], z_rows[r].astype(BF16), preferred_element_type=F32))
                            for r in range(FFT_G)])


def sig_stage1(z_src, slab0, d, w1_half, tc=HY_TC):
    s = z_src.shape[1]
    rows = s // FFT_R
    zv = z_src.reshape(z_src.shape[0], rows, FFT_R, LANE)
    return pl.pallas_call(
        _sig_stage1_kernel,
        grid=(FFT_R // FFT_G, _steps(d, tc)),
        in_specs=[
            pl.BlockSpec((HY_NQ, rows, FFT_G, LANE), lambda g, j: (slab0 // HY_NQ + j, 0, g, 0)),
            pl.BlockSpec((FFT_G, 2 * FFT_K1, rows), lambda g, j: (g, 0, 0)),
        ],
        out_specs=pl.BlockSpec((HY_NQ, FFT_K1, FFT_G, LANE), lambda g, j: (j, 0, g, 0)),
        out_shape=jax.ShapeDtypeStruct((d // LANE, FFT_K1, FFT_R, LANE), jnp.uint32),
        compiler_params=_params("parallel", "parallel"),
        name="hy_sig_stage1",
    )(zv, w1_half)


def _sig_stage3_kernel(zb_ref, ks_ref, gf_ref, gi_ref, o_ref):
    gf = gf_ref[...]
    gi = gi_ref[...]
    for r in range(FFT_G):
        x = jnp.dot(gf, _unpack_pair(_join_lanes(zb_ref, (r,))), preferred_element_type=F32)
        ks = ks_ref[0, r].astype(F32)
        xr, xi = x[:FFT_R], x[FFT_R:]
        kr, ki = ks[:FFT_R], ks[FFT_R:]
        y = jnp.concatenate([xr * kr - xi * ki, xr * ki + xi * kr], axis=0).astype(BF16)
        packed = _pack_pair(jnp.dot(gi, y, preferred_element_type=F32))
        for q in range(o_ref.shape[0]):
            o_ref[q, r] = packed[:, q * LANE:(q + 1) * LANE]


def sig_stage3(zb, kspec, order, g_fwd, g_inv, tc=HY_TC):
    d = zb.shape[0] * LANE
    blk = pl.BlockSpec((HY_NQ, FFT_G, FFT_R, LANE), lambda kg, j: (j, kg, 0, 0))
    mat = pl.BlockSpec((2 * FFT_R, 2 * FFT_R), lambda kg, j: (0, 0))
    return pl.pallas_call(
        _sig_stage3_kernel,
        grid=(FFT_K1 // FFT_G, _steps(d, tc)),
        in_specs=[blk, pl.BlockSpec((1, FFT_G, 2 * FFT_R, tc), lambda kg, j: (order, kg, 0, j)), mat, mat],
        out_specs=blk,
        out_shape=jax.ShapeDtypeStruct(zb.shape, jnp.uint32),
        compiler_params=_params("parallel", "parallel"),
        name="hy_sig_stage3",
    )(zb, kspec, g_fwd, g_inv)


def _sig_inverse1_kernel(yb_ref, v_ref, gate_ref, z_ref, nrm_ref, bias_ref, *rest, next_stage1):
    inv = 1.0 / nrm_ref[0]
    bias = bias_ref[0]
    yb_rows, gate_rows, z_rows = _rows_by_group(yb_ref), _rows_by_group(gate_ref), _rows_by_group(z_ref)
    out = []
    for r in range(FFT_G):
        conv = jnp.dot(v_ref[r], _unpack_pair(yb_rows[r]), preferred_element_type=F32) * inv
        out.append(gate_rows[r] * (conv + z_rows[r] * bias))
    if next_stage1:
        w1_ref, o_ref, zb_ref = rest
        _store_by_group(zb_ref, [_pack_pair(jnp.dot(w1_ref[r], out[r].astype(BF16), preferred_element_type=F32))
                                 for r in range(FFT_G)])
    else:
        o_ref, = rest
    _store_by_group(o_ref, out)


def sig_inverse1(yb, v1_half, gate_src, gate_slab0, z_src, z_slab0, nrm, bias, order, w1_half=None, tc=HY_TC):
    d = yb.shape[0] * LANE
    s = gate_src.shape[1]
    rows = s // FFT_R
    view = lambda a: a.reshape(a.shape[0], rows, FFT_R, LANE)
    sig = (HY_NQ, rows, FFT_G, LANE)
    freq = (HY_NQ, FFT_K1, FFT_G, LANE)
    in_specs = [
        pl.BlockSpec(freq, lambda g, j: (j, 0, g, 0)),
        pl.BlockSpec((FFT_G, rows, 2 * FFT_K1), lambda g, j: (g, 0, 0)),
        pl.BlockSpec(sig, lambda g, j: (gate_slab0 // HY_NQ + j, 0, g, 0)),
        pl.BlockSpec(sig, lambda g, j: (z_slab0 // HY_NQ + j, 0, g, 0)),
        pl.BlockSpec((1, 1, tc), lambda g, j: (order, 0, j)),
        pl.BlockSpec((1, 1, tc), lambda g, j: (order, 0, j)),
    ]
    args = [yb, v1_half, view(gate_src), view(z_src), nrm, bias.reshape(HY_ORDER, 1, d)]
    out_specs = [pl.BlockSpec(sig, lambda g, j: (j, 0, g, 0))]
    out_shape = [jax.ShapeDtypeStruct((d // LANE, rows, FFT_R, LANE), F32)]
    if w1_half is not None:
        in_specs.append(pl.BlockSpec((FFT_G, 2 * FFT_K1, rows), lambda g, j: (g, 0, 0)))
        args.append(w1_half)
        out_specs.append(pl.BlockSpec(freq, lambda g, j: (j, 0, g, 0)))
        out_shape.append(jax.ShapeDtypeStruct((d // LANE, FFT_K1, FFT_R, LANE), jnp.uint32))
    outs = pl.pallas_call(
        functools.partial(_sig_inverse1_kernel, next_stage1=w1_half is not None),
        grid=(FFT_R // FFT_G, _steps(d, tc)),
        in_specs=in_specs,
        out_specs=out_specs,
        out_shape=out_shape,
        compiler_params=_params("parallel", "parallel"),
        name="hy_sig_inverse1",
    )(*args)
    z_new = outs[0].reshape(d // LANE, s, LANE)
    return (z_new, outs[1]) if w1_half is not None else (z_new, None)


def _hyena_positions(seq):
    t = jnp.linspace(0.0, 1.0, seq, dtype=F32)[:, None]
    bands = (HY_EMB_DIM - 1) // 2
    w = 2.0 * math.pi * jnp.arange(seq, dtype=F32)[:, None] / seq
    f = jnp.linspace(1e-4, bands - 1, bands, dtype=F32)[None, :]
    z = jnp.concatenate([t, jnp.cos(f * w), -jnp.sin(f * w)], axis=-1)
    return jnp.pad(z, ((0, 0), (0, LANE - HY_EMB_DIM)))


def hyena_mixer(u, f_w1, f_b1, f_freq, f_w2, f_b2, f_w3, f_b3, f_w4, bias):
    s = u.shape[1]
    d = u.shape[0] * LANE // 3
    assert 2 * s == FFT_N and d % HY_TC == 0
    c = _dft_constants()
    h3 = filter_mlp(_hyena_positions(s), jnp.pad(f_w1, ((0, LANE - HY_EMB_DIM), (0, 0))), f_b1, f_freq, f_w2, f_b2, f_w3, f_b3)
    h3 = jnp.concatenate([h3, h3[:1], h3[:0:-1]], axis=0)
    w4 = jnp.pad(f_w4.astype(BF16), ((0, LANE - f_w4.shape[0]), (0, 0)))
    deltas = jnp.abs(jnp.linspace(HY_MIN_DECAY, HY_MAX_DECAY, d, dtype=F32))
    kb, nrm = kern_stage1(h3, w4, deltas, c["w1_full"], s)
    kspec = kern_stage3(kb, c["g_fwd"])
    nslab = d // LANE
    z_src, z_slab0 = u, 2 * nslab
    zb = sig_stage1(z_src, z_slab0, d, c["w1_half"])
    for o in range(HY_ORDER):
        yb = sig_stage3(zb, kspec, o, c["g_fwd"], c["g_inv"])
        w1_next = c["w1_half"] if o + 1 < HY_ORDER else None
        z_src, zb = sig_inverse1(yb, c["v1_half"], u, o * nslab, z_src, z_slab0, nrm, bias, o, w1_next)
        z_slab0 = 0
    return z_src


def hyena_layer(x, mix_norm, w_in, conv_w, conv_b, f_w1, f_b1, f_freq, f_w2, f_b2, f_w3, f_b3, f_w4, bias, w_out):
    u = in_proj_conv(x, mix_norm, w_in.astype(BF16), conv_w, conv_b)
    z = hyena_mixer(u, f_w1, f_b1, f_freq, f_w2, f_b2, f_w3, f_b3, f_w4, bias)
    return matmul_residual(z, w_out.astype(BF16), x, a_in_slabs=True)


def _rope_tables(seq, dim):
    inv = 1.0 / (ROPE_THETA ** (jnp.arange(0, dim, 2, dtype=F32) / dim))
    ang = jnp.arange(seq, dtype=F32)[:, None] * inv[None, :]
    return jnp.cos(ang), jnp.sin(ang)


def kernel(x, l0_mix_norm, l0_mla_w_in, l0_mla_q_norm, l0_mla_w_uq, l0_mla_kv_norm, l0_mla_w_ukv, l0_mla_w_o, l0_ffn_norm, l0_ffn_w_gate, l0_ffn_w_up, l0_ffn_w_down, l1_mix_norm, l1_hy_w_in, l1_hy_conv_w, l1_hy_conv_b, l1_hy_f_w1, l1_hy_f_b1, l1_hy_f_freq, l1_hy_f_w2, l1_hy_f_b2, l1_hy_f_w3, l1_hy_f_b3, l1_hy_f_w4, l1_hy_bias, l1_hy_w_out, l1_ffn_norm, l1_ffn_w_gate, l1_ffn_w_up, l1_ffn_w_down, l2_mix_norm, l2_da_w_qkv, l2_da_lq1, l2_da_lk1, l2_da_lq2, l2_da_lk2, l2_da_subln, l2_da_w_o, l2_ffn_norm, l2_ffn_w_gate, l2_ffn_w_up, l2_ffn_w_down, l3_mix_norm, l3_mla_w_in, l3_mla_q_norm, l3_mla_w_uq, l3_mla_kv_norm, l3_mla_w_ukv, l3_mla_w_o, l3_ffn_norm, l3_ffn_w_gate, l3_ffn_w_up, l3_ffn_w_down, final_norm):
    b, s, d = x.shape
    assert b == 1
    h = x.reshape(s, d)

    mc, ms = _rope_tables(s, MLA_ROPE)
    mla_tab = jnp.concatenate([mc, mc, ms, ms], axis=-1)
    mla = (mla_tab, mc.T, ms.T)
    dc, ds = _rope_tables(s, DIFF_HEAD_DIM)
    diff_rope = (jnp.concatenate([dc, dc], axis=-1), jnp.concatenate([-ds, ds], axis=-1))
    diff_rope_t = (dc.T, ds.T)

    def ffn(h, g, wg, wu, wd, out_norm_gain=None):
        return ffn_block(h, g, wg.astype(BF16), wu.astype(BF16), wd.astype(BF16), out_norm_gain)

    h = mla_layer(h, l0_mix_norm, l0_mla_w_in, l0_mla_q_norm, l0_mla_w_uq, l0_mla_kv_norm, l0_mla_w_ukv, l0_mla_w_o, *mla)
    h = ffn(h, l0_ffn_norm, l0_ffn_w_gate, l0_ffn_w_up, l0_ffn_w_down)
    h = hyena_layer(h, l1_mix_norm, l1_hy_w_in, l1_hy_conv_w, l1_hy_conv_b, l1_hy_f_w1, l1_hy_f_b1, l1_hy_f_freq, l1_hy_f_w2, l1_hy_f_b2, l1_hy_f_w3, l1_hy_f_b3, l1_hy_f_w4, l1_hy_bias, l1_hy_w_out)
    h = ffn(h, l1_ffn_norm, l1_ffn_w_gate, l1_ffn_w_up, l1_ffn_w_down)
    h = diff_layer(h, l2_mix_norm, l2_da_w_qkv, l2_da_lq1, l2_da_lk1, l2_da_lq2, l2_da_lk2, l2_da_subln, l2_da_w_o, diff_rope, diff_rope_t, 2)
    h = ffn(h, l2_ffn_norm, l2_ffn_w_gate, l2_ffn_w_up, l2_ffn_w_down)
    h = mla_layer(h, l3_mix_norm, l3_mla_w_in, l3_mla_q_norm, l3_mla_w_uq, l3_mla_kv_norm, l3_mla_w_ukv, l3_mla_w_o, *mla)
    h = ffn(h, l3_ffn_norm, l3_ffn_w_gate, l3_ffn_w_up, l3_ffn_w_down, final_norm)
    return h.reshape(b, s, d)
```

```python
import functools
import math

import jax
import jax.numpy as jnp
import numpy as np
from jax import lax
from jax.experimental import pallas as pl
from jax.experimental.pallas import tpu as pltpu

F32 = jnp.float32
BF16 = jnp.bfloat16

D_MODEL = 2048
RMS_EPS = 1e-6
ROPE_THETA = 10000.0
LOG2E = math.log2(math.e)

MLA_HEADS = 16
MLA_Q_LORA = 768
MLA_KV_LORA = 512
MLA_NOPE = 128
MLA_ROPE = 64
MLA_V = 128
MLA_QK = MLA_NOPE + MLA_ROPE
MLA_SCALE = MLA_QK ** -0.5
MLA_QK_PAD = 256

HY_ORDER = 2
HY_EMB_DIM = 33
HY_TARGET = 1e-2
HY_MAX_DECAY = math.log(HY_TARGET) / 0.3
HY_MIN_DECAY = math.log(HY_TARGET) / 1.5

DIFF_HEAD_DIM = 128
DIFF_HEADS = D_MODEL // (2 * DIFF_HEAD_DIM)
DIFF_QK = DIFF_HEADS * 2 * DIFF_HEAD_DIM
DIFF_SCALE = DIFF_HEAD_DIM ** -0.5
DIFF_EPS = 1e-5

LANE = 128
VMEM_BYTES = 64 * 1024 * 1024
VMEM_LIMIT = VMEM_BYTES * 7 // 8
NEG_BIG = -1e30

PROJ_TM, PROJ_TN = 1024, 1024
MLA_IN_TM = 1024
MLA_HEADS_PER_STEP = 8
FFN_TM, FFN_TF = 1024, 512
MLA_TQ, DIFF_TQ, ATTN_TK = 4096, 2048, 512
HY_IN_TN = 1024


def _params(*sem):
    return pltpu.CompilerParams(dimension_semantics=sem, vmem_limit_bytes=VMEM_LIMIT)


def _steps(dim, tile):
    assert dim % tile == 0, f"tile {tile} does not divide {dim}"
    return dim // tile


def _rms(x, g, eps):
    return x * lax.rsqrt(jnp.mean(x * x, axis=-1, keepdims=True) + eps) * g


def _dot_nt(a, b):
    return lax.dot_general(a, b, (((1,), (1,)), ((), ())), preferred_element_type=F32)


def _rope_rows(y, c, s):
    h = c.shape[0]
    y1, y2 = y[:h], y[h:]
    return y1 * c - y2 * s, y2 * c + y1 * s


def _norm_matmul_kernel(x_ref, g_ref, w_ref, *rest, rope, scale):
    o_ref, xn_ref = rest[-2:]

    @pl.when(pl.program_id(1) == 0)
    def _():
        xn_ref[...] = _rms(x_ref[...], g_ref[...], RMS_EPS).astype(BF16)

    y = jnp.dot(xn_ref[...], w_ref[...], preferred_element_type=F32)
    if rope:
        c = rest[0][...] * scale
        s = rest[1][...] * scale
        hd = c.shape[1]
        for g in range(y.shape[1] // hd):
            t = y[:, g * hd:(g + 1) * hd]
            o_ref[:, g * hd:(g + 1) * hd] = (t * c + pltpu.roll(t, hd // 2, axis=1) * s).astype(o_ref.dtype)
    else:
        o_ref[...] = (y * scale).astype(o_ref.dtype)


def norm_matmul(x, g, w, out_dtype, rope=None, scale=1.0, tm=PROJ_TM, tn=PROJ_TN):
    m, k = x.shape
    n = w.shape[1]
    in_specs = [
        pl.BlockSpec((tm, k), lambda i, j: (i, 0)),
        pl.BlockSpec((1, k), lambda i, j: (0, 0)),
        pl.BlockSpec((k, tn), lambda i, j: (0, j)),
    ]
    args = [x, g.reshape(1, k), w]
    if rope is not None:
        in_specs += [pl.BlockSpec((tm, rope[0].shape[1]), lambda i, j: (i, 0))] * 2
        args += list(rope)
    return pl.pallas_call(
        functools.partial(_norm_matmul_kernel, rope=rope is not None, scale=scale),
        grid=(_steps(m, tm), _steps(n, tn)),
        in_specs=in_specs,
        out_specs=pl.BlockSpec((tm, tn), lambda i, j: (i, j)),
        out_shape=jax.ShapeDtypeStruct((m, n), out_dtype),
        scratch_shapes=[pltpu.VMEM((tm, k), BF16)],
        compiler_params=_params("parallel", "arbitrary"),
        name="norm_matmul",
    )(*args)


def _norm_matmul_t_kernel(x_ref, g_ref, wt_ref, *rest, rope, scale):
    o_ref, xn_ref = rest[-2:]

    @pl.when(pl.program_id(1) == 0)
    def _():
        xn_ref[...] = _rms(x_ref[...], g_ref[...], RMS_EPS).astype(BF16)

    y = _dot_nt(wt_ref[...], xn_ref[...]) * scale
    if rope:
        c = rest[0][...]
        s = rest[1][...]
        half = c.shape[0]
        for g in range(y.shape[0] // (2 * half)):
            r1, r2 = _rope_rows(y[2 * g * half:2 * (g + 1) * half], c, s)
            o_ref[2 * g * half:(2 * g + 1) * half, :] = r1.astype(o_ref.dtype)
            o_ref[(2 * g + 1) * half:2 * (g + 1) * half, :] = r2.astype(o_ref.dtype)
    else:
        o_ref[...] = y.astype(o_ref.dtype)


def norm_matmul_t(x, g, wt, out_dtype, rope_t=None, scale=1.0, tm=PROJ_TM, tn=PROJ_TN):
    m, k = x.shape
    n = wt.shape[0]
    in_specs = [
        pl.BlockSpec((tm, k), lambda i, j: (i, 0)),
        pl.BlockSpec((1, k), lambda i, j: (0, 0)),
        pl.BlockSpec((tn, k), lambda i, j: (j, 0)),
    ]
    args = [x, g.reshape(1, k), wt]
    if rope_t is not None:
        in_specs += [pl.BlockSpec((rope_t[0].shape[0], tm), lambda i, j: (0, i))] * 2
        args += list(rope_t)
    return pl.pallas_call(
        functools.partial(_norm_matmul_t_kernel, rope=rope_t is not None, scale=scale),
        grid=(_steps(m, tm), _steps(n, tn)),
        in_specs=in_specs,
        out_specs=pl.BlockSpec((tn, tm), lambda i, j: (j, i)),
        out_shape=jax.ShapeDtypeStruct((n, m), out_dtype),
        scratch_shapes=[pltpu.VMEM((tm, k), BF16)],
        compiler_params=_params("parallel", "arbitrary"),
        name="norm_matmul_t",
    )(*args)


def _matmul_res_kernel(a_ref, w_ref, r_ref, o_ref):
    if len(a_ref.shape) == 3:
        a = jnp.concatenate([a_ref[q] for q in range(a_ref.shape[0])], axis=1)
    else:
        a = a_ref[...]
    o_ref[...] = r_ref[...] + jnp.dot(a.astype(BF16), w_ref[...], preferred_element_type=F32)


def matmul_residual(a, w, res, a_in_slabs=False, tm=PROJ_TM, tn=PROJ_TN):
    k, n = w.shape
    m = res.shape[0]
    if a_in_slabs:
        a_spec = pl.BlockSpec((k // LANE, tm, LANE), lambda i, j: (0, i, 0))
    else:
        a_spec = pl.BlockSpec((tm, k), lambda i, j: (i, 0))
    return pl.pallas_call(
        _matmul_res_kernel,
        grid=(_steps(m, tm), _steps(n, tn)),
        in_specs=[
            a_spec,
            pl.BlockSpec((k, tn), lambda i, j: (0, j)),
            pl.BlockSpec((tm, tn), lambda i, j: (i, j)),
        ],
        out_specs=pl.BlockSpec((tm, tn), lambda i, j: (i, j)),
        out_shape=jax.ShapeDtypeStruct((m, n), F32),
        compiler_params=_params("parallel", "parallel"),
        name="matmul_residual",
    )(a, w, res)


def _ffn_kernel(x_ref, g_ref, wg_ref, wu_ref, wd_ref, *rest, out_norm):
    o_ref, xn_ref = rest[-2:]

    @pl.when(pl.program_id(1) == 0)
    def _():
        x = x_ref[...]
        xn_ref[...] = _rms(x, g_ref[...], RMS_EPS).astype(BF16)
        o_ref[...] = x

    xn = xn_ref[...]
    a = jnp.dot(xn, wg_ref[...], preferred_element_type=F32)
    b = jnp.dot(xn, wu_ref[...], preferred_element_type=F32)
    h = (a * jax.nn.sigmoid(a) * b).astype(BF16)
    o_ref[...] += jnp.dot(h, wd_ref[...], preferred_element_type=F32)

    if out_norm:
        @pl.when(pl.program_id(1) == pl.num_programs(1) - 1)
        def _():
            o_ref[...] = _rms(o_ref[...], rest[0][...], RMS_EPS)


def ffn_block(x, g, wg, wu, wd, out_norm_gain=None, tm=FFN_TM, tf=FFN_TF):
    m, d = x.shape
    f = wg.shape[1]
    vec = pl.BlockSpec((1, d), lambda i, j: (0, 0))
    in_specs = [
        pl.BlockSpec((tm, d), lambda i, j: (i, 0)),
        vec,
        pl.BlockSpec((d, tf), lambda i, j: (0, j)),
        pl.BlockSpec((d, tf), lambda i, j: (0, j)),
        pl.BlockSpec((tf, d), lambda i, j: (j, 0)),
    ]
    args = [x, g.reshape(1, d), wg, wu, wd]
    if out_norm_gain is not None:
        in_specs.append(vec)
        args.append(out_norm_gain.reshape(1, d))
    return pl.pallas_call(
        functools.partial(_ffn_kernel, out_norm=out_norm_gain is not None),
        grid=(_steps(m, tm), _steps(f, tf)),
        in_specs=in_specs,
        out_specs=pl.BlockSpec((tm, d), lambda i, j: (i, 0)),
        out_shape=jax.ShapeDtypeStruct((m, d), F32),
        scratch_shapes=[pltpu.VMEM((tm, d), BF16)],
        compiler_params=_params("parallel", "arbitrary"),
        name="ffn_block",
    )(*args)


def _flash_t(qs, k_of_part, k_ref, vt_ref, sa_ref, sb_ref, acc_ref, *, tk):
    parts = len(qs)
    tp = qs[0].shape[1]
    nk = k_ref.shape[0] // tk
    assert nk % 2 == 0 and nk >= 4

    def chunk(c):
        return pl.ds(c * tk if isinstance(c, int) else pl.multiple_of(c * tk, tk), tk)

    def scores(c, s_ref):
        k = k_ref[chunk(c), :]
        mc = ()
        for i in range(parts):
            st = jnp.dot(k_of_part(k, i), qs[i], preferred_element_type=F32)
            s_ref[i] = st
            mc += (jnp.max(st, axis=0, keepdims=True),)
        return mc

    def softmax_pv(c, s_ref, mc, state):
        vt = vt_ref[:, chunk(c)]
        new = ()
        for i in range(parts):
            m, l = state[2 * i:2 * i + 2]
            m_new = jnp.maximum(m, mc[i])
            alpha = jnp.exp2(m - m_new)
            p = jnp.exp2(s_ref[i] - m_new)
            l = alpha * l + jnp.sum(p, axis=0, keepdims=True)
            acc_ref[i] = alpha * acc_ref[i] + jnp.dot(vt, p.astype(BF16), preferred_element_type=F32)
            new += (m_new, l)
        return new

    def body(c2, carry):
        mc_a, state = carry[:parts], carry[parts:]
        c = 2 * c2
        mc_b = scores(c + 1, sb_ref)
        state = softmax_pv(c, sa_ref, mc_a, state)
        mc_a = scores(c + 2, sa_ref)
        state = softmax_pv(c + 1, sb_ref, mc_b, state)
        return mc_a + state

    acc_ref[...] = jnp.zeros(acc_ref.shape, F32)
    state0 = (jnp.full((1, tp), NEG_BIG, F32), jnp.zeros((1, tp), F32)) * parts
    carry = lax.fori_loop(0, nk // 2 - 1, body, scores(0, sa_ref) + state0)
    mc_a, state = carry[:parts], carry[parts:]
    mc_b = scores(nk - 1, sb_ref)
    state = softmax_pv(nk - 2, sa_ref, mc_a, state)
    state = softmax_pv(nk - 1, sb_ref, mc_b, state)
    return state[1::2]


MLA_KR_PAD = MLA_QK_PAD - MLA_NOPE


def _rope_pair(t, tab):
    t = t * tab
    return t + pltpu.roll(t, MLA_ROPE, axis=1)


def _mla_in_kernel(x_ref, g_ref, w_ref, qn_ref, kvn_ref, tab_ref, cq_ref, ckv_ref, kr_ref):
    xn = _rms(x_ref[...], g_ref[...], RMS_EPS).astype(BF16)
    h = jnp.dot(xn, w_ref[...], preferred_element_type=F32)
    cq_ref[...] = _rms(h[:, :MLA_Q_LORA], qn_ref[...], RMS_EPS).astype(BF16)
    ckv_ref[...] = _rms(h[:, MLA_Q_LORA:MLA_Q_LORA + MLA_KV_LORA], kvn_ref[...], RMS_EPS).astype(BF16)
    kr = _rope_pair(h[:, MLA_Q_LORA + MLA_KV_LORA:], tab_ref[...])
    lane = lax.broadcasted_iota(jnp.int32, kr.shape, 1)
    kr_ref[...] = jnp.where(lane < MLA_ROPE, kr, 0.0).astype(BF16)


def mla_in(x, g, w_in_ext, q_norm, kv_norm, tab, tm=MLA_IN_TM):
    m, d = x.shape
    n = w_in_ext.shape[1]
    row = lambda i: (i, 0)
    fixed = lambda i: (0, 0)
    return pl.pallas_call(
        _mla_in_kernel,
        grid=(_steps(m, tm),),
        in_specs=[
            pl.BlockSpec((tm, d), row),
            pl.BlockSpec((1, d), fixed),
            pl.BlockSpec((d, n), fixed),
            pl.BlockSpec((1, MLA_Q_LORA), fixed),
            pl.BlockSpec((1, MLA_KV_LORA), fixed),
            pl.BlockSpec((tm, MLA_KR_PAD), row),
        ],
        out_specs=[
            pl.BlockSpec((tm, MLA_Q_LORA), row),
            pl.BlockSpec((tm, MLA_KV_LORA), row),
            pl.BlockSpec((tm, MLA_KR_PAD), row),
        ],
        out_shape=[
            jax.ShapeDtypeStruct((m, MLA_Q_LORA), BF16),
            jax.ShapeDtypeStruct((m, MLA_KV_LORA), BF16),
            jax.ShapeDtypeStruct((m, MLA_KR_PAD), BF16),
        ],
        compiler_params=_params("parallel"),
        name="mla_in",
    )(x, g.reshape(1, d), w_in_ext, q_norm.reshape(1, -1), kv_norm.reshape(1, -1), tab)


def _mla_qt_kernel(c_ref, wt_ref, cos_ref, sin_ref, q_ref, *, heads):
    y = _dot_nt(wt_ref[...], c_ref[...]) * (MLA_SCALE * LOG2E)
    c = cos_ref[...]
    s = sin_ref[...]
    half = MLA_ROPE // 2
    zeros = jnp.zeros((MLA_QK_PAD - MLA_QK, y.shape[1]), BF16)
    for h in range(heads):
        src = h * MLA_QK
        dst = h * MLA_QK_PAD
        q_ref[dst:dst + MLA_NOPE, :] = y[src:src + MLA_NOPE].astype(BF16)
        r1, r2 = _rope_rows(y[src + MLA_NOPE:src + MLA_QK], c, s)
        q_ref[dst + MLA_NOPE:dst + MLA_NOPE + half, :] = r1.astype(BF16)
        q_ref[dst + MLA_NOPE + half:dst + MLA_QK, :] = r2.astype(BF16)
        q_ref[dst + MLA_QK:dst + MLA_QK_PAD, :] = zeros


def mla_qt(cq, w_uq_t, cos_t, sin_t, tm=PROJ_TM, heads_per_step=MLA_HEADS_PER_STEP):
    m, k = cq.shape
    tn_in = heads_per_step * MLA_QK
    tn_out = heads_per_step * MLA_QK_PAD
    return pl.pallas_call(
        functools.partial(_mla_qt_kernel, heads=heads_per_step),
        grid=(_steps(m, tm), _steps(MLA_HEADS, heads_per_step)),
        in_specs=[
            pl.BlockSpec((tm, k), lambda i, j: (i, 0)),
            pl.BlockSpec((tn_in, k), lambda i, j: (j, 0)),
            pl.BlockSpec((MLA_ROPE // 2, tm), lambda i, j: (0, i)),
            pl.BlockSpec((MLA_ROPE // 2, tm), lambda i, j: (0, i)),
        ],
        out_specs=pl.BlockSpec((tn_out, tm), lambda i, j: (j, i)),
        out_shape=jax.ShapeDtypeStruct((MLA_HEADS * MLA_QK_PAD, m), BF16),
        compiler_params=_params("parallel", "parallel"),
        name="mla_qt",
    )(cq, w_uq_t, cos_t, sin_t)


def _mla_kv_kernel(c_ref, wk_ref, wvt_ref, kr_ref, k_ref, vt_ref, *, heads):
    c = c_ref[...]
    yk = jnp.dot(c, wk_ref[...], preferred_element_type=F32)
    kr = kr_ref[...]
    for h in range(heads):
        dst = h * MLA_QK_PAD
        k_ref[:, dst:dst + MLA_NOPE] = yk[:, h * MLA_NOPE:(h + 1) * MLA_NOPE].astype(BF16)
        k_ref[:, dst + MLA_NOPE:dst + MLA_QK_PAD] = kr
    vt_ref[...] = _dot_nt(wvt_ref[...], c).astype(BF16)


def mla_kv(ckv, w_k, w_v_t, kr, tm=PROJ_TM, heads_per_step=MLA_HEADS_PER_STEP):
    m, k = ckv.shape
    hp = heads_per_step
    return pl.pallas_call(
        functools.partial(_mla_kv_kernel, heads=hp),
        grid=(_steps(m, tm), _steps(MLA_HEADS, hp)),
        in_specs=[
            pl.BlockSpec((tm, k), lambda i, j: (i, 0)),
            pl.BlockSpec((k, hp * MLA_NOPE), lambda i, j: (0, j)),
            pl.BlockSpec((hp * MLA_V, k), lambda i, j: (j, 0)),
            pl.BlockSpec((tm, MLA_KR_PAD), lambda i, j: (i, 0)),
        ],
        out_specs=[
            pl.BlockSpec((tm, hp * MLA_QK_PAD), lambda i, j: (i, j)),
            pl.BlockSpec((hp * MLA_V, tm), lambda i, j: (j, i)),
        ],
        out_shape=[
            jax.ShapeDtypeStruct((m, MLA_HEADS * MLA_QK_PAD), BF16),
            jax.ShapeDtypeStruct((MLA_HEADS * MLA_V, m), BF16),
        ],
        compiler_params=_params("parallel", "parallel"),
        name="mla_kv",
    )(ckv, w_k, w_v_t, kr)


def _mla_attn_kernel(qt_ref, k_ref, vt_ref, o_ref, sa_ref, sb_ref, acc_ref, *, tk):
    parts, _, tp = sa_ref.shape
    qs = [qt_ref[:, i * tp:(i + 1) * tp] for i in range(parts)]
    ls = _flash_t(qs, lambda k, i: k, k_ref, vt_ref, sa_ref, sb_ref, acc_ref, tk=tk)
    for i in range(parts):
        o_ref[i * tp:(i + 1) * tp, :] = (acc_ref[i] / ls[i]).T.astype(o_ref.dtype)


def mla_attention(qt, k, vt, tq=MLA_TQ, tk=ATTN_TK, parts=2):
    s = k.shape[0]
    return pl.pallas_call(
        functools.partial(_mla_attn_kernel, tk=tk),
        grid=(MLA_HEADS, _steps(s, tq)),
        in_specs=[
            pl.BlockSpec((MLA_QK_PAD, tq), lambda h, i: (h, i)),
            pl.BlockSpec((s, MLA_QK_PAD), lambda h, i: (0, h)),
            pl.BlockSpec((MLA_V, s), lambda h, i: (h, 0)),
        ],
        out_specs=pl.BlockSpec((tq, MLA_V), lambda h, i: (i, h)),
        out_shape=jax.ShapeDtypeStruct((s, MLA_HEADS * MLA_V), BF16),
        scratch_shapes=[
            pltpu.VMEM((parts, tk, tq // parts), F32),
            pltpu.VMEM((parts, tk, tq // parts), F32),
            pltpu.VMEM((parts, MLA_V, tq // parts), F32),
        ],
        compiler_params=_params("parallel", "parallel"),
        name="mla_attention",
    )(qt, k, vt)


def _rot_cols(w):
    half = w.shape[-1] // 2
    return jnp.concatenate([-w[..., half:], w[..., :half]], axis=-1)


def mla_layer(x, mix_norm, w_in, q_norm, w_uq, kv_norm, w_ukv, w_o, tab, cos_t, sin_t):
    w_kr = w_in[:, MLA_Q_LORA + MLA_KV_LORA:]
    w_in_ext = jnp.concatenate([w_in, _rot_cols(w_kr)], axis=1).astype(BF16)
    w_uq_t = w_uq.astype(BF16).T
    wkv = w_ukv.astype(BF16).reshape(MLA_KV_LORA, MLA_HEADS, MLA_NOPE + MLA_V)
    w_k = wkv[..., :MLA_NOPE].reshape(MLA_KV_LORA, -1)
    w_v_t = wkv[..., MLA_NOPE:].reshape(MLA_KV_LORA, -1).T
    cq, ckv, kr = mla_in(x, mix_norm, w_in_ext, q_norm, kv_norm, tab)
    qt = mla_qt(cq, w_uq_t, cos_t, sin_t)
    k, vt = mla_kv(ckv, w_k, w_v_t, kr)
    o = mla_attention(qt, k, vt)
    return matmul_residual(o, w_o.astype(BF16), x)


def _diff_attn_kernel(lam_ref, qt_ref, k_ref, vt_ref, g_ref, o_ref, sa_ref, sb_ref, acc_ref, *, tk, lam_init):
    d = DIFF_HEAD_DIM
    qs = [qt_ref[:d, :], qt_ref[d:, :]]
    l0, l1 = _flash_t(qs, lambda k, i: k[:, i * d:(i + 1) * d], k_ref, vt_ref, sa_ref, sb_ref, acc_ref, tk=tk)
    ot = acc_ref[0] / l0 - lam_ref[0, 0] * (acc_ref[1] / l1)
    o = _rms(ot.T, g_ref[...], DIFF_EPS) * (1.0 - lam_init)
    o_ref[...] = o.astype(o_ref.dtype)


def diff_attention(qt, k, vt, lam, subln, lam_init, tq=DIFF_TQ, tk=ATTN_TK):
    s = k.shape[0]
    hd = 2 * DIFF_HEAD_DIM
    return pl.pallas_call(
        functools.partial(_diff_attn_kernel, tk=tk, lam_init=lam_init),
        grid=(DIFF_HEADS, _steps(s, tq)),
        in_specs=[
            pl.BlockSpec(memory_space=pltpu.SMEM),
            pl.BlockSpec((hd, tq), lambda h, i: (h, i)),
            pl.BlockSpec((s, hd), lambda h, i: (0, h)),
            pl.BlockSpec((hd, s), lambda h, i: (h, 0)),
            pl.BlockSpec((1, hd), lambda h, i: (0, 0)),
        ],
        out_specs=pl.BlockSpec((tq, hd), lambda h, i: (i, h)),
        out_shape=jax.ShapeDtypeStruct((s, DIFF_HEADS * hd), BF16),
        scratch_shapes=[
            pltpu.VMEM((2, tk, tq), F32),
            pltpu.VMEM((2, tk, tq), F32),
            pltpu.VMEM((2, hd, tq), F32),
        ],
        compiler_params=_params("parallel", "parallel"),
        name="diff_attention",
    )(lam, qt, k, vt, subln.reshape(1, hd))


def _lambda_kernel(lq1_ref, lk1_ref, lq2_ref, lk2_ref, o_ref, *, lam_init):
    a = jnp.sum(lq1_ref[...] * lk1_ref[...], axis=-1, keepdims=True)
    b = jnp.sum(lq2_ref[...] * lk2_ref[...], axis=-1, keepdims=True)
    o_ref[...] = jnp.exp(a) - jnp.exp(b) + lam_init


def diff_lambda(lq1, lk1, lq2, lk2, lam_init):
    args = [v.reshape(1, -1) for v in (lq1, lk1, lq2, lk2)]
    return pl.pallas_call(
        functools.partial(_lambda_kernel, lam_init=lam_init),
        out_shape=jax.ShapeDtypeStruct((1, 1), F32),
        name="diff_lambda",
    )(*args)


def diff_layer(x, mix_norm, w_qkv, lq1, lk1, lq2, lk2, subln, w_o, rope, rope_t, layer_idx):
    lam_init = 0.8 - 0.6 * math.exp(-0.3 * layer_idx)
    w = w_qkv.astype(BF16)
    qt = norm_matmul_t(x, mix_norm, w[:, :DIFF_QK].T, BF16, rope_t=rope_t, scale=DIFF_SCALE * LOG2E, tn=DIFF_QK)
    k = norm_matmul(x, mix_norm, w[:, DIFF_QK:2 * DIFF_QK], BF16, rope=rope, tn=DIFF_QK)
    vt = norm_matmul_t(x, mix_norm, w[:, 2 * DIFF_QK:].T, BF16, tn=DIFF_QK)
    lam = diff_lambda(lq1, lk1, lq2, lk2, lam_init)
    o = diff_attention(qt, k, vt, lam, subln, lam_init)
    return matmul_residual(o, w_o.astype(BF16), x)


FFT_R = 128
FFT_N = FFT_R * FFT_R
FFT_G = 8
FFT_K1 = 72
HY_TC = 1024
HY_NQ = HY_TC // LANE


def _dft_constants():
    r = np.arange(FFT_R)
    n2, k1, n1 = r[:, None, None], np.arange(FFT_K1)[None, :, None], r[None, None, :]
    th = 2.0 * np.pi * ((k1 * (FFT_R * n1 + n2)) % FFT_N) / FFT_N
    w1 = np.concatenate([np.cos(th), -np.sin(th)], axis=1)
    tht = np.transpose(th, (0, 2, 1))
    weight = np.where(np.arange(FFT_K1) > FFT_R // 2, 0.0, np.where(np.arange(FFT_K1) % (FFT_R // 2) == 0, 1.0, 2.0))
    v1 = np.concatenate([np.cos(tht) * weight, -np.sin(tht) * weight], axis=2) / FFT_N
    ph = 2.0 * np.pi * ((r[:, None] * r[None, :]) % FFT_R) / FFT_R
    c, s = np.cos(ph), np.sin(ph)
    g_fwd = np.block([[c, s], [-s, c]])
    g_inv = np.block([[c, -s], [s, c]])
    as_bf16 = lambda a: a.astype(np.float32).astype(BF16)
    half = FFT_R // 2
    return dict(w1_full=as_bf16(w1), w1_half=as_bf16(w1[:, :, :half]), v1_half=as_bf16(v1[:, :half, :]),
                g_fwd=as_bf16(g_fwd), g_inv=as_bf16(g_inv))


def _pack_pair(x):
    h = x.shape[0] // 2
    hi = lax.bitcast_convert_type(x[:h].astype(BF16).astype(F32), jnp.uint32)
    lo = lax.bitcast_convert_type(x[h:].astype(BF16).astype(F32), jnp.uint32)
    return hi | (lo >> 16)


def _unpack_pair(p):
    hi = lax.bitcast_convert_type(p & jnp.uint32(0xFFFF0000), F32)
    lo = lax.bitcast_convert_type(p << 16, F32)
    return jnp.concatenate([hi, lo], axis=0).astype(BF16)


HALO = 16


def _in_proj_conv_kernel(x_ref, prev_ref, next_ref, g_ref, w_ref, cw_ref, cb_ref, o_ref, xn_ref):
    i = pl.program_id(0)
    tm = x_ref.shape[0]

    @pl.when(pl.program_id(1) == 0)
    def _():
        g = g_ref[...]
        before = jnp.where(i > 0, _rms(prev_ref[...], g, RMS_EPS), 0.0)
        after = jnp.where(i < pl.num_programs(0) - 1, _rms(next_ref[...], g, RMS_EPS), 0.0)
        xn_ref[:HALO] = before.astype(BF16)
        xn_ref[HALO:HALO + tm] = _rms(x_ref[...], g, RMS_EPS).astype(BF16)
        xn_ref[HALO + tm:] = after.astype(BF16)

    u = jnp.dot(xn_ref[...], w_ref[...], preferred_element_type=F32)
    y = (u[HALO - 1:HALO - 1 + tm] * cw_ref[0:1, :] + u[HALO:HALO + tm] * cw_ref[1:2, :]
         + u[HALO + 1:HALO + 1 + tm] * cw_ref[2:3, :] + cb_ref[...])
    for q in range(o_ref.shape[0]):
        o_ref[q] = y[:, q * LANE:(q + 1) * LANE]


def in_proj_conv(x, g, w, conv_w, conv_b, tm=PROJ_TM, tn=HY_IN_TN):
    s, k = x.shape
    n = w.shape[1]
    nb = tm // HALO
    return pl.pallas_call(
        _in_proj_conv_kernel,
        grid=(_steps(s, tm), _steps(n, tn)),
        in_specs=[
            pl.BlockSpec((tm, k), lambda i, j: (i, 0)),
            pl.BlockSpec((HALO, k), lambda i, j: (jnp.maximum(i * nb - 1, 0), 0)),
            pl.BlockSpec((HALO, k), lambda i, j: (jnp.minimum((i + 1) * nb, s // HALO - 1), 0)),
            pl.BlockSpec((1, k), lambda i, j: (0, 0)),
            pl.BlockSpec((k, tn), lambda i, j: (0, j)),
            pl.BlockSpec((3, tn), lambda i, j: (0, j)),
            pl.BlockSpec((1, tn), lambda i, j: (0, j)),
        ],
        out_specs=pl.BlockSpec((tn // LANE, tm, LANE), lambda i, j: (j, i, 0)),
        out_shape=jax.ShapeDtypeStruct((n // LANE, s, LANE), F32),
        scratch_shapes=[pltpu.VMEM((tm + 2 * HALO, k), BF16)],
        compiler_params=_params("parallel", "arbitrary"),
        name="hy_in_proj_conv",
    )(x, x, x, g.reshape(1, k), w, conv_w, conv_b.reshape(1, n))


def _filter_mlp_kernel(z_ref, w1_ref, b1_ref, f_ref, w2_ref, b2_ref, w3_ref, b3_ref, f3_ref, o_ref):
    hp = lax.Precision.HIGHEST
    f = f_ref[...]
    h = jnp.sin(f * (jnp.dot(z_ref[...], w1_ref[...], precision=hp, preferred_element_type=F32) + b1_ref[...]))
    h = jnp.sin(f * (jnp.dot(h, w2_ref[...], precision=hp, preferred_element_type=F32) + b2_ref[...]))
    o_ref[...] = jnp.sin(f3_ref[...] * (jnp.dot(h, w3_ref[...], precision=hp, preferred_element_type=F32) + b3_ref[...]))


def filter_mlp(zfeat, f_w1, f_b1, f_freq, f_w2, f_b2, f_w3, f_b3, tm=2048):
    n, e = zfeat.shape
    wd = f_w2.shape[0]
    row = lambda i: (i, 0)
    fixed = lambda i: (0, 0)
    vec = lambda v: v.reshape(1, wd)
    lane_pad = lambda a: jnp.pad(a, ((0, 0), (0, LANE - wd)))
    wide = pl.BlockSpec((1, LANE), fixed)
    return pl.pallas_call(
        _filter_mlp_kernel,
        grid=(_steps(n, tm),),
        in_specs=[pl.BlockSpec((tm, e), row), pl.BlockSpec((e, wd), fixed), pl.BlockSpec((1, wd), fixed),
                  pl.BlockSpec((1, wd), fixed), pl.BlockSpec((wd, wd), fixed), pl.BlockSpec((1, wd), fixed),
                  pl.BlockSpec((wd, LANE), fixed), wide, wide],
        out_specs=pl.BlockSpec((tm, LANE), row),
        out_shape=jax.ShapeDtypeStruct((n, LANE), F32),
        compiler_params=_params("parallel"),
        name="hy_filter_mlp",
    )(zfeat, f_w1, vec(f_b1), vec(f_freq), f_w2, vec(f_b2), lane_pad(f_w3), lane_pad(vec(f_b3)), lane_pad(vec(f_freq)))


def _rows_by_group(ref):
    t = [jnp.swapaxes(ref[i], 0, 1) for i in range(ref.shape[0])]
    return [jnp.concatenate([ti[r] for ti in t], axis=1) for r in range(FFT_G)]


def _store_by_group(ref, vals):
    for i in range(ref.shape[0]):
        ref[i] = jnp.swapaxes(jnp.stack([v[:, i * LANE:(i + 1) * LANE] for v in vals], axis=0), 0, 1)


def _kern_stage1_kernel(h_ref, w00_ref, w01_ref, w10_ref, w11_ref, dl_ref, w1_ref, kb_ref, nrm_ref, *, seq):
    h_rows = _rows_by_group(h_ref)
    g = pl.program_id(1)
    half = FFT_R // 2
    w_fwd = (w00_ref[...], w01_ref[...])
    w_bwd = (w10_ref[...], w11_ref[...])
    rate = dl_ref[...] * (-1.0 / (seq - 1))
    n1 = lax.broadcasted_iota(jnp.int32, (half, 1), 0)
    packed = [[] for _ in range(HY_ORDER)]
    acc = [jnp.zeros(nrm_ref.shape[1:], F32) for _ in range(HY_ORDER)]
    for r in range(FFT_G):
        n2 = g * FFT_G + r
        hs = h_rows[r].astype(BF16)
        lag_f = FFT_R * n1 + n2
        lag_b = seq - FFT_R * n1 - n2
        decay_f = jnp.exp(lag_f.astype(F32) * rate)
        decay_b = jnp.where(lag_b == seq, 0.0, jnp.exp(lag_b.astype(F32) * rate))
        for o in range(HY_ORDER):
            kf = jnp.dot(hs[:half], w_fwd[o], preferred_element_type=F32) * decay_f
            kb = jnp.dot(hs[half:], w_bwd[o], preferred_element_type=F32) * decay_b
            slab = jnp.concatenate([kf, kb], axis=0)
            acc[o] += jnp.sum(jnp.abs(slab), axis=0, keepdims=True)
            packed[o].append(_pack_pair(jnp.dot(w1_ref[r], slab.astype(BF16), preferred_element_type=F32)))
    for o in range(HY_ORDER):
        _store_by_group(kb_ref.at[o], packed[o])

    @pl.when(g == 0)
    def _():
        for o in range(HY_ORDER):
            nrm_ref[o] = acc[o]

    @pl.when(g > 0)
    def _():
        for o in range(HY_ORDER):
            nrm_ref[o] += acc[o]


def kern_stage1(h3, w4, deltas, w1_full, seq, tc=HY_TC):
    d = deltas.shape[0]
    jn = _steps(d, tc)
    h3v = h3.reshape(1, FFT_R, FFT_R, LANE)
    assert HY_ORDER == 2
    kern = functools.partial(_kern_stage1_kernel, seq=seq)
    w4_spec = lambda direction, order: pl.BlockSpec((LANE, tc), lambda j, g: (0, (direction * HY_ORDER + order) * jn + j))
    return pl.pallas_call(
        kern,
        grid=(jn, FFT_R // FFT_G),
        in_specs=[
            pl.BlockSpec((1, FFT_R, FFT_G, LANE), lambda j, g: (0, 0, g, 0)),
            w4_spec(0, 0), w4_spec(0, 1), w4_spec(1, 0), w4_spec(1, 1),
            pl.BlockSpec((1, tc), lambda j, g: (0, j)),
            pl.BlockSpec((FFT_G, 2 * FFT_K1, FFT_R), lambda j, g: (g, 0, 0)),
        ],
        out_specs=[
            pl.BlockSpec((HY_ORDER, HY_NQ, FFT_K1, FFT_G, LANE), lambda j, g: (0, j, 0, g, 0)),
            pl.BlockSpec((HY_ORDER, 1, tc), lambda j, g: (0, 0, j)),
        ],
        out_shape=[
            jax.ShapeDtypeStruct((HY_ORDER, d // LANE, FFT_K1, FFT_R, LANE), jnp.uint32),
            jax.ShapeDtypeStruct((HY_ORDER, 1, d), F32),
        ],
        compiler_params=_params("parallel", "arbitrary"),
        name="hy_kern_stage1",
    )(h3v, w4, w4, w4, w4, deltas.reshape(1, d), w1_full)


def _join_lanes(ref, idx):
    return jnp.concatenate([ref[(q,) + idx] for q in range(ref.shape[0])], axis=1)


def _kern_stage3_kernel(kb_ref, g_ref, o_ref):
    gm = g_ref[...]
    for r in range(FFT_G):
        b = _unpack_pair(_join_lanes(kb_ref.at[0], (r,)))
        o_ref[0, r] = jnp.dot(gm, b, preferred_element_type=F32).astype(o_ref.dtype)


def kern_stage3(kb, g_fwd, tc=HY_TC):
    d = kb.shape[1] * LANE
    return pl.pallas_call(
        _kern_stage3_kernel,
        grid=(HY_ORDER, FFT_K1 // FFT_G, _steps(d, tc)),
        in_specs=[
            pl.BlockSpec((1, HY_NQ, FFT_G, FFT_R, LANE), lambda o, kg, j: (o, j, kg, 0, 0)),
            pl.BlockSpec((2 * FFT_R, 2 * FFT_R), lambda o, kg, j: (0, 0)),
        ],
        out_specs=pl.BlockSpec((1, FFT_G, 2 * FFT_R, tc), lambda o, kg, j: (o, kg, 0, j)),
        out_shape=jax.ShapeDtypeStruct((HY_ORDER, FFT_K1, 2 * FFT_R, d), BF16),
        compiler_params=_params("parallel", "parallel", "parallel"),
        name="hy_kern_stage3",
    )(kb, g_fwd)


def _sig_stage1_kernel(z_ref, w1_ref, o_ref):
    z_rows = _rows_by_group(z_ref)
    _store_by_group(o_ref, [_pack_pair(jnp.dot(w1_ref[r], z_rows[r].astype(BF16), preferred_element_type=F32))
                            for r in range(FFT_G)])


def sig_stage1(z_src, slab0, d, w1_half, tc=HY_TC):
    s = z_src.shape[1]
    rows = s // FFT_R
    zv = z_src.reshape(z_src.shape[0], rows, FFT_R, LANE)
    return pl.pallas_call(
        _sig_stage1_kernel,
        grid=(FFT_R // FFT_G, _steps(d, tc)),
        in_specs=[
            pl.BlockSpec((HY_NQ, rows, FFT_G, LANE), lambda g, j: (slab0 // HY_NQ + j, 0, g, 0)),
            pl.BlockSpec((FFT_G, 2 * FFT_K1, rows), lambda g, j: (g, 0, 0)),
        ],
        out_specs=pl.BlockSpec((HY_NQ, FFT_K1, FFT_G, LANE), lambda g, j: (j, 0, g, 0)),
        out_shape=jax.ShapeDtypeStruct((d // LANE, FFT_K1, FFT_R, LANE), jnp.uint32),
        compiler_params=_params("parallel", "parallel"),
        name="hy_sig_stage1",
    )(zv, w1_half)


def _sig_stage3_kernel(zb_ref, ks_ref, gf_ref, gi_ref, o_ref):
    gf = gf_ref[...]
    gi = gi_ref[...]
    for r in range(FFT_G):
        x = jnp.dot(gf, _unpack_pair(_join_lanes(zb_ref, (r,))), preferred_element_type=F32)
        ks = ks_ref[0, r].astype(F32)
        xr, xi = x[:FFT_R], x[FFT_R:]
        kr, ki = ks[:FFT_R], ks[FFT_R:]
        y = jnp.concatenate([xr * kr - xi * ki, xr * ki + xi * kr], axis=0).astype(BF16)
        packed = _pack_pair(jnp.dot(gi, y, preferred_element_type=F32))
        for q in range(o_ref.shape[0]):
            o_ref[q, r] = packed[:, q * LANE:(q + 1) * LANE]


def sig_stage3(zb, kspec, order, g_fwd, g_inv, tc=HY_TC):
    d = zb.shape[0] * LANE
    blk = pl.BlockSpec((HY_NQ, FFT_G, FFT_R, LANE), lambda kg, j: (j, kg, 0, 0))
    mat = pl.BlockSpec((2 * FFT_R, 2 * FFT_R), lambda kg, j: (0, 0))
    return pl.pallas_call(
        _sig_stage3_kernel,
        grid=(FFT_K1 // FFT_G, _steps(d, tc)),
        in_specs=[blk, pl.BlockSpec((1, FFT_G, 2 * FFT_R, tc), lambda kg, j: (order, kg, 0, j)), mat, mat],
        out_specs=blk,
        out_shape=jax.ShapeDtypeStruct(zb.shape, jnp.uint32),
        compiler_params=_params("parallel", "parallel"),
        name="hy_sig_stage3",
    )(zb, kspec, g_fwd, g_inv)


def _sig_inverse1_kernel(yb_ref, v_ref, gate_ref, z_ref, nrm_ref, bias_ref, *rest, next_stage1):
    inv = 1.0 / nrm_ref[0]
    bias = bias_ref[0]
    yb_rows, gate_rows, z_rows = _rows_by_group(yb_ref), _rows_by_group(gate_ref), _rows_by_group(z_ref)
    out = []
    for r in range(FFT_G):
        conv = jnp.dot(v_ref[r], _unpack_pair(yb_rows[r]), preferred_element_type=F32) * inv
        out.append(gate_rows[r] * (conv + z_rows[r] * bias))
    if next_stage1:
        w1_ref, o_ref, zb_ref = rest
        _store_by_group(zb_ref, [_pack_pair(jnp.dot(w1_ref[r], out[r].astype(BF16), preferred_element_type=F32))
                                 for r in range(FFT_G)])
    else:
        o_ref, = rest
    _store_by_group(o_ref, out)


def sig_inverse1(yb, v1_half, gate_src, gate_slab0, z_src, z_slab0, nrm, bias, order, w1_half=None, tc=HY_TC):
    d = yb.shape[0] * LANE
    s = gate_src.shape[1]
    rows = s // FFT_R
    view = lambda a: a.reshape(a.shape[0], rows, FFT_R, LANE)
    sig = (HY_NQ, rows, FFT_G, LANE)
    freq = (HY_NQ, FFT_K1, FFT_G, LANE)
    in_specs = [
        pl.BlockSpec(freq, lambda g, j: (j, 0, g, 0)),
        pl.BlockSpec((FFT_G, rows, 2 * FFT_K1), lambda g, j: (g, 0, 0)),
        pl.BlockSpec(sig, lambda g, j: (gate_slab0 // HY_NQ + j, 0, g, 0)),
        pl.BlockSpec(sig, lambda g, j: (z_slab0 // HY_NQ + j, 0, g, 0)),
        pl.BlockSpec((1, 1, tc), lambda g, j: (order, 0, j)),
        pl.BlockSpec((1, 1, tc), lambda g, j: (order, 0, j)),
    ]
    args = [yb, v1_half, view(gate_src), view(z_src), nrm, bias.reshape(HY_ORDER, 1, d)]
    out_specs = [pl.BlockSpec(sig, lambda g, j: (j, 0, g, 0))]
    out_shape = [jax.ShapeDtypeStruct((d // LANE, rows, FFT_R, LANE), F32)]
    if w1_half is not None:
        in_specs.append(pl.BlockSpec((FFT_G, 2 * FFT_K1, rows), lambda g, j: (g, 0, 0)))
        args.append(w1_half)
        out_specs.append(pl.BlockSpec(freq, lambda g, j: (j, 0, g, 0)))
        out_shape.append(jax.ShapeDtypeStruct((d // LANE, FFT_K1, FFT_R, LANE), jnp.uint32))
    outs = pl.pallas_call(
        functools.partial(_sig_inverse1_kernel, next_stage1=w1_half is not None),
        grid=(FFT_R // FFT_G, _steps(d, tc)),
        in_specs=in_specs,
        out_specs=out_specs,
        out_shape=out_shape,
        compiler_params=_params("parallel", "parallel"),
        name="hy_sig_inverse1",
    )(*args)
    z_new = outs[0].reshape(d // LANE, s, LANE)
    return (z_new, outs[1]) if w1_half is not None else (z_new, None)


def _hyena_positions(seq):
    t = jnp.linspace(0.0, 1.0, seq, dtype=F32)[:, None]
    bands = (HY_EMB_DIM - 1) // 2
    w = 2.0 * math.pi * jnp.arange(seq, dtype=F32)[:, None] / seq
    f = jnp.linspace(1e-4, bands - 1, bands, dtype=F32)[None, :]
    z = jnp.concatenate([t, jnp.cos(f * w), -jnp.sin(f * w)], axis=-1)
    return jnp.pad(z, ((0, 0), (0, LANE - HY_EMB_DIM)))


def hyena_mixer(u, f_w1, f_b1, f_freq, f_w2, f_b2, f_w3, f_b3, f_w4, bias):
    s = u.shape[1]
    d = u.shape[0] * LANE // 3
    assert 2 * s == FFT_N and d % HY_TC == 0
    c = _dft_constants()
    h3 = filter_mlp(_hyena_positions(s), jnp.pad(f_w1, ((0, LANE - HY_EMB_DIM), (0, 0))), f_b1, f_freq, f_w2, f_b2, f_w3, f_b3)
    h3 = jnp.concatenate([h3, h3[:1], h3[:0:-1]], axis=0)
    w4 = jnp.pad(f_w4.astype(BF16), ((0, LANE - f_w4.shape[0]), (0, 0)))
    deltas = jnp.abs(jnp.linspace(HY_MIN_DECAY, HY_MAX_DECAY, d, dtype=F32))
    kb, nrm = kern_stage1(h3, w4, deltas, c["w1_full"], s)
    kspec = kern_stage3(kb, c["g_fwd"])
    nslab = d // LANE
    z_src, z_slab0 = u, 2 * nslab
    zb = sig_stage1(z_src, z_slab0, d, c["w1_half"])
    for o in range(HY_ORDER):
        yb = sig_stage3(zb, kspec, o, c["g_fwd"], c["g_inv"])
        w1_next = c["w1_half"] if o + 1 < HY_ORDER else None
        z_src, zb = sig_inverse1(yb, c["v1_half"], u, o * nslab, z_src, z_slab0, nrm, bias, o, w1_next)
        z_slab0 = 0
    return z_src


def hyena_layer(x, mix_norm, w_in, conv_w, conv_b, f_w1, f_b1, f_freq, f_w2, f_b2, f_w3, f_b3, f_w4, bias, w_out):
    u = in_proj_conv(x, mix_norm, w_in.astype(BF16), conv_w, conv_b)
    z = hyena_mixer(u, f_w1, f_b1, f_freq, f_w2, f_b2, f_w3, f_b3, f_w4, bias)
    return matmul_residual(z, w_out.astype(BF16), x, a_in_slabs=True)


def _rope_tables(seq, dim):
    inv = 1.0 / (ROPE_THETA ** (jnp.arange(0, dim, 2, dtype=F32) / dim))
    ang = jnp.arange(seq, dtype=F32)[:, None] * inv[None, :]
    return jnp.cos(ang), jnp.sin(ang)


def kernel(x, l0_mix_norm, l0_mla_w_in, l0_mla_q_norm, l0_mla_w_uq, l0_mla_kv_norm, l0_mla_w_ukv, l0_mla_w_o, l0_ffn_norm, l0_ffn_w_gate, l0_ffn_w_up, l0_ffn_w_down, l1_mix_norm, l1_hy_w_in, l1_hy_conv_w, l1_hy_conv_b, l1_hy_f_w1, l1_hy_f_b1, l1_hy_f_freq, l1_hy_f_w2, l1_hy_f_b2, l1_hy_f_w3, l1_hy_f_b3, l1_hy_f_w4, l1_hy_bias, l1_hy_w_out, l1_ffn_norm, l1_ffn_w_gate, l1_ffn_w_up, l1_ffn_w_down, l2_mix_norm, l2_da_w_qkv, l2_da_lq1, l2_da_lk1, l2_da_lq2, l2_da_lk2, l2_da_subln, l2_da_w_o, l2_ffn_norm, l2_ffn_w_gate, l2_ffn_w_up, l2_ffn_w_down, l3_mix_norm, l3_mla_w_in, l3_mla_q_norm, l3_mla_w_uq, l3_mla_kv_norm, l3_mla_w_ukv, l3_mla_w_o, l3_ffn_norm, l3_ffn_w_gate, l3_ffn_w_up, l3_ffn_w_down, final_norm):
    b, s, d = x.shape
    assert b == 1
    h = x.reshape(s, d)

    mc, ms = _rope_tables(s, MLA_ROPE)
    mla_tab = jnp.concatenate([mc, mc, ms, ms], axis=-1)
    mla = (mla_tab, mc.T, ms.T)
    dc, ds = _rope_tables(s, DIFF_HEAD_DIM)
    diff_rope = (jnp.concatenate([dc, dc], axis=-1), jnp.concatenate([-ds, ds], axis=-1))
    diff_rope_t = (dc.T, ds.T)

    def ffn(h, g, wg, wu, wd, out_norm_gain=None):
        return ffn_block(h, g, wg.astype(BF16), wu.astype(BF16), wd.astype(BF16), out_norm_gain)

    h = mla_layer(h, l0_mix_norm, l0_mla_w_in, l0_mla_q_norm, l0_mla_w_uq, l0_mla_kv_norm, l0_mla_w_ukv, l0_mla_w_o, *mla)
    h = ffn(h, l0_ffn_norm, l0_ffn_w_gate, l0_ffn_w_up, l0_ffn_w_down)
    h = hyena_layer(h, l1_mix_norm, l1_hy_w_in, l1_hy_conv_w, l1_hy_conv_b, l1_hy_f_w1, l1_hy_f_b1, l1_hy_f_freq, l1_hy_f_w2, l1_hy_f_b2, l1_hy_f_w3, l1_hy_f_b3, l1_hy_f_w4, l1_hy_bias, l1_hy_w_out)
    h = ffn(h, l1_ffn_norm, l1_ffn_w_gate, l1_ffn_w_up, l1_ffn_w_down)
    h = diff_layer(h, l2_mix_norm, l2_da_w_qkv, l2_da_lq1, l2_da_lk1, l2_da_lq2, l2_da_lk2, l2_da_subln, l2_da_w_o, diff_rope, diff_rope_t, 2)
    h = ffn(h, l2_ffn_norm, l2_ffn_w_gate, l2_ffn_w_up, l2_ffn_w_down)
    h = mla_layer(h, l3_mix_norm, l3_mla_w_in, l3_mla_q_norm, l3_mla_w_uq, l3_mla_kv_norm, l3_mla_w_ukv, l3_mla_w_o, *mla)
    h = ffn(h, l3_ffn_norm, l3_ffn_w_gate, l3_ffn_w_up, l3_ffn_w_down, final_norm)
    return h.reshape(b, s, d)
```
